```python
import math
import jax
import jax.numpy as jnp
from jax import lax
import numpy as np

D_MODEL = 2048
BATCH = 4
SEQ = 2048
DEPTH = 2
DEC_BATCH = 128
DEC_SEQ = 8
PAST_LEN = 16384
PAGE_SIZE = 128

N_AB = (DEPTH + 1) // 2
N_C = DEPTH // 2

GDN_HEADS = 8
GDN_DK = D_MODEL // 16
GDN_DV = D_MODEL // 16
GDN_CONV = 4
GDN_CHUNK = 64
RET_HEADS = 4
RET_DK = D_MODEL // 16
RET_DV = D_MODEL // 8
RET_CHUNK = 64
ROPE_BASE = 10000.0
S5_GROUP = 16
S5_GROUPS = D_MODEL // S5_GROUP
S5_P = 64
D_FF = ((8 * D_MODEL // 3 + 255) // 256) * 256
FFN_CONV = 3
EPS = 1e-6

GDN_QK_W = GDN_HEADS * GDN_DK
GDN_V_W = GDN_HEADS * GDN_DV
GDN_CONV_W = 2 * GDN_QK_W + GDN_V_W
RET_QK_W = RET_HEADS * RET_DK
RET_V_W = RET_HEADS * RET_DV
IN_SIZES = (GDN_CONV_W, GDN_V_W, GDN_HEADS, GDN_HEADS, RET_QK_W, RET_QK_W, RET_V_W, RET_V_W)
IN_SPLITS = tuple(int(s) for s in np.cumsum(IN_SIZES)[:-1])
D_IN = sum(IN_SIZES)
MIX_W = GDN_V_W + RET_V_W

kernel_name = 'hybrid_gdn_retention_s5_convffn_step'


def rmsnorm(x, w):
    xf = x.astype(jnp.float32)
    y = xf * lax.rsqrt(jnp.mean(xf * xf, axis=-1, keepdims=True) + EPS)
    return (y * w.astype(jnp.float32)).astype(x.dtype)


def l2norm(x):
    return x * lax.rsqrt(jnp.sum(x * x, axis=-1, keepdims=True) + EPS)


def causal_dwconv(x, buf, w):
    k_taps = w.shape[0]
    L = x.shape[1]
    xp = jnp.concatenate([buf.astype(x.dtype), x], axis=1)
    out = xp[:, 0:L] * w[0]
    for j in range(1, k_taps):
        out = out + xp[:, j:j + L] * w[j]
    return out, xp[:, L:]


def rotary(t, pos):
    half = t.shape[-1] // 2
    inv = ROPE_BASE ** (-jnp.arange(half, dtype=jnp.float32) / half)
    ang = pos[:, None] * inv[None, :]
    cos = jnp.cos(ang)[None, :, None, :]
    sin = jnp.sin(ang)[None, :, None, :]
    t1, t2 = t[..., :half], t[..., half:]
    return jnp.concatenate([t1 * cos - t2 * sin, t1 * sin + t2 * cos], axis=-1)


def to_chunks(t, c):
    b, l = t.shape[:2]
    t = t.reshape((b, l // c, c) + t.shape[2:])
    return t.transpose((1, 0, 3, 2) + tuple(range(4, t.ndim)))


def from_chunks(t):
    n, b, h, c = t.shape[:4]
    t = t.transpose((1, 0, 3, 2) + tuple(range(4, t.ndim)))
    return t.reshape((b, n * c, h) + t.shape[4:])


def gated_delta_chunked(q, k, v, log_a, beta, s0):
    L = q.shape[1]
    dv = v.shape[-1]
    c = math.gcd(L, GDN_CHUNK)
    qc, kc, vc = to_chunks(q, c), to_chunks(k, c), to_chunks(v, c)
    g = jnp.cumsum(to_chunks(log_a, c), axis=-1)
    bc = to_chunks(beta, c)
    incl = jnp.tril(jnp.ones((c, c), dtype=bool))
    strict = jnp.tril(jnp.ones((c, c), dtype=bool), -1)
    diff = g[..., :, None] - g[..., None, :]
    decay = jnp.where(incl, jnp.exp(jnp.where(incl, diff, 0.0)), 0.0)
    kk = jnp.einsum('nbhid,nbhjd->nbhij', kc, kc)
    a_mat = jnp.where(strict, bc[..., :, None] * kk * decay, 0.0) + jnp.eye(c, dtype=q.dtype)
    rhs = jnp.concatenate([vc * bc[..., None], kc * (bc * jnp.exp(g))[..., None]], axis=-1)
    sol = lax.linalg.triangular_solve(a_mat, rhs, left_side=True, lower=True, unit_diagonal=True)
    u, w = sol[..., :dv], sol[..., dv:]
    qk = jnp.einsum('nbhid,nbhjd->nbhij', qc, kc) * decay
    q_dec = qc * jnp.exp(g)[..., None]
    k_dec = kc * jnp.exp(g[..., -1:] - g)[..., None]
    g_last = jnp.exp(g[..., -1])

    def step(s, inp):
        u_n, w_n, qk_n, qd_n, kd_n, gl_n = inp
        v_new = u_n - jnp.einsum('bhcd,bhde->bhce', w_n, s)
        o = jnp.einsum('bhcd,bhde->bhce', qd_n, s) + jnp.einsum('bhij,bhje->bhie', qk_n, v_new)
        s = s * gl_n[..., None, None] + jnp.einsum('bhcd,bhce->bhde', kd_n, v_new)
        return s, o

    s_fin, o = lax.scan(step, s0, (u, w, qk, q_dec, k_dec, g_last))
    return from_chunks(o), s_fin


def retention_chunked(q, k, v, s0):
    L = q.shape[1]
    c = math.gcd(L, RET_CHUNK)
    log_g = jnp.log1p(-jnp.exp2(-5.0 - jnp.arange(RET_HEADS, dtype=jnp.float32)))
    qc, kc, vc = to_chunks(q, c), to_chunks(k, c), to_chunks(v, c)
    idx = jnp.arange(c, dtype=jnp.float32)
    diff = idx[:, None] - idx[None, :]
    causal = diff >= 0
    decay = jnp.where(causal, jnp.exp(log_g[:, None, None] * jnp.where(causal, diff, 0.0)), 0.0)
    q_dec = jnp.exp(log_g[:, None] * (idx + 1.0))[..., None]
    k_dec = jnp.exp(log_g[:, None] * (c - 1.0 - idx))[..., None]
    g_chunk = jnp.exp(log_g * c)[:, None, None]
    o = jnp.einsum('nbhij,nbhje->nbhie', jnp.einsum('nbhid,nbhjd->nbhij', qc, kc) * decay, vc)
    kv = jnp.einsum('nbhjd,nbhje->nbhde', kc * k_dec, vc)

    def step(s, kv_n):
        return s * g_chunk + kv_n, s

    s_fin, s_prev = lax.scan(step, s0, kv)
    o = o + jnp.einsum('nbhid,nbhde->nbhie', qc * q_dec, s_prev)
    return from_chunks(o), s_fin


def ab_mixer(h, pos, gdn_s, gdn_cb, ret_s, w_in, gdn_conv_w, gdn_a_log, gdn_dt_bias,
             gdn_norm_w, ret_gn_w, ret_gn_b, w_out):
    f32 = jnp.float32
    bn, L, _ = h.shape
    proj = h @ w_in
    qkv_a, z_a, a_a, b_a, q_b, k_b, v_b, g_b = jnp.split(proj, IN_SPLITS, axis=-1)
    qkv_a, new_cb = causal_dwconv(qkv_a, gdn_cb, gdn_conv_w)
    qkv_a = jax.nn.silu(qkv_a.astype(f32))
    q_a, k_a, v_a = jnp.split(qkv_a, [GDN_QK_W, 2 * GDN_QK_W], axis=-1)
    q_a = l2norm(q_a.reshape(bn, L, GDN_HEADS, GDN_DK)) * (GDN_DK ** -0.5)
    k_a = l2norm(k_a.reshape(bn, L, GDN_HEADS, GDN_DK))
    v_a = v_a.reshape(bn, L, GDN_HEADS, GDN_DV)
    log_alpha = -jnp.exp(gdn_a_log.astype(f32)) * jax.nn.softplus(a_a.astype(f32) + gdn_dt_bias.astype(f32))
    beta = jax.nn.sigmoid(b_a.astype(f32))
    o_a, s_a = gated_delta_chunked(q_a, k_a, v_a, log_alpha, beta, gdn_s.astype(f32))
    o_a = (o_a * lax.rsqrt(jnp.mean(o_a * o_a, axis=-1, keepdims=True) + EPS) * gdn_norm_w.astype(f32)
           * jax.nn.silu(z_a.astype(f32).reshape(bn, L, GDN_HEADS, GDN_DV)))
    qb = rotary(q_b.astype(f32).reshape(bn, L, RET_HEADS, RET_DK), pos)
    kb = rotary(k_b.astype(f32).reshape(bn, L, RET_HEADS, RET_DK), pos) * (RET_DK ** -0.5)
    vb = v_b.astype(f32).reshape(bn, L, RET_HEADS, RET_DV)
    o_b, s_b = retention_chunked(qb, kb, vb, ret_s.astype(f32))
    mu = jnp.mean(o_b, axis=-1, keepdims=True)
    var = jnp.mean(jnp.square(o_b - mu), axis=-1, keepdims=True)
    o_b = ((o_b - mu) * lax.rsqrt(var + EPS)).reshape(bn, L, RET_V_W)
    o_b = (o_b * ret_gn_w.astype(f32) + ret_gn_b.astype(f32)) * jax.nn.silu(g_b.astype(f32))
    mixed = jnp.concatenate([o_a.reshape(bn, L, GDN_V_W), o_b], axis=-1).astype(h.dtype)
    return mixed @ w_out, s_a, new_cb, s_b


def s5_combine(e1, e2):
    a1r, a1i, b1r, b1i = e1
    a2r, a2i, b2r, b2i = e2
    return (a2r * a1r - a2i * a1i, a2r * a1i + a2i * a1r,
            a2r * b1r - a2i * b1i + b2r, a2r * b1i + a2i * b1r + b2i)


def s5_mixer(h, h0_re, h0_im, lam_re, lam_im, log_dt, b_re, b_im, c_re, c_im, d, w_glu):
    f32 = jnp.float32
    u = h.astype(f32)
    bn, L, _ = u.shape
    ug = u.reshape(bn, L, S5_GROUPS, S5_GROUP)
    lam_re = lam_re.astype(f32)
    lam_im = lam_im.astype(f32)
    dt = jnp.exp(log_dt.astype(f32))[:, None]
    mag = jnp.exp(lam_re * dt)
    ph = lam_im * dt
    ab_re, ab_im = mag * jnp.cos(ph), mag * jnp.sin(ph)
    den = lam_re * lam_re + lam_im * lam_im
    cf_re = ((ab_re - 1.0) * lam_re + ab_im * lam_im) / den
    cf_im = (ab_im * lam_re - (ab_re - 1.0) * lam_im) / den
    b_re = b_re.astype(f32)
    b_im = b_im.astype(f32)
    bb_re = cf_re[..., None] * b_re - cf_im[..., None] * b_im
    bb_im = cf_re[..., None] * b_im + cf_im[..., None] * b_re
    bu_re = jnp.einsum('blgc,gpc->blgp', ug, bb_re)
    bu_im = jnp.einsum('blgc,gpc->blgp', ug, bb_im)
    h0_re = h0_re.astype(f32)
    h0_im = h0_im.astype(f32)
    bu_re = bu_re.at[:, 0].add(ab_re * h0_re - ab_im * h0_im)
    bu_im = bu_im.at[:, 0].add(ab_re * h0_im + ab_im * h0_re)
    a_re = jnp.broadcast_to(ab_re, bu_re.shape)
    a_im = jnp.broadcast_to(ab_im, bu_im.shape)
    _, _, x_re, x_im = lax.associative_scan(s5_combine, (a_re, a_im, bu_re, bu_im), axis=1)
    y = (jnp.einsum('blgp,gcp->blgc', x_re, c_re.astype(f32))
         - jnp.einsum('blgp,gcp->blgc', x_im, c_im.astype(f32)))
    y = y.reshape(bn, L, D_MODEL) + d.astype(f32) * u
    g = jax.nn.gelu(y).astype(h.dtype)
    val, gate = jnp.split(g @ w_glu, 2, axis=-1)
    return val * jax.nn.sigmoid(gate), x_re[:, -1], x_im[:, -1]


def conv_ffn(h, buf, w_up, conv_w, conv_b, w_down):
    up = h @ w_up
    up, new_buf = causal_dwconv(up, buf, conv_w)
    up = up + conv_b
    val, gate = jnp.split(up, 2, axis=-1)
    return (jax.nn.silu(gate) * val) @ w_down, new_buf


def trunk(x, pos, gdn_s, gdn_cb, ret_s, s5_re, s5_im, ffn_cb, p):
    gdn_new, gcb_new, ret_new, s5r_new, s5i_new, fcb_new = [], [], [], [], [], []
    for layer in range(DEPTH):
        i = layer // 2
        h = rmsnorm(x, p['norm_mix_w'][layer])
        if layer % 2 == 0:
            mix, s_a, cb_a, s_b = ab_mixer(h, pos, gdn_s[i], gdn_cb[i], ret_s[i], p['w_in'][i],
                                           p['gdn_conv_w'][i], p['gdn_a_log'][i], p['gdn_dt_bias'][i],
                                           p['gdn_norm_w'][i], p['ret_gn_w'][i], p['ret_gn_b'][i],
                                           p['w_out'][i])
            gdn_new.append(s_a)
            gcb_new.append(cb_a)
            ret_new.append(s_b)
        else:
            mix, hr, hi = s5_mixer(h, s5_re[i], s5_im[i], p['s5_lam_re'][i], p['s5_lam_im'][i],
                                   p['s5_log_dt'][i], p['s5_b_re'][i], p['s5_b_im'][i],
                                   p['s5_c_re'][i], p['s5_c_im'][i], p['s5_d'][i], p['w_glu'][i])
            s5r_new.append(hr)
            s5i_new.append(hi)
        x = x + mix.astype(x.dtype)
        h = rmsnorm(x, p['norm_ffn_w'][layer])
        f, cb = conv_ffn(h, ffn_cb[layer], p['w_up'][layer], p['ffn_conv_w'][layer],
                         p['ffn_conv_b'][layer], p['w_down'][layer])
        fcb_new.append(cb)
        x = x + f.astype(x.dtype)
    y = rmsnorm(x, p['norm_final_w'])
    return (y, jnp.stack(gdn_new), jnp.stack(gcb_new), jnp.stack(ret_new),
            jnp.stack(s5r_new), jnp.stack(s5i_new), jnp.stack(fcb_new))


def setup_inputs(seed: int = 0) -> dict:
    key = jax.random.key(seed)
    ks = iter(jax.random.split(key, 40))
    f32 = jnp.float32

    def nrm(shape, scale):
        return scale * jax.random.normal(next(ks), shape, f32)

    x_prompt = nrm((BATCH, SEQ, D_MODEL), 1.0)
    x_sample = nrm((DEC_BATCH, DEC_SEQ, D_MODEL), 1.0)
    state_gdn = nrm((N_AB, DEC_BATCH, GDN_HEADS, GDN_DK, GDN_DV), 0.1)
    state_gdn_conv = nrm((N_AB, DEC_BATCH, GDN_CONV - 1, GDN_CONV_W), 1.0)
    state_ret = nrm((N_AB, DEC_BATCH, RET_HEADS, RET_DK, RET_DV), 1.0)
    state_s5_re = nrm((N_C, DEC_BATCH, S5_GROUPS, S5_P), 0.3)
    state_s5_im = nrm((N_C, DEC_BATCH, S5_GROUPS, S5_P), 0.3)
    state_ffn_conv = nrm((DEPTH, DEC_BATCH, FFN_CONV - 1, 2 * D_FF), 1.0)
    norm_mix_w = 1.0 + nrm((DEPTH, D_MODEL), 0.02)
    norm_ffn_w = 1.0 + nrm((DEPTH, D_MODEL), 0.02)
    norm_final_w = 1.0 + nrm((D_MODEL,), 0.02)
    w_in = nrm((N_AB, D_MODEL, D_IN), D_MODEL ** -0.5)
    gdn_conv_w = nrm((N_AB, GDN_CONV, GDN_CONV_W), GDN_CONV ** -0.5)
    gdn_a_log = jnp.log(jax.random.uniform(next(ks), (N_AB, GDN_HEADS), f32, 1.0, 16.0))
    dt = jnp.exp(jax.random.uniform(next(ks), (N_AB, GDN_HEADS), f32, math.log(1e-3), math.log(1e-1)))
    gdn_dt_bias = dt + jnp.log(-jnp.expm1(-dt))
    gdn_norm_w = 1.0 + nrm((N_AB, GDN_DV), 0.02)
    ret_gn_w = 1.0 + nrm((N_AB, RET_V_W), 0.02)
    ret_gn_b = nrm((N_AB, RET_V_W), 0.02)
    w_out = nrm((N_AB, MIX_W, D_MODEL), MIX_W ** -0.5)
    s5_lam_re = -0.5 + nrm((N_C, S5_GROUPS, S5_P), 0.01)
    s5_lam_im = math.pi * jnp.arange(S5_P, dtype=f32) + nrm((N_C, S5_GROUPS, S5_P), 0.01)
    s5_log_dt = jax.random.uniform(next(ks), (N_C, S5_GROUPS), f32, math.log(1e-3), math.log(1e-1))
    s5_b_re = nrm((N_C, S5_GROUPS, S5_P, S5_GROUP), (2 * S5_GROUP) ** -0.5)
    s5_b_im = nrm((N_C, S5_GROUPS, S5_P, S5_GROUP), (2 * S5_GROUP) ** -0.5)
    s5_c_re = nrm((N_C, S5_GROUPS, S5_GROUP, S5_P), S5_P ** -0.5)
    s5_c_im = nrm((N_C, S5_GROUPS, S5_GROUP, S5_P), S5_P ** -0.5)
    s5_d = nrm((N_C, D_MODEL), 1.0)
    w_glu = nrm((N_C, D_MODEL, 2 * D_MODEL), D_MODEL ** -0.5)
    w_up = nrm((DEPTH, D_MODEL, 2 * D_FF), D_MODEL ** -0.5)
    ffn_conv_w = nrm((DEPTH, FFN_CONV, 2 * D_FF), FFN_CONV ** -0.5)
    ffn_conv_b = nrm((DEPTH, 2 * D_FF), 0.01)
    w_down = nrm((DEPTH, D_FF, D_MODEL), D_FF ** -0.5)
    return {'x_prompt': x_prompt, 'x_sample': x_sample,
            'state_gdn': state_gdn, 'state_gdn_conv': state_gdn_conv, 'state_ret': state_ret,
            'state_s5_re': state_s5_re, 'state_s5_im': state_s5_im, 'state_ffn_conv': state_ffn_conv,
            'norm_mix_w': norm_mix_w, 'norm_ffn_w': norm_ffn_w, 'norm_final_w': norm_final_w,
            'w_in': w_in, 'gdn_conv_w': gdn_conv_w, 'gdn_a_log': gdn_a_log, 'gdn_dt_bias': gdn_dt_bias,
            'gdn_norm_w': gdn_norm_w, 'ret_gn_w': ret_gn_w, 'ret_gn_b': ret_gn_b, 'w_out': w_out,
            's5_lam_re': s5_lam_re, 's5_lam_im': s5_lam_im, 's5_log_dt': s5_log_dt,
            's5_b_re': s5_b_re, 's5_b_im': s5_b_im, 's5_c_re': s5_c_re, 's5_c_im': s5_c_im,
            's5_d': s5_d, 'w_glu': w_glu,
            'w_up': w_up, 'ffn_conv_w': ffn_conv_w, 'ffn_conv_b': ffn_conv_b, 'w_down': w_down}


def reference(x_prompt, x_sample, state_gdn, state_gdn_conv, state_ret, state_s5_re, state_s5_im,
              state_ffn_conv, norm_mix_w, norm_ffn_w, norm_final_w,
              w_in, gdn_conv_w, gdn_a_log, gdn_dt_bias, gdn_norm_w, ret_gn_w, ret_gn_b, w_out,
              s5_lam_re, s5_lam_im, s5_log_dt, s5_b_re, s5_b_im, s5_c_re, s5_c_im, s5_d, w_glu,
              w_up, ffn_conv_w, ffn_conv_b, w_down):
    f32 = jnp.float32
    p = dict(norm_mix_w=norm_mix_w, norm_ffn_w=norm_ffn_w, norm_final_w=norm_final_w,
             w_in=w_in, gdn_conv_w=gdn_conv_w, gdn_a_log=gdn_a_log, gdn_dt_bias=gdn_dt_bias,
             gdn_norm_w=gdn_norm_w, ret_gn_w=ret_gn_w, ret_gn_b=ret_gn_b, w_out=w_out,
             s5_lam_re=s5_lam_re, s5_lam_im=s5_lam_im, s5_log_dt=s5_log_dt,
             s5_b_re=s5_b_re, s5_b_im=s5_b_im, s5_c_re=s5_c_re, s5_c_im=s5_c_im,
             s5_d=s5_d, w_glu=w_glu, w_up=w_up, ffn_conv_w=ffn_conv_w,
             ffn_conv_b=ffn_conv_b, w_down=w_down)
    bp = x_prompt.shape[0]
    z_gdn = jnp.zeros((N_AB, bp, GDN_HEADS, GDN_DK, GDN_DV), f32)
    z_gcb = jnp.zeros((N_AB, bp, GDN_CONV - 1, GDN_CONV_W), x_prompt.dtype)
    z_ret = jnp.zeros((N_AB, bp, RET_HEADS, RET_DK, RET_DV), f32)
    z_s5 = jnp.zeros((N_C, bp, S5_GROUPS, S5_P), f32)
    z_fcb = jnp.zeros((DEPTH, bp, FFN_CONV - 1, 2 * D_FF), x_prompt.dtype)
    pos_p = jnp.arange(x_prompt.shape[1], dtype=f32)
    pos_s = PAST_LEN + jnp.arange(x_sample.shape[1], dtype=f32)
    y_prompt, gdn_p, gcb_p, ret_p, s5r_p, s5i_p, fcb_p = trunk(
        x_prompt, pos_p, z_gdn, z_gcb, z_ret, z_s5, z_s5, z_fcb, p)
    y_sample, gdn_s, gcb_s, ret_s, s5r_s, s5i_s, fcb_s = trunk(
        x_sample, pos_s, state_gdn, state_gdn_conv, state_ret, state_s5_re, state_s5_im,
        state_ffn_conv, p)
    return (y_prompt, y_sample, gdn_p, gdn_s, gcb_p, gcb_s, ret_p, ret_s,
            s5r_p, s5r_s, s5i_p, s5i_s, fcb_p, fcb_s)
```

```python
import functools
import math

import jax
import jax.numpy as jnp
from jax import lax
from jax.experimental import pallas as pl
from jax.experimental.pallas import tpu as pltpu

F32 = jnp.float32
BF16 = jnp.bfloat16

D_MODEL = 2048
GDN_HEADS = 8
GDN_DK = 128
GDN_DV = 128
GDN_CONV = 4
RET_HEADS = 4
RET_DK = 128
RET_DV = 256
ROPE_BASE = 10000.0
S5_GROUP = 16
S5_GROUPS = 128
S5_P = 64
S5_STATE = S5_GROUPS * S5_P
D_FF = 5632
FFN_CONV = 3
EPS = 1e-6

GDN_QK_W = GDN_HEADS * GDN_DK
GDN_V_W = GDN_HEADS * GDN_DV
GDN_CONV_W = 2 * GDN_QK_W + GDN_V_W
RET_QK_W = RET_HEADS * RET_DK
RET_V_W = RET_HEADS * RET_DV
PROJ_W = GDN_CONV_W + GDN_V_W + 2 * RET_QK_W + 2 * RET_V_W
GATE_W = 128

CHUNK_ROWS = 64
LANES = 128
VMEM_LIMIT = 56 * 1024 * 1024

S5_GB = 8
S5_NB = S5_GROUPS // S5_GB
S5_BW = S5_GB * S5_P
S5_SCAN_W = 1024
S5_SCAN_LB = S5_SCAN_W // LANES
GELU_C = math.sqrt(2.0 / math.pi)


def _params(n_grid):
    return pltpu.CompilerParams(dimension_semantics=("arbitrary",) * n_grid,
                                vmem_limit_bytes=VMEM_LIMIT)


def _dot(a, b):
    return jnp.dot(a, b, preferred_element_type=F32)


def _dot_nt(a, b):
    return lax.dot_general(a, b, (((1,), (1,)), ((), ())), preferred_element_type=F32)


def _dot_tn(a, b):
    return lax.dot_general(a, b, (((0,), (0,)), ((), ())), preferred_element_type=F32)


def _split2(a):
    hi = a.astype(BF16)
    lo = (a - hi.astype(F32)).astype(BF16)
    return hi, lo


def _dot_split(a_parts, b_parts):
    ah, al = a_parts
    bh, bl = b_parts
    return _dot(ah, bh) + (_dot(ah, bl) + _dot(al, bh))


def _sigmoid(x):
    return 1.0 / (1.0 + jnp.exp(-x))


def _silu(x):
    return x * _sigmoid(x)


def _rms_rows(x, w):
    return x * lax.rsqrt(jnp.mean(x * x, axis=-1, keepdims=True) + EPS) * w


def _shift_rows(x, k, tpos, buf, nseq, seg):
    rows, width = x.shape
    nb = buf.shape[1]
    prev = pltpu.roll(x, k, 0)
    for t in range(k):
        src = buf[:, nb - k + t:nb - k + t + 1, :]
        srcb = jnp.broadcast_to(src, (nseq, seg, width)).reshape(rows, width)
        prev = jnp.where(tpos == t, srcb, prev)
    return prev


def _seq_masks(rows, seg):
    shift = int(math.log2(seg))
    ri = lax.broadcasted_iota(jnp.int32, (rows, rows), 0)
    ci = lax.broadcasted_iota(jnp.int32, (rows, rows), 1)
    same = (ri >> shift) == (ci >> shift)
    incl = same & (ci <= ri)
    strict = same & (ci < ri)
    return same, incl, strict


def _in_proj_kernel(x_ref, nw_ref, w_ref, wg_ref, o_ref, og_ref, h_scr):
    @pl.when(pl.program_id(1) == 0)
    def _():
        hb = _rms_rows(x_ref[...], nw_ref[...]).astype(BF16)
        h_scr[...] = hb
        og_ref[...] = _dot(hb, wg_ref[...])

    o_ref[...] = _dot(h_scr[...], w_ref[...])


def _in_proj(x, nw, w_main, w_gate, tm=512, tn=1024):
    m = x.shape[0]
    return pl.pallas_call(
        _in_proj_kernel,
        grid=(m // tm, PROJ_W // tn),
        in_specs=[
            pl.BlockSpec((tm, D_MODEL), lambda i, j: (i, 0)),
            pl.BlockSpec((1, D_MODEL), lambda i, j: (0, 0)),
            pl.BlockSpec((D_MODEL, tn), lambda i, j: (0, j)),
            pl.BlockSpec((D_MODEL, GATE_W), lambda i, j: (0, 0)),
        ],
        out_specs=[
            pl.BlockSpec((tm, tn), lambda i, j: (i, j)),
            pl.BlockSpec((tm, GATE_W), lambda i, j: (i, 0)),
        ],
        out_shape=[jax.ShapeDtypeStruct((m, PROJ_W), F32),
                   jax.ShapeDtypeStruct((m, GATE_W), F32)],
        scratch_shapes=[pltpu.VMEM((tm, D_MODEL), BF16)],
        compiler_params=_params(2),
        name="in_proj",
    )(x, nw, w_main, w_gate)


def _gdn_kernel(*refs, nseq, seg, chained):
    if chained:
        (qkv_ref, z_ref, gate_ref, cbuf_ref, cw_ref, alog_ref, dtb_ref, nw_ref,
         o_ref, s_out_ref, cb_out_ref, carry_scr) = refs
        s_in_ref = s_out_ref
        first = pl.program_id(1) == 0

        @pl.when(first)
        def _():
            s_out_ref[...] = jnp.zeros_like(s_out_ref)
    else:
        (qkv_ref, z_ref, gate_ref, cbuf_ref, cw_ref, alog_ref, dtb_ref, nw_ref, s_in_ref,
         o_ref, s_out_ref, cb_out_ref) = refs

    rows = nseq * seg
    shift = int(math.log2(seg))
    x = qkv_ref[...]
    if chained:
        buf = jnp.where(first, cbuf_ref[...], carry_scr[5:8, :].reshape(1, GDN_CONV - 1, GDN_CONV_W))
    else:
        buf = cbuf_ref[...]
    row_id = lax.broadcasted_iota(jnp.int32, (rows, 1), 0)
    tpos = row_id & (seg - 1)

    acc = x * cw_ref[GDN_CONV - 1:GDN_CONV, :]
    for k in range(1, GDN_CONV):
        acc = acc + _shift_rows(x, k, tpos, buf, nseq, seg) * cw_ref[GDN_CONV - 1 - k:GDN_CONV - k, :]
    act = _silu(acc)
    cb_out_ref[...] = x.reshape(nseq, seg, GDN_CONV_W)[:, seg - (GDN_CONV - 1):, :]
    if chained:
        carry_scr[...] = x[rows - 8:, :]

    gate = gate_ref[...]
    xa = gate + dtb_ref[...]
    softplus = jnp.maximum(xa, 0.0) + jnp.log(1.0 + jnp.exp(-jnp.abs(xa)))
    log_a = -jnp.exp(alog_ref[...]) * softplus
    beta_t = _sigmoid(gate)

    same, incl, strict = _seq_masks(rows, seg)
    sum_m = jnp.concatenate([incl.astype(F32), same.astype(F32)], axis=0).astype(BF16)
    a1 = log_a.astype(BF16)
    r1 = log_a - a1.astype(F32)
    a2 = r1.astype(BF16)
    a3 = (r1 - a2.astype(F32)).astype(BF16)
    g_all = _dot(sum_m, a1) + (_dot(sum_m, a2) + _dot(sum_m, a3))
    g = g_all[:rows]
    g_last = g_all[rows:]
    g_t = g.T
    e_g = jnp.exp(g)
    e_rest = jnp.exp(g_last - g)
    e_last = jnp.exp(g_last)
    eye = (lax.broadcasted_iota(jnp.int32, (rows, rows), 0)
           == lax.broadcasted_iota(jnp.int32, (rows, rows), 1)).astype(F32)
    if nseq > 1:
        row_masks = [((row_id >> shift) == s).astype(F32) for s in range(nseq)]
    nw = nw_ref[...]

    for h in range(GDN_HEADS):
        lo = h * GDN_DK
        q = act[:, lo:lo + GDN_DK]
        k = act[:, GDN_QK_W + lo:GDN_QK_W + lo + GDN_DK]
        v = act[:, 2 * GDN_QK_W + lo:2 * GDN_QK_W + lo + GDN_DV]
        q = q * lax.rsqrt(jnp.sum(q * q, axis=-1, keepdims=True) + EPS) * (GDN_DK ** -0.5)
        k = k * lax.rsqrt(jnp.sum(k * k, axis=-1, keepdims=True) + EPS)
        gc = g[:, h:h + 1]
        gr = g_t[h:h + 1, :]
        decay = jnp.where(incl, jnp.exp(jnp.where(incl, gc - gr, 0.0)), 0.0)
        bc = beta_t[:, GDN_HEADS + h:GDN_HEADS + h + 1]
        egc = e_g[:, h:h + 1]
        kb = k.astype(BF16)
        qb = q.astype(BF16)
        kk = _dot_nt(kb, kb)
        qk = _dot_nt(qb, kb) * decay
        m_parts = _split2(-jnp.where(strict, bc * kk * decay, 0.0))
        inv = eye + m_parts[0].astype(F32) + m_parts[1].astype(F32)
        for _ in range(shift - 1):
            m_parts = _split2(_dot_split(m_parts, m_parts))
            inv = inv + _dot_split(_split2(inv), m_parts)
        rhs = jnp.concatenate([v * bc, k * (bc * egc)], axis=-1)
        sol = _dot_split(_split2(inv), _split2(rhs))
        u = sol[:, :GDN_DV]
        w = sol[:, GDN_DV:]
        lhs = jnp.concatenate([w, q * egc], axis=0).astype(BF16)
        k_rest = (k * e_rest[:, h:h + 1]).astype(BF16)

        states = [s_in_ref[s, h] for s in range(nseq)]
        ws = None
        for s in range(nseq):
            part = _dot(lhs, states[s].astype(BF16))
            if nseq > 1:
                part = part * jnp.concatenate([row_masks[s], row_masks[s]], axis=0)
            ws = part if ws is None else ws + part
        v_new = u - ws[:rows]
        o = ws[rows:] + _dot(qk.astype(BF16), v_new.astype(BF16))
        for s in range(nseq):
            vs = v_new * row_masks[s] if nseq > 1 else v_new
            s_out_ref[s, h] = (states[s] * e_last[s * seg:s * seg + 1, h:h + 1]
                               + _dot_tn(k_rest, vs.astype(BF16)))
        o = o * lax.rsqrt(jnp.mean(o * o, axis=-1, keepdims=True) + EPS) * nw
        o = o * _silu(z_ref[:, lo:lo + GDN_DV])
        o_ref[:, lo:lo + GDN_DV] = o.astype(BF16)


def _gdn(proj, gate, cbuf, conv_w, alog_row, dtb_row, norm_w, state, *, nbatch, nchunk):
    chained = state is None
    rows_total = proj.shape[0]
    if chained:
        nseq, seg = 1, CHUNK_ROWS
        grid = (nbatch, nchunk)
        rb = lambda b, n: b * nchunk + n
        sb = lambda b, n: b
    else:
        nseq, seg = 8, 8
        grid = (rows_total // CHUNK_ROWS,)
        rb = lambda i: i
        sb = lambda i: i
    n_state = nbatch
    wrap = lambda f, *tail: (lambda *g: (f(*g),) + tail)
    const = lambda *tail: (lambda *g: tail)
    in_specs = [
        pl.BlockSpec((CHUNK_ROWS, GDN_CONV_W), wrap(rb, 0)),
        pl.BlockSpec((CHUNK_ROWS, GDN_V_W), wrap(rb, GDN_CONV_W // GDN_V_W)),
        pl.BlockSpec((CHUNK_ROWS, GATE_W), wrap(rb, 0)),
        pl.BlockSpec((nseq, GDN_CONV - 1, GDN_CONV_W), wrap(sb, 0, 0)),
        pl.BlockSpec((GDN_CONV, GDN_CONV_W), const(0, 0)),
        pl.BlockSpec((1, GATE_W), const(0, 0)),
        pl.BlockSpec((1, GATE_W), const(0, 0)),
        pl.BlockSpec((1, GDN_DV), const(0, 0)),
    ]
    args = [proj, proj, gate, cbuf, conv_w, alog_row, dtb_row, norm_w]
    scratch = []
    if chained:
        scratch = [pltpu.VMEM((8, GDN_CONV_W), F32)]
    else:
        in_specs.append(pl.BlockSpec((nseq, GDN_HEADS, GDN_DK, GDN_DV), wrap(sb, 0, 0, 0)))
        args.append(state)
    out_specs = [
        pl.BlockSpec((CHUNK_ROWS, GDN_V_W), wrap(rb, 0)),
        pl.BlockSpec((nseq, GDN_HEADS, GDN_DK, GDN_DV), wrap(sb, 0, 0, 0)),
        pl.BlockSpec((nseq, GDN_CONV - 1, GDN_CONV_W), wrap(sb, 0, 0)),
    ]
    out_shape = [
        jax.ShapeDtypeStruct((rows_total, GDN_V_W), BF16),
        jax.ShapeDtypeStruct((n_state, GDN_HEADS, GDN_DK, GDN_DV), F32),
        jax.ShapeDtypeStruct((n_state, GDN_CONV - 1, GDN_CONV_W), F32),
    ]
    return pl.pallas_call(
        functools.partial(_gdn_kernel, nseq=nseq, seg=seg, chained=chained),
        grid=grid, in_specs=in_specs, out_specs=out_specs, out_shape=out_shape,
        scratch_shapes=scratch, compiler_params=_params(len(grid)),
        name="gdn_chained" if chained else "gdn_step",
    )(*args)


def _ret_kernel(*refs, nseq, seg, chained, g_chunk):
    if chained:
        (q_ref, k_ref, v_ref, gt_ref, cos_ref, sin_ref, dec_ref, qd_ref, kd_ref, gw_ref, gb_ref,
         o_ref, s_out_ref) = refs
        s_in_ref = s_out_ref

        @pl.when(pl.program_id(1) == 0)
        def _():
            s_out_ref[...] = jnp.zeros_like(s_out_ref)
    else:
        (q_ref, k_ref, v_ref, gt_ref, cos_ref, sin_ref, dec_ref, qd_ref, kd_ref, gw_ref, gb_ref,
         s_in_ref, o_ref, s_out_ref) = refs

    rows = nseq * seg
    shift = int(math.log2(seg))
    cosf = cos_ref[...]
    sinf = sin_ref[...]
    if nseq > 1:
        row_id = lax.broadcasted_iota(jnp.int32, (rows, 1), 0)
        row_masks = [((row_id >> shift) == s).astype(F32) for s in range(nseq)]

    for h in range(RET_HEADS):
        lo = h * RET_DK
        vlo = h * RET_DV
        q = q_ref[:, lo:lo + RET_DK]
        k = k_ref[:, lo:lo + RET_DK]
        q = q * cosf + pltpu.roll(q, RET_DK // 2, 1) * sinf
        k = (k * cosf + pltpu.roll(k, RET_DK // 2, 1) * sinf) * (RET_DK ** -0.5)
        vb = v_ref[:, vlo:vlo + RET_DV].astype(BF16)
        kb = k.astype(BF16)
        sc = _dot_nt(q.astype(BF16), kb) * dec_ref[h]
        o = _dot(sc.astype(BF16), vb)
        q_dec = (q * qd_ref[h]).astype(BF16)
        k_dec = (k * kd_ref[h]).astype(BF16)
        for s in range(nseq):
            state = s_in_ref[s, h]
            part = _dot(q_dec, state.astype(BF16))
            if nseq > 1:
                part = part * row_masks[s]
                vs = (v_ref[:, vlo:vlo + RET_DV] * row_masks[s]).astype(BF16)
            else:
                vs = vb
            o = o + part
            s_out_ref[s, h] = state * g_chunk[h] + _dot_tn(k_dec, vs)
        mu = jnp.mean(o, axis=-1, keepdims=True)
        oc = o - mu
        var = jnp.mean(oc * oc, axis=-1, keepdims=True)
        on = oc * lax.rsqrt(var + EPS)
        on = (on * gw_ref[:, vlo:vlo + RET_DV] + gb_ref[:, vlo:vlo + RET_DV]) * _silu(gt_ref[:, vlo:vlo + RET_DV])
        o_ref[:, vlo:vlo + RET_DV] = on.astype(BF16)


def _ret(proj, cosf, sinf, dec, qd, kd, gw, gb, state, g_chunk, *, nbatch, nchunk):
    chained = state is None
    rows_total = proj.shape[0]
    if chained:
        nseq, seg = 1, CHUNK_ROWS
        grid = (nbatch, nchunk)
        rb = lambda b, n: b * nchunk + n
        sb = lambda b, n: b
        pb = lambda b, n: n
    else:
        nseq, seg = 8, 8
        grid = (rows_total // CHUNK_ROWS,)
        rb = lambda i: i
        sb = lambda i: i
        pb = lambda i: 0
    wrap = lambda f, *tail: (lambda *g: (f(*g),) + tail)
    const = lambda *tail: (lambda *g: tail)
    q_col = (GDN_CONV_W + GDN_V_W) // RET_QK_W
    v_col = (GDN_CONV_W + GDN_V_W + 2 * RET_QK_W) // RET_V_W
    in_specs = [
        pl.BlockSpec((CHUNK_ROWS, RET_QK_W), wrap(rb, q_col)),
        pl.BlockSpec((CHUNK_ROWS, RET_QK_W), wrap(rb, q_col + 1)),
        pl.BlockSpec((CHUNK_ROWS, RET_V_W), wrap(rb, v_col)),
        pl.BlockSpec((CHUNK_ROWS, RET_V_W), wrap(rb, v_col + 1)),
        pl.BlockSpec((CHUNK_ROWS, RET_DK), wrap(pb, 0)),
        pl.BlockSpec((CHUNK_ROWS, RET_DK), wrap(pb, 0)),
        pl.BlockSpec((RET_HEADS, CHUNK_ROWS, CHUNK_ROWS), const(0, 0, 0)),
        pl.BlockSpec((RET_HEADS, CHUNK_ROWS, RET_DK), const(0, 0, 0)),
        pl.BlockSpec((RET_HEADS, CHUNK_ROWS, RET_DK), const(0, 0, 0)),
        pl.BlockSpec((1, RET_V_W), const(0, 0)),
        pl.BlockSpec((1, RET_V_W), const(0, 0)),
    ]
    args = [proj, proj, proj, proj, cosf, sinf, dec, qd, kd, gw, gb]
    if not chained:
        in_specs.append(pl.BlockSpec((nseq, RET_HEADS, RET_DK, RET_DV), wrap(sb, 0, 0, 0)))
        args.append(state)
    out_specs = [
        pl.BlockSpec((CHUNK_ROWS, RET_V_W), wrap(rb, 0)),
        pl.BlockSpec((nseq, RET_HEADS, RET_DK, RET_DV), wrap(sb, 0, 0, 0)),
    ]
    out_shape = [
        jax.ShapeDtypeStruct((rows_total, RET_V_W), BF16),
        jax.ShapeDtypeStruct((nbatch, RET_HEADS, RET_DK, RET_DV), F32),
    ]
    return pl.pallas_call(
        functools.partial(_ret_kernel, nseq=nseq, seg=seg, chained=chained, g_chunk=g_chunk),
        grid=grid, in_specs=in_specs, out_specs=out_specs, out_shape=out_shape,
        compiler_params=_params(len(grid)),
        name="ret_chained" if chained else "ret_step",
    )(*args)


def _out_proj_kernel(x_ref, a_ref, b_ref, wa_ref, wb_ref, o_ref):
    o_ref[...] = x_ref[...] + (_dot(a_ref[...], wa_ref[...]) + _dot(b_ref[...], wb_ref[...]))


def _out_proj(x, o_a, o_b, w_a, w_b, tm=512):
    m = x.shape[0]
    return pl.pallas_call(
        _out_proj_kernel,
        grid=(m // tm,),
        in_specs=[
            pl.BlockSpec((tm, D_MODEL), lambda i: (i, 0)),
            pl.BlockSpec((tm, GDN_V_W), lambda i: (i, 0)),
            pl.BlockSpec((tm, RET_V_W), lambda i: (i, 0)),
            pl.BlockSpec((GDN_V_W, D_MODEL), lambda i: (0, 0)),
            pl.BlockSpec((RET_V_W, D_MODEL), lambda i: (0, 0)),
        ],
        out_specs=pl.BlockSpec((tm, D_MODEL), lambda i: (i, 0)),
        out_shape=jax.ShapeDtypeStruct((m, D_MODEL), F32),
        compiler_params=_params(1),
        name="out_proj",
    )(x, o_a, o_b, w_a, w_b)


def _ffn_kernel(*refs, nseq, seg, tiles_per_seq, final_norm):
    (x_ref, nw_ref, wv_ref, wg_ref, cwv_ref, cwg_ref, bv_ref, bg_ref, wd_ref, bufv_ref, bufg_ref,
     fw_ref, o_ref, nbv_ref, nbg_ref, h_scr, acc_scr) = refs[:17]
    chained = tiles_per_seq > 1
    i = pl.program_id(0)
    j = pl.program_id(1)
    nf = pl.num_programs(1)
    rows = nseq * seg

    @pl.when(j == 0)
    def _():
        h_scr[...] = _rms_rows(x_ref[...], nw_ref[...]).astype(BF16)
        acc_scr[...] = jnp.zeros_like(acc_scr)

    tpos = lax.broadcasted_iota(jnp.int32, (rows, 1), 0) & (seg - 1)
    hb = h_scr[...]

    def branch(w_ref, cw_ref, b_ref, buf_ref, nb_ref, carry_ref):
        up = _dot(hb, w_ref[...])
        if chained:
            buf = jnp.where(i % tiles_per_seq == 0, buf_ref[...], carry_ref[j][None, 6:8, :])
        else:
            buf = buf_ref[...]
        out = up * cw_ref[FFN_CONV - 1:FFN_CONV, :] + b_ref[...]
        for k in range(1, FFN_CONV):
            out = out + _shift_rows(up, k, tpos, buf, nseq, seg) * cw_ref[FFN_CONV - 1 - k:FFN_CONV - k, :]
        nb_ref[...] = up.reshape(nseq, seg, up.shape[1])[:, seg - (FFN_CONV - 1):, :]
        if chained:
            carry_ref[j] = up[rows - 8:, :]
        return out

    if chained:
        cv_scr, cg_scr = refs[17], refs[18]
    else:
        cv_scr = cg_scr = None
    val = branch(wv_ref, cwv_ref, bv_ref, bufv_ref, nbv_ref, cv_scr)
    gate = branch(wg_ref, cwg_ref, bg_ref, bufg_ref, nbg_ref, cg_scr)
    act = (_silu(gate) * val).astype(BF16)
    acc_scr[...] += _dot(act, wd_ref[...])

    @pl.when(j == nf - 1)
    def _():
        y = x_ref[...] + acc_scr[...]
        if final_norm:
            y = _rms_rows(y, fw_ref[...])
        o_ref[...] = y


def _ffn(x, nw, w_up, conv_w, conv_b, w_down, buf, final_w, *, nseq_total, final_norm, tm=512, tf=512):
    m = x.shape[0]
    seq_len = m // nseq_total
    if seq_len >= tm:
        nseq, seg, tiles_per_seq = 1, tm, seq_len // tm
    else:
        nseq, seg, tiles_per_seq = tm // seq_len, seq_len, 1
    nf = D_FF // tf
    sidx = (lambda i, j: i // tiles_per_seq) if tiles_per_seq > 1 else (lambda i, j: i)
    in_specs = [
        pl.BlockSpec((tm, D_MODEL), lambda i, j: (i, 0)),
        pl.BlockSpec((1, D_MODEL), lambda i, j: (0, 0)),
        pl.BlockSpec((D_MODEL, tf), lambda i, j: (0, j)),
        pl.BlockSpec((D_MODEL, tf), lambda i, j: (0, nf + j)),
        pl.BlockSpec((FFN_CONV, tf), lambda i, j: (0, j)),
        pl.BlockSpec((FFN_CONV, tf), lambda i, j: (0, nf + j)),
        pl.BlockSpec((1, tf), lambda i, j: (0, j)),
        pl.BlockSpec((1, tf), lambda i, j: (0, nf + j)),
        pl.BlockSpec((tf, D_MODEL), lambda i, j: (j, 0)),
        pl.BlockSpec((nseq, FFN_CONV - 1, tf), lambda i, j: (sidx(i, j), 0, j)),
        pl.BlockSpec((nseq, FFN_CONV - 1, tf), lambda i, j: (sidx(i, j), 0, nf + j)),
        pl.BlockSpec((1, D_MODEL), lambda i, j: (0, 0)),
    ]
    out_specs = [
        pl.BlockSpec((tm, D_MODEL), lambda i, j: (i, 0)),
        pl.BlockSpec((nseq, FFN_CONV - 1, tf), lambda i, j: (i, 0, j)),
        pl.BlockSpec((nseq, FFN_CONV - 1, tf), lambda i, j: (i, 0, j)),
    ]
    out_shape = [
        jax.ShapeDtypeStruct((m, D_MODEL), F32),
        jax.ShapeDtypeStruct((m // tm * nseq, FFN_CONV - 1, D_FF), F32),
        jax.ShapeDtypeStruct((m // tm * nseq, FFN_CONV - 1, D_FF), F32),
    ]
    scratch = [pltpu.VMEM((tm, D_MODEL), BF16), pltpu.VMEM((tm, D_MODEL), F32)]
    if tiles_per_seq > 1:
        scratch += [pltpu.VMEM((nf, 8, tf), F32), pltpu.VMEM((nf, 8, tf), F32)]
    y, nbv, nbg = pl.pallas_call(
        functools.partial(_ffn_kernel, nseq=nseq, seg=seg, tiles_per_seq=tiles_per_seq, final_norm=final_norm),
        grid=(m // tm, nf), in_specs=in_specs, out_specs=out_specs, out_shape=out_shape,
        scratch_shapes=scratch, compiler_params=_params(2),
        name="conv_ffn",
    )(x, nw, w_up, w_up, conv_w, conv_w, conv_b, conv_b, w_down, buf, buf, final_w)
    new_buf = jnp.concatenate([nbv, nbg], axis=-1)
    return y, new_buf[tiles_per_seq - 1::tiles_per_seq]


def _s5_disc_kernel(lre_ref, lim_ref, ldt_ref, bre_ref, bim_ref, are_ref, aim_ref, bbre_ref, bbim_ref):
    lam_re = lre_ref[...]
    lam_im = lim_ref[...]
    dt = jnp.exp(ldt_ref[...])
    mag = jnp.exp(lam_re * dt)
    ph = lam_im * dt
    ab_re = mag * jnp.cos(ph)
    ab_im = mag * jnp.sin(ph)
    den = lam_re * lam_re + lam_im * lam_im
    cf_re = ((ab_re - 1.0) * lam_re + ab_im * lam_im) / den
    cf_im = (ab_im * lam_re - (ab_re - 1.0) * lam_im) / den
    are_ref[...] = ab_re
    aim_ref[...] = ab_im
    b_re = bre_ref[...]
    b_im = bim_ref[...]
    bbre_ref[...] = cf_re * b_re - cf_im * b_im
    bbim_ref[...] = cf_re * b_im + cf_im * b_re


def _s5_disc(lam_re, lam_im, log_dt, b_re_t, b_im_t):
    g, c, p = b_re_t.shape
    tile = lambda t: jnp.tile(t, (1, c))
    shp = jax.ShapeDtypeStruct((g, c * p), F32)
    a_re, a_im, bb_re, bb_im = pl.pallas_call(
        _s5_disc_kernel, out_shape=[shp, shp, shp, shp], name="s5_disc",
    )(tile(lam_re), tile(lam_im), jnp.broadcast_to(log_dt, (g, c * p)),
      b_re_t.reshape(g, c * p), b_im_t.reshape(g, c * p))
    return a_re[:, :p], a_im[:, :p], bb_re.reshape(g, c, p), bb_im.reshape(g, c, p)


def _s5_kernel(x_ref, nw_ref, wb_ref, wcr_ref, wci_ref, are_ref, aim_ref, d_ref, h0r_ref, h0i_ref,
               g_ref, hr_ref, hi_ref, u_scr, xr_scr, xi_scr, *, nseq, seg, chained):
    rows = nseq * seg
    if chained:
        @pl.when(pl.program_id(0) == 0)
        def _():
            hr_ref[...] = h0r_ref[...]
            hi_ref[...] = h0i_ref[...]
        sr_ref, si_ref = hr_ref, hi_ref
    else:
        sr_ref, si_ref = h0r_ref, h0i_ref

    u = _rms_rows(x_ref[...].reshape(rows, D_MODEL), nw_ref[...])
    u_scr[...] = u
    ub = u.astype(BF16)
    lpb = S5_BW // LANES
    for kb in range(S5_NB):
        res = _dot(ub[:, kb * LANES:(kb + 1) * LANES], wb_ref[kb])
        for q in range(lpb):
            xr_scr[kb * lpb + q] = res[:, q * LANES:(q + 1) * LANES]
            xi_scr[kb * lpb + q] = res[:, S5_BW + q * LANES:S5_BW + (q + 1) * LANES]

    for c in range(S5_STATE // S5_SCAN_W):
        blocks = range(c * S5_SCAN_LB, (c + 1) * S5_SCAN_LB)
        lane = [pl.ds(lb * LANES, LANES) for lb in blocks]
        a_re = [jnp.broadcast_to(are_ref[:, sl], (nseq, LANES)) for sl in lane]
        a_im = [jnp.broadcast_to(aim_ref[:, sl], (nseq, LANES)) for sl in lane]

        def step(t, carry):
            idx = pl.ds(t, nseq, stride=seg)
            out = []
            for n, lb in enumerate(blocks):
                s_re, s_im = carry[2 * n], carry[2 * n + 1]
                n_re = a_re[n] * s_re - a_im[n] * s_im + xr_scr[lb, idx, :]
                n_im = a_re[n] * s_im + a_im[n] * s_re + xi_scr[lb, idx, :]
                xr_scr[lb, idx, :] = n_re
                xi_scr[lb, idx, :] = n_im
                out += [n_re, n_im]
            return tuple(out)

        init = []
        for sl in lane:
            init += [sr_ref[:, sl], si_ref[:, sl]]
        fin = lax.fori_loop(0, seg, step, tuple(init))
        for n, sl in enumerate(lane):
            hr_ref[:, sl] = fin[2 * n]
            hi_ref[:, sl] = fin[2 * n + 1]

    for kb in range(S5_NB):
        xr = jnp.concatenate([xr_scr[kb * lpb + q] for q in range(lpb)], axis=1).astype(BF16)
        xi = jnp.concatenate([xi_scr[kb * lpb + q] for q in range(lpb)], axis=1).astype(BF16)
        lsl = slice(kb * LANES, (kb + 1) * LANES)
        y = _dot(xr, wcr_ref[kb]) - _dot(xi, wci_ref[kb]) + d_ref[:, lsl] * u_scr[:, lsl]
        act = y * (0.5 * (1.0 + jnp.tanh(GELU_C * (y + 0.044715 * (y * y * y)))))
        if chained:
            g_ref[:, :, lsl] = act.reshape(nseq, seg, LANES).astype(BF16)
        else:
            g_ref[:, lsl] = act.astype(BF16)


def _s5(x3, nw, wb, wcr, wci, a_re, a_im, d, h0_re, h0_im, *, chained, seg):
    b, l, _ = x3.shape
    if chained:
        nseq = b
        grid = (l // seg,)
        xmap = lambda n: (0, n, 0)
        smap = lambda n: (0, 0)
    else:
        nseq = 8
        grid = (b // nseq,)
        xmap = lambda i: (i, 0, 0)
        smap = lambda i: (i, 0)
        gmap = lambda i: (i, 0)
    rows = nseq * seg
    const2 = lambda *g: (0, 0)
    const3 = lambda *g: (0, 0, 0)
    if chained:
        g_shape, g_spec = (b, l, D_MODEL), pl.BlockSpec((nseq, seg, D_MODEL), xmap)
    else:
        g_shape, g_spec = (b * l, D_MODEL), pl.BlockSpec((rows, D_MODEL), gmap)
    g, h_re, h_im = pl.pallas_call(
        functools.partial(_s5_kernel, nseq=nseq, seg=seg, chained=chained),
        grid=grid,
        in_specs=[
            pl.BlockSpec((nseq, seg, D_MODEL), xmap),
            pl.BlockSpec((1, D_MODEL), const2),
            pl.BlockSpec((S5_NB, LANES, 2 * S5_BW), const3),
            pl.BlockSpec((S5_NB, S5_BW, LANES), const3),
            pl.BlockSpec((S5_NB, S5_BW, LANES), const3),
            pl.BlockSpec((1, S5_STATE), const2),
            pl.BlockSpec((1, S5_STATE), const2),
            pl.BlockSpec((1, D_MODEL), const2),
            pl.BlockSpec((nseq, S5_STATE), smap),
            pl.BlockSpec((nseq, S5_STATE), smap),
        ],
        out_specs=[
            g_spec,
            pl.BlockSpec((nseq, S5_STATE), smap),
            pl.BlockSpec((nseq, S5_STATE), smap),
        ],
        out_shape=[jax.ShapeDtypeStruct(g_shape, BF16),
                   jax.ShapeDtypeStruct((b, S5_STATE), F32),
                   jax.ShapeDtypeStruct((b, S5_STATE), F32)],
        scratch_shapes=[pltpu.VMEM((rows, D_MODEL), F32),
                        pltpu.VMEM((S5_STATE // LANES, rows, LANES), F32),
                        pltpu.VMEM((S5_STATE // LANES, rows, LANES), F32)],
        compiler_params=_params(1),
        name="s5_chained" if chained else "s5_step",
    )(x3, nw, wb, wcr, wci, a_re, a_im, d, h0_re, h0_im)
    return g, h_re, h_im


def _glu_kernel(x_ref, g_ref, wv_ref, wg_ref, o_ref):
    gb = g_ref[...]
    o_ref[...] = x_ref[...] + _dot(gb, wv_ref[...]) * _sigmoid(_dot(gb, wg_ref[...]))


def _glu(x, g, w_glu, tm=512, tn=512):
    m = x.shape[0]
    nn = D_MODEL // tn
    return pl.pallas_call(
        _glu_kernel,
        grid=(m // tm, nn),
        in_specs=[
            pl.BlockSpec((tm, tn), lambda i, j: (i, j)),
            pl.BlockSpec((tm, D_MODEL), lambda i, j: (i, 0)),
            pl.BlockSpec((D_MODEL, tn), lambda i, j: (0, j)),
            pl.BlockSpec((D_MODEL, tn), lambda i, j: (0, nn + j)),
        ],
        out_specs=pl.BlockSpec((tm, tn), lambda i, j: (i, j)),
        out_shape=jax.ShapeDtypeStruct((m, D_MODEL), F32),
        compiler_params=_params(2),
        name="glu",
    )(x, g, w_glu, w_glu)


def _rotary_tables(pos):
    half = RET_DK // 2
    inv = ROPE_BASE ** (-jnp.arange(half, dtype=F32) / half)
    ang = pos[:, None] * inv[None, :]
    cos = jnp.cos(ang)
    sin = jnp.sin(ang)
    return jnp.concatenate([cos, cos], axis=-1), jnp.concatenate([-sin, sin], axis=-1)


def _retention_tables(nseq, seg):
    log_g = jnp.log1p(-jnp.exp2(-5.0 - jnp.arange(RET_HEADS, dtype=F32)))
    row = jnp.arange(nseq * seg)
    t = (row % seg).astype(F32)
    sid = row // seg
    diff = t[:, None] - t[None, :]
    ok = (sid[:, None] == sid[None, :]) & (diff >= 0)
    dec = jnp.where(ok, jnp.exp(log_g[:, None, None] * jnp.where(ok, diff, 0.0)), 0.0)
    qd = jnp.exp(log_g[:, None] * (t + 1.0))[..., None]
    kd = jnp.exp(log_g[:, None] * (seg - 1.0 - t))[..., None]
    qd = jnp.broadcast_to(qd, (RET_HEADS, nseq * seg, RET_DK))
    kd = jnp.broadcast_to(kd, (RET_HEADS, nseq * seg, RET_DK))
    return dec, qd, kd


def _ret_chunk_gains(seg):
    import numpy as np
    log_g = np.log1p(-np.exp2(-5.0 - np.arange(RET_HEADS, dtype=np.float32))).astype(np.float32)
    return tuple(float(v) for v in np.exp(log_g * np.float32(seg)).astype(np.float32))


def _block_diag(blocks):
    nb, gb, r, c = blocks.shape
    eye = jnp.eye(gb, dtype=blocks.dtype)
    return (blocks[:, :, :, None, :] * eye[None, :, None, :, None]).reshape(nb, gb * r, gb * c)


def _trunk(x3, pos, gdn_s, gdn_cb, ret_s, s5_re, s5_im, ffn_cb, prm, *, chained):
    b, l, _ = x3.shape
    m = b * l
    x = x3.reshape(m, D_MODEL)
    nchunk = l // CHUNK_ROWS if chained else 1

    proj, gate = _in_proj(x, prm['norm_mix_w'][0:1], prm['w_in_main'], prm['w_in_gate'])
    o_a, gdn_new, gcb_new = _gdn(proj, gate, gdn_cb, prm['gdn_conv_w'], prm['alog_row'], prm['dtb_row'],
                                 prm['gdn_norm_w'], None if chained else gdn_s, nbatch=b, nchunk=nchunk)
    if chained:
        cosf, sinf = _rotary_tables(pos)
        dec, qd, kd = _retention_tables(1, CHUNK_ROWS)
        g_chunk = _ret_chunk_gains(CHUNK_ROWS)
    else:
        cosf, sinf = _rotary_tables(jnp.tile(pos, CHUNK_ROWS // l))
        dec, qd, kd = _retention_tables(CHUNK_ROWS // l, l)
        g_chunk = _ret_chunk_gains(l)
    o_b, ret_new = _ret(proj, cosf, sinf, dec, qd, kd, prm['ret_gn_w'], prm['ret_gn_b'],
                        None if chained else ret_s, g_chunk, nbatch=b, nchunk=nchunk)
    x = _out_proj(x, o_a, o_b, prm['w_out_a'], prm['w_out_b'])
    x, fcb0 = _ffn(x, prm['norm_ffn_w'][0:1], prm['w_up'][0], prm['ffn_conv_w'][0], prm['ffn_conv_b'][0:1],
                   prm['w_down'][0], ffn_cb[0], prm['norm_final_w'], nseq_total=b, final_norm=False)

    seg = CHUNK_ROWS if chained else l
    g, s5r_new, s5i_new = _s5(x.reshape(b, l, D_MODEL), prm['norm_mix_w'][1:2], prm['s5_wb'], prm['s5_wc_re'],
                              prm['s5_wc_im'], prm['s5_a_re'], prm['s5_a_im'], prm['s5_d'],
                              s5_re, s5_im, chained=chained, seg=seg)
    x = _glu(x, g.reshape(m, D_MODEL), prm['w_glu'])
    y, fcb1 = _ffn(x, prm['norm_ffn_w'][1:2], prm['w_up'][1], prm['ffn_conv_w'][1], prm['ffn_conv_b'][1:2],
                   prm['w_down'][1], ffn_cb[1], prm['norm_final_w'], nseq_total=b, final_norm=True)
    return (y.reshape(b, l, D_MODEL), gdn_new[None], gcb_new[None], ret_new[None],
            s5r_new.reshape(1, b, S5_GROUPS, S5_P), s5i_new.reshape(1, b, S5_GROUPS, S5_P),
            jnp.stack([fcb0, fcb1]))


def kernel(x_prompt, x_sample, state_gdn, state_gdn_conv, state_ret, state_s5_re, state_s5_im,
           state_ffn_conv, norm_mix_w, norm_ffn_w, norm_final_w,
           w_in, gdn_conv_w, gdn_a_log, gdn_dt_bias, gdn_norm_w, ret_gn_w, ret_gn_b, w_out,
           s5_lam_re, s5_lam_im, s5_log_dt, s5_b_re, s5_b_im, s5_c_re, s5_c_im, s5_d, w_glu,
           w_up, ffn_conv_w, ffn_conv_b, w_down):
    bp, lp, _ = x_prompt.shape
    bs, ls, _ = x_sample.shape
    past_len = 16384

    w = w_in[0]
    o_z = GDN_CONV_W
    o_a = o_z + GDN_V_W
    o_b = o_a + GDN_HEADS
    o_rest = o_b + GDN_HEADS
    w_in_main = jnp.concatenate([w[:, :o_a], w[:, o_rest:]], axis=1).astype(BF16)
    w_in_gate = jnp.pad(w[:, o_a:o_rest], ((0, 0), (0, GATE_W - 2 * GDN_HEADS))).astype(BF16)
    pad_row = lambda v: jnp.pad(v.reshape(1, -1), ((0, 0), (0, GATE_W - v.shape[-1])))

    a_re, a_im, bb_re, bb_im = _s5_disc(s5_lam_re[0], s5_lam_im[0], s5_log_dt[0].reshape(S5_GROUPS, 1),
                                        s5_b_re[0].transpose(0, 2, 1), s5_b_im[0].transpose(0, 2, 1))
    blk = lambda t: t.reshape(S5_NB, S5_GB, t.shape[1], t.shape[2])
    s5_wb = jnp.concatenate([_block_diag(blk(bb_re)), _block_diag(blk(bb_im))], axis=-1).astype(BF16)
    s5_wc_re = _block_diag(blk(s5_c_re[0].transpose(0, 2, 1))).astype(BF16)
    s5_wc_im = _block_diag(blk(s5_c_im[0].transpose(0, 2, 1))).astype(BF16)

    prm = dict(
        norm_mix_w=norm_mix_w, norm_ffn_w=norm_ffn_w, norm_final_w=norm_final_w.reshape(1, D_MODEL),
        w_in_main=w_in_main, w_in_gate=w_in_gate,
        gdn_conv_w=gdn_conv_w[0], alog_row=pad_row(gdn_a_log[0]), dtb_row=pad_row(gdn_dt_bias[0]),
        gdn_norm_w=gdn_norm_w[0].reshape(1, GDN_DV),
        ret_gn_w=ret_gn_w[0].reshape(1, RET_V_W), ret_gn_b=ret_gn_b[0].reshape(1, RET_V_W),
        w_out_a=w_out[0, :GDN_V_W].astype(BF16), w_out_b=w_out[0, GDN_V_W:].astype(BF16),
        s5_wb=s5_wb, s5_wc_re=s5_wc_re, s5_wc_im=s5_wc_im,
        s5_a_re=a_re.reshape(1, S5_STATE), s5_a_im=a_im.reshape(1, S5_STATE), s5_d=s5_d[0].reshape(1, D_MODEL),
        w_glu=w_glu[0].astype(BF16),
        w_up=w_up.astype(BF16), ffn_conv_w=ffn_conv_w, ffn_conv_b=ffn_conv_b, w_down=w_down.astype(BF16),
    )

    z_gcb = jnp.zeros((bp, GDN_CONV - 1, GDN_CONV_W), F32)
    z_s5 = jnp.zeros((bp, S5_STATE), F32)
    z_fcb = jnp.zeros((2, bp, FFN_CONV - 1, 2 * D_FF), F32)
    pos_p = jnp.arange(lp, dtype=F32)
    pos_s = past_len + jnp.arange(ls, dtype=F32)

    y_p, gdn_p, gcb_p, ret_p, s5r_p, s5i_p, fcb_p = _trunk(
        x_prompt, pos_p, None, z_gcb, None, z_s5, z_s5, z_fcb, prm, chained=True)
    y_s, gdn_s, gcb_s, ret_s, s5r_s, s5i_s, fcb_s = _trunk(
        x_sample, pos_s, state_gdn[0], state_gdn_conv[0], state_ret[0],
        state_s5_re[0].reshape(bs, S5_STATE), state_s5_im[0].reshape(bs, S5_STATE),
        state_ffn_conv, prm, chained=False)
    return (y_p, y_s, gdn_p, gdn_s, gcb_p, gcb_s, ret_p, ret_s,
            s5r_p, s5r_s, s5i_p, s5i_s, fcb_p, fcb_s)
```

```python
import functools
import math

import jax
import jax.numpy as jnp
from jax import lax
from jax.experimental import pallas as pl
from jax.experimental.pallas import tpu as pltpu

F32 = jnp.float32
BF16 = jnp.bfloat16

D_MODEL = 2048
GDN_HEADS = 8
GDN_DK = 128
GDN_DV = 128
GDN_CONV = 4
RET_HEADS = 4
RET_DK = 128
RET_DV = 256
ROPE_BASE = 10000.0
S5_GROUP = 16
S5_GROUPS = 128
S5_P = 64
S5_STATE = S5_GROUPS * S5_P
D_FF = 5632
FFN_CONV = 3
EPS = 1e-6

GDN_QK_W = GDN_HEADS * GDN_DK
GDN_V_W = GDN_HEADS * GDN_DV
GDN_CONV_W = 2 * GDN_QK_W + GDN_V_W
RET_QK_W = RET_HEADS * RET_DK
RET_V_W = RET_HEADS * RET_DV
PROJ_W = GDN_CONV_W + GDN_V_W + 2 * RET_QK_W + 2 * RET_V_W
GATE_W = 128

CHUNK_ROWS = 64
GDN_GROUP = 4
LANES = 128
VMEM_LIMIT = 56 * 1024 * 1024

S5_GB = 8
S5_NB = S5_GROUPS // S5_GB
S5_BW = S5_GB * S5_P
S5_SCAN_W = 1024
S5_SCAN_LB = S5_SCAN_W // LANES
GELU_C = math.sqrt(2.0 / math.pi)


def _params(n_grid):
    return pltpu.CompilerParams(dimension_semantics=("arbitrary",) * n_grid,
                                vmem_limit_bytes=VMEM_LIMIT)


def _dot(a, b):
    return jnp.dot(a, b, preferred_element_type=F32)


def _dot_nt(a, b):
    return lax.dot_general(a, b, (((1,), (1,)), ((), ())), preferred_element_type=F32)


def _dot_tn(a, b):
    return lax.dot_general(a, b, (((0,), (0,)), ((), ())), preferred_element_type=F32)


def _sigmoid(x):
    return 1.0 / (1.0 + jnp.exp(-x))


def _silu(x):
    return x * _sigmoid(x)


def _rms_rows(x, w):
    return x * lax.rsqrt(jnp.mean(x * x, axis=-1, keepdims=True) + EPS) * w


def _shift_rows(x, k, tpos, buf, nseq, seg):
    rows, width = x.shape
    nb = buf.shape[1]
    prev = pltpu.roll(x, k, 0)
    for t in range(k):
        src = buf[:, nb - k + t:nb - k + t + 1, :]
        srcb = jnp.broadcast_to(src, (nseq, seg, width)).reshape(rows, width)
        prev = jnp.where(tpos == t, srcb, prev)
    return prev


def _seq_masks(rows, seg):
    shift = int(math.log2(seg))
    ri = lax.broadcasted_iota(jnp.int32, (rows, rows), 0)
    ci = lax.broadcasted_iota(jnp.int32, (rows, rows), 1)
    same = (ri >> shift) == (ci >> shift)
    return same, same & (ci <= ri)


def _in_proj_kernel(x_ref, nw_ref, w_ref, wg_ref, o_ref, og_ref, h_scr):
    @pl.when(pl.program_id(1) == 0)
    def _():
        hb = _rms_rows(x_ref[...], nw_ref[...]).astype(BF16)
        h_scr[...] = hb
        og_ref[...] = _dot(hb, wg_ref[...])

    o_ref[...] = _dot(h_scr[...], w_ref[...])


def _in_proj(x, nw, w_main, w_gate, tm=512, tn=1024):
    m = x.shape[0]
    return pl.pallas_call(
        _in_proj_kernel,
        grid=(m // tm, PROJ_W // tn),
        in_specs=[
            pl.BlockSpec((tm, D_MODEL), lambda i, j: (i, 0)),
            pl.BlockSpec((1, D_MODEL), lambda i, j: (0, 0)),
            pl.BlockSpec((D_MODEL, tn), lambda i, j: (0, j)),
            pl.BlockSpec((D_MODEL, GATE_W), lambda i, j: (0, 0)),
        ],
        out_specs=[
            pl.BlockSpec((tm, tn), lambda i, j: (i, j)),
            pl.BlockSpec((tm, GATE_W), lambda i, j: (i, 0)),
        ],
        out_shape=[jax.ShapeDtypeStruct((m, PROJ_W), F32),
                   jax.ShapeDtypeStruct((m, GATE_W), F32)],
        scratch_shapes=[pltpu.VMEM((tm, D_MODEL), BF16)],
        compiler_params=_params(2),
        name="in_proj",
    )(x, nw, w_main, w_gate)


def _gdn_kernel(*refs, nseq, seg, chained):
    if chained:
        (qkv_ref, z_ref, gate_ref, cbuf_ref, cw_ref, alog_ref, dtb_ref, nw_ref,
         o_ref, s_out_ref, cb_out_ref, carry_scr) = refs
        s_in_ref = s_out_ref
        first = pl.program_id(1) == 0

        @pl.when(first)
        def _():
            s_out_ref[...] = jnp.zeros_like(s_out_ref)
    else:
        (qkv_ref, z_ref, gate_ref, cbuf_ref, cw_ref, alog_ref, dtb_ref, nw_ref, s_in_ref,
         o_ref, s_out_ref, cb_out_ref) = refs

    rows = nseq * seg
    shift = int(math.log2(seg))
    x = qkv_ref[...]
    if chained:
        buf = jnp.where(first, cbuf_ref[...], carry_scr[5:8, :].reshape(1, GDN_CONV - 1, GDN_CONV_W))
    else:
        buf = cbuf_ref[...]
    row_id = lax.broadcasted_iota(jnp.int32, (rows, 1), 0)
    tpos = row_id & (seg - 1)

    acc = x * cw_ref[GDN_CONV - 1:GDN_CONV, :]
    for k in range(1, GDN_CONV):
        acc = acc + _shift_rows(x, k, tpos, buf, nseq, seg) * cw_ref[GDN_CONV - 1 - k:GDN_CONV - k, :]
    act = _silu(acc)
    cb_out_ref[...] = x.reshape(nseq, seg, GDN_CONV_W)[:, seg - (GDN_CONV - 1):, :]
    if chained:
        carry_scr[...] = x[rows - 8:, :]

    gate = gate_ref[...]
    xa = gate + dtb_ref[...]
    softplus = jnp.maximum(xa, 0.0) + jnp.log(1.0 + jnp.exp(-jnp.abs(xa)))
    log_a = -jnp.exp(alog_ref[...]) * softplus
    beta_t = _sigmoid(gate)

    same, incl = _seq_masks(rows, seg)
    sum_m = jnp.concatenate([incl.astype(F32), same.astype(F32)], axis=0).astype(BF16)
    a1 = log_a.astype(BF16)
    r1 = log_a - a1.astype(F32)
    a2 = r1.astype(BF16)
    a3 = (r1 - a2.astype(F32)).astype(BF16)
    g_all = _dot(sum_m, a1) + (_dot(sum_m, a2) + _dot(sum_m, a3))
    g = g_all[:rows]
    g_last = g_all[rows:]
    g2_t = jnp.concatenate([g, g], axis=0).T
    e_g = jnp.exp(g)
    e_rest = jnp.exp(g_last - g)
    e_last = jnp.exp(g_last)
    if nseq > 1:
        row_masks = [((row_id >> shift) == s).astype(F32) for s in range(nseq)]
    nw = nw_ref[...]

    srows = GDN_GROUP * rows
    ri = lax.broadcasted_iota(jnp.int32, (srows, srows), 0)
    ci = lax.broadcasted_iota(jnp.int32, (srows, srows), 1)
    same_bd = (ri >> shift) == (ci >> shift)
    incl_bd = same_bd & (ci <= ri)
    strict_bd = same_bd & (ci < ri)
    eye_bd = (ri == ci).astype(F32)
    lane = lax.broadcasted_iota(jnp.int32, (1, LANES), 1)

    def stack_cols(tile, heads, off=0):
        return jnp.concatenate([tile[:, off + h:off + h + 1] for h in heads], axis=0)

    groups = [list(range(g0, g0 + GDN_GROUP)) for g0 in range(0, GDN_HEADS, GDN_GROUP)]
    ks = []
    q_all, k_all, v_all = [], [], []
    for heads in groups:
        qs, kn, vs = [], [], []
        for h in heads:
            lo = h * GDN_DK
            q = act[:, lo:lo + GDN_DK]
            k = act[:, GDN_QK_W + lo:GDN_QK_W + lo + GDN_DK]
            qs.append(q * lax.rsqrt(jnp.sum(q * q, axis=-1, keepdims=True) + EPS) * (GDN_DK ** -0.5))
            kn.append(k * lax.rsqrt(jnp.sum(k * k, axis=-1, keepdims=True) + EPS))
            vs.append(act[:, 2 * GDN_QK_W + lo:2 * GDN_QK_W + lo + GDN_DV])
        ks.append(kn)
        q_all.append(jnp.concatenate(qs, axis=0))
        k_all.append(jnp.concatenate(kn, axis=0))
        v_all.append(jnp.concatenate(vs, axis=0))
    gc = [stack_cols(g, heads) for heads in groups]
    bc = [stack_cols(beta_t, heads, GDN_HEADS) for heads in groups]
    egc = [stack_cols(e_g, heads) for heads in groups]
    gr = [jnp.concatenate(
        [jnp.where(lane < rows, g2_t[heads[j]:heads[j] + 1, :], g2_t[heads[j + 1]:heads[j + 1] + 1, :])
         for j in range(0, GDN_GROUP, 2)], axis=1) for heads in groups]
    decay = [jnp.where(incl_bd, jnp.exp(jnp.where(incl_bd, c - r, 0.0)), 0.0) for c, r in zip(gc, gr)]
    kb = [k.astype(BF16) for k in k_all]
    kk = [_dot_nt(k, k) for k in kb]
    qk = [_dot_nt(q.astype(BF16), k) * d for q, k, d in zip(q_all, kb, decay)]
    lms = [jnp.where(strict_bd, b * m * d, 0.0) for b, m, d in zip(bc, kk, decay)]
    inv = _unit_lower_inverse(lms, ri, ci, eye_bd, seg)
    rhs = [jnp.concatenate([v * b, k * (b * e)], axis=-1) for v, k, b, e in zip(v_all, k_all, bc, egc)]
    sol = [r + _dot((t - eye_bd).astype(BF16), r.astype(BF16)) for t, r in zip(inv, rhs)]
    qd_all = [q * e for q, e in zip(q_all, egc)]

    states = [[[s_in_ref[s, h] for s in range(nseq)] for h in heads] for heads in groups]
    v_new, o_all = [], []
    for gi, heads in enumerate(groups):
        ws_parts, qs_parts = [], []
        for j, h in enumerate(heads):
            r0 = j * rows
            lhs = jnp.concatenate([sol[gi][r0:r0 + rows, GDN_DV:], qd_all[gi][r0:r0 + rows]], axis=0).astype(BF16)
            ws = None
            for s in range(nseq):
                part = _dot(lhs, states[gi][j][s].astype(BF16))
                if nseq > 1:
                    part = part * jnp.concatenate([row_masks[s], row_masks[s]], axis=0)
                ws = part if ws is None else ws + part
            ws_parts.append(ws[:rows])
            qs_parts.append(ws[rows:])
        v_new.append(sol[gi][:, :GDN_DV] - jnp.concatenate(ws_parts, axis=0))
        o_all.append(jnp.concatenate(qs_parts, axis=0))
    o_all = [o + _dot(a.astype(BF16), v.astype(BF16)) for o, a, v in zip(o_all, qk, v_new)]
    for gi, heads in enumerate(groups):
        for j, h in enumerate(heads):
            r0 = j * rows
            lo = h * GDN_DK
            k_rest = (ks[gi][j] * e_rest[:, h:h + 1]).astype(BF16)
            vn = v_new[gi][r0:r0 + rows]
            for s in range(nseq):
                vsel = vn * row_masks[s] if nseq > 1 else vn
                s_out_ref[s, h] = (states[gi][j][s] * e_last[s * seg:s * seg + 1, h:h + 1]
                                   + _dot_tn(k_rest, vsel.astype(BF16)))
            o = o_all[gi][r0:r0 + rows]
            o = o * lax.rsqrt(jnp.mean(o * o, axis=-1, keepdims=True) + EPS) * nw
            o = o * _silu(z_ref[:, lo:lo + GDN_DV])
            o_ref[:, lo:lo + GDN_DV] = o.astype(BF16)


def _unit_lower_inverse(lms, ri, ci, eye, seg):
    blk = min(16, seg)
    sb = int(math.log2(blk))
    diag = (ri >> sb) == (ci >> sb)
    neg_d = [-jnp.where(diag, lm, 0.0) for lm in lms]
    m_b = [d.astype(BF16) for d in neg_d]
    inv = [eye + d for d in neg_d]
    for _ in range(sb - 1):
        m_b = [_dot(m, m).astype(BF16) for m in m_b]
        inv = [t + _dot(t.astype(BF16), m) for t, m in zip(inv, m_b)]
    size_log = sb
    while (1 << size_log) < seg:
        off = ((ri >> (size_log + 1)) == (ci >> (size_log + 1))) & ((ri >> size_log) != (ci >> size_log))
        c = [jnp.where(off, lm, 0.0) for lm in lms]
        e_b = [(t - eye).astype(BF16) for t in inv]
        y = [x + _dot(e, x.astype(BF16)) for x, e in zip(c, e_b)]
        inv = [t - (v + _dot(v.astype(BF16), e)) for t, v, e in zip(inv, y, e_b)]
        size_log += 1
    return inv


def _gdn(proj, gate, cbuf, conv_w, alog_row, dtb_row, norm_w, state, *, nbatch, nchunk):
    chained = state is None
    rows_total = proj.shape[0]
    if chained:
        nseq, seg = 1, CHUNK_ROWS
        grid = (nbatch, nchunk)
        rb = lambda b, n: b * nchunk + n
        sb = lambda b, n: b
    else:
        nseq, seg = 8, 8
        grid = (rows_total // CHUNK_ROWS,)
        rb = lambda i: i
        sb = lambda i: i
    n_state = nbatch
    wrap = lambda f, *tail: (lambda *g: (f(*g),) + tail)
    const = lambda *tail: (lambda *g: tail)
    in_specs = [
        pl.BlockSpec((CHUNK_ROWS, GDN_CONV_W), wrap(rb, 0)),
        pl.BlockSpec((CHUNK_ROWS, GDN_V_W), wrap(rb, GDN_CONV_W // GDN_V_W)),
        pl.BlockSpec((CHUNK_ROWS, GATE_W), wrap(rb, 0)),
        pl.BlockSpec((nseq, GDN_CONV - 1, GDN_CONV_W), wrap(sb, 0, 0)),
        pl.BlockSpec((GDN_CONV, GDN_CONV_W), const(0, 0)),
        pl.BlockSpec((1, GATE_W), const(0, 0)),
        pl.BlockSpec((1, GATE_W), const(0, 0)),
        pl.BlockSpec((1, GDN_DV), const(0, 0)),
    ]
    args = [proj, proj, gate, cbuf, conv_w, alog_row, dtb_row, norm_w]
    scratch = []
    if chained:
        scratch = [pltpu.VMEM((8, GDN_CONV_W), F32)]
    else:
        in_specs.append(pl.BlockSpec((nseq, GDN_HEADS, GDN_DK, GDN_DV), wrap(sb, 0, 0, 0)))
        args.append(state)
    out_specs = [
        pl.BlockSpec((CHUNK_ROWS, GDN_V_W), wrap(rb, 0)),
        pl.BlockSpec((nseq, GDN_HEADS, GDN_DK, GDN_DV), wrap(sb, 0, 0, 0)),
        pl.BlockSpec((nseq, GDN_CONV - 1, GDN_CONV_W), wrap(sb, 0, 0)),
    ]
    out_shape = [
        jax.ShapeDtypeStruct((rows_total, GDN_V_W), BF16),
        jax.ShapeDtypeStruct((n_state, GDN_HEADS, GDN_DK, GDN_DV), F32),
        jax.ShapeDtypeStruct((n_state, GDN_CONV - 1, GDN_CONV_W), F32),
    ]
    return pl.pallas_call(
        functools.partial(_gdn_kernel, nseq=nseq, seg=seg, chained=chained),
        grid=grid, in_specs=in_specs, out_specs=out_specs, out_shape=out_shape,
        scratch_shapes=scratch, compiler_params=_params(len(grid)),
        name="gdn_chained" if chained else "gdn_step",
    )(*args)


def _ret_kernel(*refs, nseq, seg, chained, g_chunk):
    if chained:
        (q_ref, k_ref, v_ref, gt_ref, cos_ref, sin_ref, dec_ref, qd_ref, kd_ref, gw_ref, gb_ref,
         o_ref, s_out_ref) = refs
        s_in_ref = s_out_ref

        @pl.when(pl.program_id(1) == 0)
        def _():
            s_out_ref[...] = jnp.zeros_like(s_out_ref)
    else:
        (q_ref, k_ref, v_ref, gt_ref, cos_ref, sin_ref, dec_ref, qd_ref, kd_ref, gw_ref, gb_ref,
         s_in_ref, o_ref, s_out_ref) = refs

    rows = nseq * seg
    shift = int(math.log2(seg))
    cosf = cos_ref[...]
    sinf = sin_ref[...]
    if nseq > 1:
        row_id = lax.broadcasted_iota(jnp.int32, (rows, 1), 0)
        row_masks = [((row_id >> shift) == s).astype(F32) for s in range(nseq)]

    for h in range(RET_HEADS):
        lo = h * RET_DK
        vlo = h * RET_DV
        q = q_ref[:, lo:lo + RET_DK]
        k = k_ref[:, lo:lo + RET_DK]
        q = q * cosf + pltpu.roll(q, RET_DK // 2, 1) * sinf
        k = (k * cosf + pltpu.roll(k, RET_DK // 2, 1) * sinf) * (RET_DK ** -0.5)
        vb = v_ref[:, vlo:vlo + RET_DV].astype(BF16)
        kb = k.astype(BF16)
        sc = _dot_nt(q.astype(BF16), kb) * dec_ref[h]
        o = _dot(sc.astype(BF16), vb)
        q_dec = (q * qd_ref[h]).astype(BF16)
        k_dec = (k * kd_ref[h]).astype(BF16)
        for s in range(nseq):
            state = s_in_ref[s, h]
            part = _dot(q_dec, state.astype(BF16))
            if nseq > 1:
                part = part * row_masks[s]
                vs = (v_ref[:, vlo:vlo + RET_DV] * row_masks[s]).astype(BF16)
            else:
                vs = vb
            o = o + part
            s_out_ref[s, h] = state * g_chunk[h] + _dot_tn(k_dec, vs)
        mu = jnp.mean(o, axis=-1, keepdims=True)
        oc = o - mu
        var = jnp.mean(oc * oc, axis=-1, keepdims=True)
        on = oc * lax.rsqrt(var + EPS)
        on = (on * gw_ref[:, vlo:vlo + RET_DV] + gb_ref[:, vlo:vlo + RET_DV]) * _silu(gt_ref[:, vlo:vlo + RET_DV])
        o_ref[:, vlo:vlo + RET_DV] = on.astype(BF16)


def _ret(proj, cosf, sinf, dec, qd, kd, gw, gb, state, g_chunk, *, nbatch, nchunk):
    chained = state is None
    rows_total = proj.shape[0]
    if chained:
        nseq, seg = 1, CHUNK_ROWS
        grid = (nbatch, nchunk)
        rb = lambda b, n: b * nchunk + n
        sb = lambda b, n: b
        pb = lambda b, n: n
    else:
        nseq, seg = 8, 8
        grid = (rows_total // CHUNK_ROWS,)
        rb = lambda i: i
        sb = lambda i: i
        pb = lambda i: 0
    wrap = lambda f, *tail: (lambda *g: (f(*g),) + tail)
    const = lambda *tail: (lambda *g: tail)
    q_col = (GDN_CONV_W + GDN_V_W) // RET_QK_W
    v_col = (GDN_CONV_W + GDN_V_W + 2 * RET_QK_W) // RET_V_W
    in_specs = [
        pl.BlockSpec((CHUNK_ROWS, RET_QK_W), wrap(rb, q_col)),
        pl.BlockSpec((CHUNK_ROWS, RET_QK_W), wrap(rb, q_col + 1)),
        pl.BlockSpec((CHUNK_ROWS, RET_V_W), wrap(rb, v_col)),
        pl.BlockSpec((CHUNK_ROWS, RET_V_W), wrap(rb, v_col + 1)),
        pl.BlockSpec((CHUNK_ROWS, RET_DK), wrap(pb, 0)),
        pl.BlockSpec((CHUNK_ROWS, RET_DK), wrap(pb, 0)),
        pl.BlockSpec((RET_HEADS, CHUNK_ROWS, CHUNK_ROWS), const(0, 0, 0)),
        pl.BlockSpec((RET_HEADS, CHUNK_ROWS, RET_DK), const(0, 0, 0)),
        pl.BlockSpec((RET_HEADS, CHUNK_ROWS, RET_DK), const(0, 0, 0)),
        pl.BlockSpec((1, RET_V_W), const(0, 0)),
        pl.BlockSpec((1, RET_V_W), const(0, 0)),
    ]
    args = [proj, proj, proj, proj, cosf, sinf, dec, qd, kd, gw, gb]
    if not chained:
        in_specs.append(pl.BlockSpec((nseq, RET_HEADS, RET_DK, RET_DV), wrap(sb, 0, 0, 0)))
        args.append(state)
    out_specs = [
        pl.BlockSpec((CHUNK_ROWS, RET_V_W), wrap(rb, 0)),
        pl.BlockSpec((nseq, RET_HEADS, RET_DK, RET_DV), wrap(sb, 0, 0, 0)),
    ]
    out_shape = [
        jax.ShapeDtypeStruct((rows_total, RET_V_W), BF16),
        jax.ShapeDtypeStruct((nbatch, RET_HEADS, RET_DK, RET_DV), F32),
    ]
    return pl.pallas_call(
        functools.partial(_ret_kernel, nseq=nseq, seg=seg, chained=chained, g_chunk=g_chunk),
        grid=grid, in_specs=in_specs, out_specs=out_specs, out_shape=out_shape,
        compiler_params=_params(len(grid)),
        name="ret_chained" if chained else "ret_step",
    )(*args)


def _out_proj_kernel(x_ref, a_ref, b_ref, wa_ref, wb_ref, o_ref):
    o_ref[...] = x_ref[...] + (_dot(a_ref[...], wa_ref[...]) + _dot(b_ref[...], wb_ref[...]))


def _out_proj(x, o_a, o_b, w_a, w_b, tm=512):
    m = x.shape[0]
    return pl.pallas_call(
        _out_proj_kernel,
        grid=(m // tm,),
        in_specs=[
            pl.BlockSpec((tm, D_MODEL), lambda i: (i, 0)),
            pl.BlockSpec((tm, GDN_V_W), lambda i: (i, 0)),
            pl.BlockSpec((tm, RET_V_W), lambda i: (i, 0)),
            pl.BlockSpec((GDN_V_W, D_MODEL), lambda i: (0, 0)),
            pl.BlockSpec((RET_V_W, D_MODEL), lambda i: (0, 0)),
        ],
        out_specs=pl.BlockSpec((tm, D_MODEL), lambda i: (i, 0)),
        out_shape=jax.ShapeDtypeStruct((m, D_MODEL), F32),
        compiler_params=_params(1),
        name="out_proj",
    )(x, o_a, o_b, w_a, w_b)


def _ffn_kernel(*refs, nseq, seg, tiles_per_seq, final_norm):
    (x_ref, nw_ref, wv_ref, wg_ref, cwv_ref, cwg_ref, bv_ref, bg_ref, wd_ref, bufv_ref, bufg_ref,
     fw_ref, o_ref, nbv_ref, nbg_ref, h_scr, acc_scr) = refs[:17]
    chained = tiles_per_seq > 1
    i = pl.program_id(0)
    j = pl.program_id(1)
    nf = pl.num_programs(1)
    rows = nseq * seg

    @pl.when(j == 0)
    def _():
        h_scr[...] = _rms_rows(x_ref[...], nw_ref[...]).astype(BF16)
        acc_scr[...] = jnp.zeros_like(acc_scr)

    tpos = lax.broadcasted_iota(jnp.int32, (rows, 1), 0) & (seg - 1)
    hb = h_scr[...]

    def branch(w_ref, cw_ref, b_ref, buf_ref, nb_ref, carry_ref):
        up = _dot(hb, w_ref[...])
        if chained:
            buf = jnp.where(i % tiles_per_seq == 0, buf_ref[...], carry_ref[j][None, 6:8, :])
        else:
            buf = buf_ref[...]
        out = up * cw_ref[FFN_CONV - 1:FFN_CONV, :] + b_ref[...]
        for k in range(1, FFN_CONV):
            out = out + _shift_rows(up, k, tpos, buf, nseq, seg) * cw_ref[FFN_CONV - 1 - k:FFN_CONV - k, :]
        nb_ref[...] = up.reshape(nseq, seg, up.shape[1])[:, seg - (FFN_CONV - 1):, :]
        if chained:
            carry_ref[j] = up[rows - 8:, :]
        return out

    if chained:
        cv_scr, cg_scr = refs[17], refs[18]
    else:
        cv_scr = cg_scr = None
    val = branch(wv_ref, cwv_ref, bv_ref, bufv_ref, nbv_ref, cv_scr)
    gate = branch(wg_ref, cwg_ref, bg_ref, bufg_ref, nbg_ref, cg_scr)
    act = (_silu(gate) * val).astype(BF16)
    acc_scr[...] += _dot(act, wd_ref[...])

    @pl.when(j == nf - 1)
    def _():
        y = x_ref[...] + acc_scr[...]
        if final_norm:
            y = _rms_rows(y, fw_ref[...])
        o_ref[...] = y


def _ffn(x, nw, w_up, conv_w, conv_b, w_down, buf, final_w, *, nseq_total, final_norm,
         x_time_major=False, out_time_major=False, tm=512, tf=512):
    m = x.shape[0] * x.shape[1] // D_MODEL
    seq_len = m // nseq_total
    if seq_len >= tm:
        nseq, seg, tiles_per_seq = 1, tm, seq_len // tm
    else:
        nseq, seg, tiles_per_seq = tm // seq_len, seq_len, 1
    nf = D_FF // tf
    sidx = (lambda i, j: i // tiles_per_seq) if tiles_per_seq > 1 else (lambda i, j: i)
    seq_major = lambda i, j: (i, 0)
    time_major = lambda i, j: (i % tiles_per_seq, i // tiles_per_seq)
    in_specs = [
        pl.BlockSpec((tm, D_MODEL), time_major if x_time_major else seq_major),
        pl.BlockSpec((1, D_MODEL), lambda i, j: (0, 0)),
        pl.BlockSpec((D_MODEL, tf), lambda i, j: (0, j)),
        pl.BlockSpec((D_MODEL, tf), lambda i, j: (0, nf + j)),
        pl.BlockSpec((FFN_CONV, tf), lambda i, j: (0, j)),
        pl.BlockSpec((FFN_CONV, tf), lambda i, j: (0, nf + j)),
        pl.BlockSpec((1, tf), lambda i, j: (0, j)),
        pl.BlockSpec((1, tf), lambda i, j: (0, nf + j)),
        pl.BlockSpec((tf, D_MODEL), lambda i, j: (j, 0)),
        pl.BlockSpec((nseq, FFN_CONV - 1, tf), lambda i, j: (sidx(i, j), 0, j)),
        pl.BlockSpec((nseq, FFN_CONV - 1, tf), lambda i, j: (sidx(i, j), 0, nf + j)),
        pl.BlockSpec((1, D_MODEL), lambda i, j: (0, 0)),
    ]
    out_specs = [
        pl.BlockSpec((tm, D_MODEL), time_major if out_time_major else seq_major),
        pl.BlockSpec((nseq, FFN_CONV - 1, tf), lambda i, j: (i, 0, j)),
        pl.BlockSpec((nseq, FFN_CONV - 1, tf), lambda i, j: (i, 0, j)),
    ]
    out_shape = [
        jax.ShapeDtypeStruct((seq_len, nseq_total * D_MODEL) if out_time_major else (m, D_MODEL), F32),
        jax.ShapeDtypeStruct((m // tm * nseq, FFN_CONV - 1, D_FF), F32),
        jax.ShapeDtypeStruct((m // tm * nseq, FFN_CONV - 1, D_FF), F32),
    ]
    scratch = [pltpu.VMEM((tm, D_MODEL), BF16), pltpu.VMEM((tm, D_MODEL), F32)]
    if tiles_per_seq > 1:
        scratch += [pltpu.VMEM((nf, 8, tf), F32), pltpu.VMEM((nf, 8, tf), F32)]
    y, nbv, nbg = pl.pallas_call(
        functools.partial(_ffn_kernel, nseq=nseq, seg=seg, tiles_per_seq=tiles_per_seq, final_norm=final_norm),
        grid=(m // tm, nf), in_specs=in_specs, out_specs=out_specs, out_shape=out_shape,
        scratch_shapes=scratch, compiler_params=_params(2),
        name="conv_ffn",
    )(x, nw, w_up, w_up, conv_w, conv_w, conv_b, conv_b, w_down, buf, buf, final_w)
    new_buf = jnp.concatenate([nbv, nbg], axis=-1)
    return y, new_buf[tiles_per_seq - 1::tiles_per_seq]


def _s5_disc_kernel(lre_ref, lim_ref, ldt_ref, bre_ref, bim_ref, are_ref, aim_ref, bbre_ref, bbim_ref):
    lam_re = lre_ref[...]
    lam_im = lim_ref[...]
    dt = jnp.exp(ldt_ref[...])
    mag = jnp.exp(lam_re * dt)
    ph = lam_im * dt
    ab_re = mag * jnp.cos(ph)
    ab_im = mag * jnp.sin(ph)
    den = lam_re * lam_re + lam_im * lam_im
    cf_re = ((ab_re - 1.0) * lam_re + ab_im * lam_im) / den
    cf_im = (ab_im * lam_re - (ab_re - 1.0) * lam_im) / den
    are_ref[...] = ab_re
    aim_ref[...] = ab_im
    b_re = bre_ref[...]
    b_im = bim_ref[...]
    bbre_ref[...] = cf_re * b_re - cf_im * b_im
    bbim_ref[...] = cf_re * b_im + cf_im * b_re


def _s5_disc(lam_re, lam_im, log_dt, b_re_t, b_im_t):
    g, c, p = b_re_t.shape
    tile = lambda t: jnp.tile(t, (1, c))
    shp = jax.ShapeDtypeStruct((g, c * p), F32)
    a_re, a_im, bb_re, bb_im = pl.pallas_call(
        _s5_disc_kernel, out_shape=[shp, shp, shp, shp], name="s5_disc",
    )(tile(lam_re), tile(lam_im), jnp.broadcast_to(log_dt, (g, c * p)),
      b_re_t.reshape(g, c * p), b_im_t.reshape(g, c * p))
    return a_re[:, :p], a_im[:, :p], bb_re.reshape(g, c, p), bb_im.reshape(g, c, p)


def _gelu(y):
    return y * (0.5 * (1.0 + jnp.tanh(GELU_C * (y + 0.044715 * (y * y * y)))))


def _s5_step_kernel(x_ref, nw_ref, wb_ref, wcr_ref, wci_ref, are_ref, aim_ref, d_ref, sr_ref, si_ref,
                    g_ref, hr_ref, hi_ref, u_scr, xr_scr, xi_scr, *, nseq, seg):
    rows = nseq * seg
    u = _rms_rows(x_ref[...].reshape(rows, D_MODEL), nw_ref[...])
    u_scr[...] = u
    ub = u.astype(BF16)
    lpb = S5_BW // LANES
    for kb in range(S5_NB):
        res = _dot(ub[:, kb * LANES:(kb + 1) * LANES], wb_ref[kb])
        for q in range(lpb):
            xr_scr[kb * lpb + q] = res[:, q * LANES:(q + 1) * LANES]
            xi_scr[kb * lpb + q] = res[:, S5_BW + q * LANES:S5_BW + (q + 1) * LANES]

    for c in range(S5_STATE // S5_SCAN_W):
        blocks = range(c * S5_SCAN_LB, (c + 1) * S5_SCAN_LB)
        lane = [pl.ds(lb * LANES, LANES) for lb in blocks]
        a_re = [jnp.broadcast_to(are_ref[:, sl], (nseq, LANES)) for sl in lane]
        a_im = [jnp.broadcast_to(aim_ref[:, sl], (nseq, LANES)) for sl in lane]

        def step(t, carry):
            idx = pl.ds(t, nseq, stride=seg)
            out = []
            for n, lb in enumerate(blocks):
                s_re, s_im = carry[2 * n], carry[2 * n + 1]
                n_re = a_re[n] * s_re - a_im[n] * s_im + xr_scr[lb, idx, :]
                n_im = a_re[n] * s_im + a_im[n] * s_re + xi_scr[lb, idx, :]
                xr_scr[lb, idx, :] = n_re
                xi_scr[lb, idx, :] = n_im
                out += [n_re, n_im]
            return tuple(out)

        init = []
        for sl in lane:
            init += [sr_ref[:, sl], si_ref[:, sl]]
        fin = lax.fori_loop(0, seg, step, tuple(init))
        for n, sl in enumerate(lane):
            hr_ref[:, sl] = fin[2 * n]
            hi_ref[:, sl] = fin[2 * n + 1]

    for kb in range(S5_NB):
        xr = jnp.concatenate([xr_scr[kb * lpb + q] for q in range(lpb)], axis=1).astype(BF16)
        xi = jnp.concatenate([xi_scr[kb * lpb + q] for q in range(lpb)], axis=1).astype(BF16)
        lsl = slice(kb * LANES, (kb + 1) * LANES)
        y = _dot(xr, wcr_ref[kb]) - _dot(xi, wci_ref[kb]) + d_ref[:, lsl] * u_scr[:, lsl]
        g_ref[:, lsl] = _gelu(y).astype(BF16)


def _s5_step(x3, nw, wb, wcr, wci, a_re, a_im, d, h0_re, h0_im):
    b, l, _ = x3.shape
    nseq, seg = 8, l
    rows = nseq * seg
    const2 = lambda i: (0, 0)
    const3 = lambda i: (0, 0, 0)
    smap = lambda i: (i, 0)
    return pl.pallas_call(
        functools.partial(_s5_step_kernel, nseq=nseq, seg=seg),
        grid=(b // nseq,),
        in_specs=[
            pl.BlockSpec((nseq, seg, D_MODEL), lambda i: (i, 0, 0)),
            pl.BlockSpec((1, D_MODEL), const2),
            pl.BlockSpec((S5_NB, LANES, 2 * S5_BW), const3),
            pl.BlockSpec((S5_NB, S5_BW, LANES), const3),
            pl.BlockSpec((S5_NB, S5_BW, LANES), const3),
            pl.BlockSpec((1, S5_STATE), const2),
            pl.BlockSpec((1, S5_STATE), const2),
            pl.BlockSpec((1, D_MODEL), const2),
            pl.BlockSpec((nseq, S5_STATE), smap),
            pl.BlockSpec((nseq, S5_STATE), smap),
        ],
        out_specs=[
            pl.BlockSpec((rows, D_MODEL), smap),
            pl.BlockSpec((nseq, S5_STATE), smap),
            pl.BlockSpec((nseq, S5_STATE), smap),
        ],
        out_shape=[jax.ShapeDtypeStruct((b * l, D_MODEL), BF16),
                   jax.ShapeDtypeStruct((b, S5_STATE), F32),
                   jax.ShapeDtypeStruct((b, S5_STATE), F32)],
        scratch_shapes=[pltpu.VMEM((rows, D_MODEL), F32),
                        pltpu.VMEM((S5_STATE // LANES, rows, LANES), F32),
                        pltpu.VMEM((S5_STATE // LANES, rows, LANES), F32)],
        compiler_params=_params(1),
        name="s5_step",
    )(x3, nw, wb, wcr, wci, a_re, a_im, d, h0_re, h0_im)


def _s5_seq_kernel(x_ref, nw_ref, perm_ref, wb_ref, wc_ref, are_ref, aim_ref, d_ref, h0r_ref, h0i_ref,
                   g_ref, hr_ref, hi_ref, st_scr, u_scr, ub_scr, x_scr, *, nseq, seg):
    wrows = 2 * nseq * seg

    @pl.when(pl.program_id(0) == 0)
    def _():
        st_scr[0:nseq, :] = h0r_ref[...]
        st_scr[nseq:2 * nseq, :] = h0i_ref[...]

    x = x_ref[...]
    nw = nw_ref[...]
    u = jnp.concatenate([_rms_rows(x[:, s * D_MODEL:(s + 1) * D_MODEL], nw) for s in range(nseq)], axis=0)
    u_hi = u.astype(BF16)
    u_lo = (u - u_hi.astype(F32)).astype(BF16)
    perm = perm_ref[...]
    cw = 512
    for c in range(D_MODEL // cw):
        csl = slice(c * cw, (c + 1) * cw)
        hi = _dot(perm, u_hi[:, csl])
        u_scr[:, csl] = hi + _dot(perm, u_lo[:, csl])
        ub_scr[:, csl] = hi.astype(BF16)

    row = lax.broadcasted_iota(jnp.int32, (wrows, 1), 0)
    is_re = (row & (2 * nseq - 1)) < nseq
    lpb = S5_BW // LANES
    for kb in range(S5_NB):
        blk = ub_scr[:, kb * LANES:(kb + 1) * LANES]
        zero = jnp.zeros_like(blk)
        lhs = jnp.concatenate([jnp.where(is_re, blk, zero), jnp.where(is_re, zero, blk)], axis=1)
        res = _dot(lhs, wb_ref[kb])
        for q in range(lpb):
            x_scr[kb * lpb + q] = res[:, q * LANES:(q + 1) * LANES]

    sign = jnp.where(lax.broadcasted_iota(jnp.int32, (2 * nseq, 1), 0) < nseq, -1.0, 1.0)
    for c in range(S5_STATE // S5_SCAN_W):
        blocks = range(c * S5_SCAN_LB, (c + 1) * S5_SCAN_LB)
        lane = [pl.ds(lb * LANES, LANES) for lb in blocks]
        a1 = [jnp.broadcast_to(are_ref[:, sl], (2 * nseq, LANES)) for sl in lane]
        a2 = [jnp.broadcast_to(aim_ref[:, sl], (2 * nseq, LANES)) * sign for sl in lane]

        def step(t, carry):
            rsl = pl.ds(pl.multiple_of(t * (2 * nseq), 2 * nseq), 2 * nseq)
            out = []
            for n, lb in enumerate(blocks):
                st = carry[n]
                nxt = a1[n] * st + a2[n] * pltpu.roll(st, nseq, 0) + x_scr[lb, rsl, :]
                x_scr[lb, rsl, :] = nxt
                out.append(nxt)
            return tuple(out)

        fin = lax.fori_loop(0, seg, step, tuple(st_scr[:, sl] for sl in lane))
        for n, sl in enumerate(lane):
            st_scr[:, sl] = fin[n]
    hr_ref[...] = st_scr[0:nseq, :]
    hi_ref[...] = st_scr[nseq:2 * nseq, :]

    for kb in range(S5_NB):
        xb = jnp.concatenate([x_scr[kb * lpb + q] for q in range(lpb)], axis=1).astype(BF16)
        y2 = _dot(xb, wc_ref[kb])
        lsl = slice(kb * LANES, (kb + 1) * LANES)
        y = y2[:, :LANES] - pltpu.roll(y2[:, LANES:], wrows - nseq, 0) + d_ref[:, lsl] * u_scr[:, lsl]
        g_ref[:, lsl] = _gelu(y).astype(BF16)


def _s5_seq(x_tm, nw, perm, wb2, wc2, a_re, a_im, d, h0_re, h0_im, *, nseq, seg):
    l = x_tm.shape[0]
    wrows = 2 * nseq * seg
    const2 = lambda n: (0, 0)
    const3 = lambda n: (0, 0, 0)
    return pl.pallas_call(
        functools.partial(_s5_seq_kernel, nseq=nseq, seg=seg),
        grid=(l // seg,),
        in_specs=[
            pl.BlockSpec((seg, nseq * D_MODEL), lambda n: (n, 0)),
            pl.BlockSpec((1, D_MODEL), const2),
            pl.BlockSpec((wrows, nseq * seg), const2),
            pl.BlockSpec((S5_NB, 2 * LANES, S5_BW), const3),
            pl.BlockSpec((S5_NB, S5_BW, 2 * LANES), const3),
            pl.BlockSpec((1, S5_STATE), const2),
            pl.BlockSpec((1, S5_STATE), const2),
            pl.BlockSpec((1, D_MODEL), const2),
            pl.BlockSpec((nseq, S5_STATE), const2),
            pl.BlockSpec((nseq, S5_STATE), const2),
        ],
        out_specs=[
            pl.BlockSpec((wrows, D_MODEL), lambda n: (n, 0)),
            pl.BlockSpec((nseq, S5_STATE), const2),
            pl.BlockSpec((nseq, S5_STATE), const2),
        ],
        out_shape=[jax.ShapeDtypeStruct((l * 2 * nseq, D_MODEL), BF16),
                   jax.ShapeDtypeStruct((nseq, S5_STATE), F32),
                   jax.ShapeDtypeStruct((nseq, S5_STATE), F32)],
        scratch_shapes=[pltpu.VMEM((2 * nseq, S5_STATE), F32),
                        pltpu.VMEM((wrows, D_MODEL), F32),
                        pltpu.VMEM((wrows, D_MODEL), BF16),
                        pltpu.VMEM((S5_STATE // LANES, wrows, LANES), F32)],
        compiler_params=_params(1),
        name="s5_seq",
    )(x_tm, nw, perm, wb2, wc2, a_re, a_im, d, h0_re, h0_im)


def _s5_row_perm(nseq, seg):
    r = jnp.arange(2 * nseq * seg)
    src = (r % nseq) * seg + r // (2 * nseq)
    return (src[:, None] == jnp.arange(nseq * seg)[None, :]).astype(BF16)


def _glu_kernel(x_ref, g_ref, wv_ref, wg_ref, o_ref):
    gb = g_ref[...]
    o_ref[...] = x_ref[...] + _dot(gb, wv_ref[...]) * _sigmoid(_dot(gb, wg_ref[...]))


def _glu(x, g, w_glu, *, time_major_seqs=0, tm=512, tn=512):
    nn = D_MODEL // tn
    if time_major_seqs:
        tps = x.shape[0] // tm
        n_tiles = tps * time_major_seqs
        xmap = lambda i, j: (i % tps, (i // tps) * nn + j)
        gmap = lambda i, j: (i % tps, i // tps)
    else:
        n_tiles = x.shape[0] // tm
        xmap = lambda i, j: (i, j)
        gmap = lambda i, j: (i, 0)
    return pl.pallas_call(
        _glu_kernel,
        grid=(n_tiles, nn),
        in_specs=[
            pl.BlockSpec((tm, tn), xmap),
            pl.BlockSpec((tm, D_MODEL), gmap),
            pl.BlockSpec((D_MODEL, tn), lambda i, j: (0, j)),
            pl.BlockSpec((D_MODEL, tn), lambda i, j: (0, nn + j)),
        ],
        out_specs=pl.BlockSpec((tm, tn), xmap),
        out_shape=jax.ShapeDtypeStruct(x.shape, F32),
        compiler_params=_params(2),
        name="glu",
    )(x, g, w_glu, w_glu)


def _rotary_tables(pos):
    half = RET_DK // 2
    inv = ROPE_BASE ** (-jnp.arange(half, dtype=F32) / half)
    ang = pos[:, None] * inv[None, :]
    cos = jnp.cos(ang)
    sin = jnp.sin(ang)
    return jnp.concatenate([cos, cos], axis=-1), jnp.concatenate([-sin, sin], axis=-1)


def _retention_tables(nseq, seg):
    log_g = jnp.log1p(-jnp.exp2(-5.0 - jnp.arange(RET_HEADS, dtype=F32)))
    row = jnp.arange(nseq * seg)
    t = (row % seg).astype(F32)
    sid = row // seg
    diff = t[:, None] - t[None, :]
    ok = (sid[:, None] == sid[None, :]) & (diff >= 0)
    dec = jnp.where(ok, jnp.exp(log_g[:, None, None] * jnp.where(ok, diff, 0.0)), 0.0)
    qd = jnp.exp(log_g[:, None] * (t + 1.0))[..., None]
    kd = jnp.exp(log_g[:, None] * (seg - 1.0 - t))[..., None]
    qd = jnp.broadcast_to(qd, (RET_HEADS, nseq * seg, RET_DK))
    kd = jnp.broadcast_to(kd, (RET_HEADS, nseq * seg, RET_DK))
    return dec, qd, kd


def _ret_chunk_gains(seg):
    import numpy as np
    log_g = np.log1p(-np.exp2(-5.0 - np.arange(RET_HEADS, dtype=np.float32))).astype(np.float32)
    return tuple(float(v) for v in np.exp(log_g * np.float32(seg)).astype(np.float32))


def _block_diag(blocks):
    nb, gb, r, c = blocks.shape
    eye = jnp.eye(gb, dtype=blocks.dtype)
    return (blocks[:, :, :, None, :] * eye[None, :, None, :, None]).reshape(nb, gb * r, gb * c)


def _trunk(x3, pos, gdn_s, gdn_cb, ret_s, s5_re, s5_im, ffn_cb, prm, *, chained):
    b, l, _ = x3.shape
    m = b * l
    x = x3.reshape(m, D_MODEL)
    nchunk = l // CHUNK_ROWS if chained else 1

    proj, gate = _in_proj(x, prm['norm_mix_w'][0:1], prm['w_in_main'], prm['w_in_gate'])
    o_a, gdn_new, gcb_new = _gdn(proj, gate, gdn_cb, prm['gdn_conv_w'], prm['alog_row'], prm['dtb_row'],
                                 prm['gdn_norm_w'], None if chained else gdn_s, nbatch=b, nchunk=nchunk)
    if chained:
        cosf, sinf = _rotary_tables(pos)
        dec, qd, kd = _retention_tables(1, CHUNK_ROWS)
        g_chunk = _ret_chunk_gains(CHUNK_ROWS)
    else:
        cosf, sinf = _rotary_tables(jnp.tile(pos, CHUNK_ROWS // l))
        dec, qd, kd = _retention_tables(CHUNK_ROWS // l, l)
        g_chunk = _ret_chunk_gains(l)
    o_b, ret_new = _ret(proj, cosf, sinf, dec, qd, kd, prm['ret_gn_w'], prm['ret_gn_b'],
                        None if chained else ret_s, g_chunk, nbatch=b, nchunk=nchunk)
    x = _out_proj(x, o_a, o_b, prm['w_out_a'], prm['w_out_b'])
    x, fcb0 = _ffn(x, prm['norm_ffn_w'][0:1], prm['w_up'][0], prm['ffn_conv_w'][0], prm['ffn_conv_b'][0:1],
                   prm['w_down'][0], ffn_cb[0], prm['norm_final_w'], nseq_total=b, final_norm=False,
                   out_time_major=chained)

    if chained:
        g, s5r_new, s5i_new = _s5_seq(x, prm['norm_mix_w'][1:2], _s5_row_perm(b, CHUNK_ROWS), prm['s5_wb2'],
                                      prm['s5_wc2'], prm['s5_a_re'], prm['s5_a_im'], prm['s5_d'],
                                      s5_re, s5_im, nseq=b, seg=CHUNK_ROWS)
        x = _glu(x, g.reshape(l, 2 * b * D_MODEL), prm['w_glu'], time_major_seqs=b)
    else:
        g, s5r_new, s5i_new = _s5_step(x.reshape(b, l, D_MODEL), prm['norm_mix_w'][1:2], prm['s5_wb'],
                                       prm['s5_wc_re'], prm['s5_wc_im'], prm['s5_a_re'], prm['s5_a_im'],
                                       prm['s5_d'], s5_re, s5_im)
        x = _glu(x, g, prm['w_glu'])
    y, fcb1 = _ffn(x, prm['norm_ffn_w'][1:2], prm['w_up'][1], prm['ffn_conv_w'][1], prm['ffn_conv_b'][1:2],
                   prm['w_down'][1], ffn_cb[1], prm['norm_final_w'], nseq_total=b, final_norm=True,
                   x_time_major=chained)
    return (y.reshape(b, l, D_MODEL), gdn_new[None], gcb_new[None], ret_new[None],
            s5r_new.reshape(1, b, S5_GROUPS, S5_P), s5i_new.reshape(1, b, S5_GROUPS, S5_P),
            jnp.stack([fcb0, fcb1]))


def kernel(x_prompt, x_sample, state_gdn, state_gdn_conv, state_ret, state_s5_re, state_s5_im,
           state_ffn_conv, norm_mix_w, norm_ffn_w, norm_final_w,
           w_in, gdn_conv_w, gdn_a_log, gdn_dt_bias, gdn_norm_w, ret_gn_w, ret_gn_b, w_out,
           s5_lam_re, s5_lam_im, s5_log_dt, s5_b_re, s5_b_im, s5_c_re, s5_c_im, s5_d, w_glu,
           w_up, ffn_conv_w, ffn_conv_b, w_down):
    bp, lp, _ = x_prompt.shape
    bs, ls, _ = x_sample.shape
    past_len = 16384

    w = w_in[0]
    o_z = GDN_CONV_W
    o_a = o_z + GDN_V_W
    o_b = o_a + GDN_HEADS
    o_rest = o_b + GDN_HEADS
    w_in_main = jnp.concatenate([w[:, :o_a], w[:, o_rest:]], axis=1).astype(BF16)
    w_in_gate = jnp.pad(w[:, o_a:o_rest], ((0, 0), (0, GATE_W - 2 * GDN_HEADS))).astype(BF16)
    pad_row = lambda v: jnp.pad(v.reshape(1, -1), ((0, 0), (0, GATE_W - v.shape[-1])))

    a_re, a_im, bb_re, bb_im = _s5_disc(s5_lam_re[0], s5_lam_im[0], s5_log_dt[0].reshape(S5_GROUPS, 1),
                                        s5_b_re[0].transpose(0, 2, 1), s5_b_im[0].transpose(0, 2, 1))
    blk = lambda t: t.reshape(S5_NB, S5_GB, t.shape[1], t.shape[2])
    wb_re, wb_im = _block_diag(blk(bb_re)).astype(BF16), _block_diag(blk(bb_im)).astype(BF16)
    s5_wb = jnp.concatenate([wb_re, wb_im], axis=-1)
    s5_wb2 = jnp.concatenate([wb_re, wb_im], axis=1)
    s5_wc_re = _block_diag(blk(s5_c_re[0].transpose(0, 2, 1))).astype(BF16)
    s5_wc_im = _block_diag(blk(s5_c_im[0].transpose(0, 2, 1))).astype(BF16)
    s5_wc2 = jnp.concatenate([s5_wc_re, s5_wc_im], axis=-1)

    prm = dict(
        norm_mix_w=norm_mix_w, norm_ffn_w=norm_ffn_w, norm_final_w=norm_final_w.reshape(1, D_MODEL),
        w_in_main=w_in_main, w_in_gate=w_in_gate,
        gdn_conv_w=gdn_conv_w[0], alog_row=pad_row(gdn_a_log[0]), dtb_row=pad_row(gdn_dt_bias[0]),
        gdn_norm_w=gdn_norm_w[0].reshape(1, GDN_DV),
        ret_gn_w=ret_gn_w[0].reshape(1, RET_V_W), ret_gn_b=ret_gn_b[0].reshape(1, RET_V_W),
        w_out_a=w_out[0, :GDN_V_W].astype(BF16), w_out_b=w_out[0, GDN_V_W:].astype(BF16),
        s5_wb=s5_wb, s5_wc_re=s5_wc_re, s5_wc_im=s5_wc_im, s5_wb2=s5_wb2, s5_wc2=s5_wc2,
        s5_a_re=a_re.reshape(1, S5_STATE), s5_a_im=a_im.reshape(1, S5_STATE), s5_d=s5_d[0].reshape(1, D_MODEL),
        w_glu=w_glu[0].astype(BF16),
        w_up=w_up.astype(BF16), ffn_conv_w=ffn_conv_w, ffn_conv_b=ffn_conv_b, w_down=w_down.astype(BF16),
    )

    z_gcb = jnp.zeros((bp, GDN_CONV - 1, GDN_CONV_W), F32)
    z_s5 = jnp.zeros((bp, S5_STATE), F32)
    z_fcb = jnp.zeros((2, bp, FFN_CONV - 1, 2 * D_FF), F32)
    pos_p = jnp.arange(lp, dtype=F32)
    pos_s = past_len + jnp.arange(ls, dtype=F32)

    y_p, gdn_p, gcb_p, ret_p, s5r_p, s5i_p, fcb_p = _trunk(
        x_prompt, pos_p, None, z_gcb, None, z_s5, z_s5, z_fcb, prm, chained=True)
    y_s, gdn_s, gcb_s, ret_s, s5r_s, s5i_s, fcb_s = _trunk(
        x_sample, pos_s, state_gdn[0], state_gdn_conv[0], state_ret[0],
        state_s5_re[0].reshape(bs, S5_STATE), state_s5_im[0].reshape(bs, S5_STATE),
        state_ffn_conv, prm, chained=False)
    return (y_p, y_s, gdn_p, gdn_s, gcb_p, gcb_s, ret_p, ret_s,
            s5r_p, s5r_s, s5i_p, s5i_s, fcb_p, fcb_s)
```

```python
import functools
import math

import jax
import jax.numpy as jnp
from jax import lax
from jax.experimental import pallas as pl
from jax.experimental.pallas import tpu as pltpu

F32 = jnp.float32
BF16 = jnp.bfloat16

D_MODEL = 2048
GDN_HEADS = 8
GDN_DK = 128
GDN_DV = 128
GDN_CONV = 4
RET_HEADS = 4
RET_DK = 128
RET_DV = 256
ROPE_BASE = 10000.0
S5_GROUP = 16
S5_GROUPS = 128
S5_P = 64
S5_STATE = S5_GROUPS * S5_P
D_FF = 5632
FFN_CONV = 3
EPS = 1e-6

GDN_QK_W = GDN_HEADS * GDN_DK
GDN_V_W = GDN_HEADS * GDN_DV
GDN_CONV_W = 2 * GDN_QK_W + GDN_V_W
RET_QK_W = RET_HEADS * RET_DK
RET_V_W = RET_HEADS * RET_DV
PROJ_W = GDN_CONV_W + GDN_V_W + 2 * RET_QK_W + 2 * RET_V_W
GATE_W = 128

CHUNK_ROWS = 64
GDN_GROUP = 4
LANES = 128
VMEM_LIMIT = 56 * 1024 * 1024

S5_GB = 8
S5_NB = S5_GROUPS // S5_GB
S5_BW = S5_GB * S5_P
S5_SCAN_W = 1024
S5_SCAN_LB = S5_SCAN_W // LANES
GELU_C = math.sqrt(2.0 / math.pi)


def _params(n_grid):
    return pltpu.CompilerParams(dimension_semantics=("arbitrary",) * n_grid,
                                vmem_limit_bytes=VMEM_LIMIT)


def _dot(a, b):
    return jnp.dot(a, b, preferred_element_type=F32)


def _dot_nt(a, b):
    return lax.dot_general(a, b, (((1,), (1,)), ((), ())), preferred_element_type=F32)


def _dot_tn(a, b):
    return lax.dot_general(a, b, (((0,), (0,)), ((), ())), preferred_element_type=F32)


def _sigmoid(x):
    return 0.5 + 0.5 * jnp.tanh(0.5 * x)


def _silu(x):
    h = 0.5 * x
    return h + h * jnp.tanh(h)


def _rms_rows(x, w):
    return x * lax.rsqrt(jnp.mean(x * x, axis=-1, keepdims=True) + EPS) * w


def _shift_rows(x, k, tpos, buf, nseq, seg):
    rows, width = x.shape
    nb = buf.shape[1]
    prev = pltpu.roll(x, k, 0)
    for t in range(k):
        src = buf[:, nb - k + t:nb - k + t + 1, :]
        srcb = jnp.broadcast_to(src, (nseq, seg, width)).reshape(rows, width)
        prev = jnp.where(tpos == t, srcb, prev)
    return prev


def _seq_masks(rows, seg):
    shift = int(math.log2(seg))
    ri = lax.broadcasted_iota(jnp.int32, (rows, rows), 0)
    ci = lax.broadcasted_iota(jnp.int32, (rows, rows), 1)
    same = (ri >> shift) == (ci >> shift)
    return same, same & (ci <= ri)


def _in_proj_kernel(x_ref, nw_ref, w_ref, wg_ref, o_ref, og_ref, h_scr):
    @pl.when(pl.program_id(1) == 0)
    def _():
        hb = _rms_rows(x_ref[...], nw_ref[...]).astype(BF16)
        h_scr[...] = hb
        og_ref[...] = _dot(hb, wg_ref[...])

    o_ref[...] = _dot(h_scr[...], w_ref[...])


def _in_proj(x, nw, w_main, w_gate, tm=1024, tn=1024):
    m = x.shape[0]
    return pl.pallas_call(
        _in_proj_kernel,
        grid=(m // tm, PROJ_W // tn),
        in_specs=[
            pl.BlockSpec((tm, D_MODEL), lambda i, j: (i, 0), pipeline_mode=pl.Buffered(1)),
            pl.BlockSpec((1, D_MODEL), lambda i, j: (0, 0)),
            pl.BlockSpec((D_MODEL, tn), lambda i, j: (0, j)),
            pl.BlockSpec((D_MODEL, GATE_W), lambda i, j: (0, 0)),
        ],
        out_specs=[
            pl.BlockSpec((tm, tn), lambda i, j: (i, j)),
            pl.BlockSpec((tm, GATE_W), lambda i, j: (i, 0)),
        ],
        out_shape=[jax.ShapeDtypeStruct((m, PROJ_W), F32),
                   jax.ShapeDtypeStruct((m, GATE_W), F32)],
        scratch_shapes=[pltpu.VMEM((tm, D_MODEL), BF16)],
        compiler_params=_params(2),
        name="in_proj",
    )(x, nw, w_main, w_gate)


def _gdn_kernel(*refs, nseq, seg, chained):
    if chained:
        (qkv_ref, z_ref, gate_ref, cbuf_ref, cw_ref, alog_ref, dtb_ref, nw_ref,
         o_ref, s_out_ref, cb_out_ref, carry_scr) = refs
        s_in_ref = s_out_ref
        first = pl.program_id(1) == 0

        @pl.when(first)
        def _():
            s_out_ref[...] = jnp.zeros_like(s_out_ref)
    else:
        (qkv_ref, z_ref, gate_ref, cbuf_ref, cw_ref, alog_ref, dtb_ref, nw_ref, s_in_ref,
         o_ref, s_out_ref, cb_out_ref) = refs

    rows = nseq * seg
    shift = int(math.log2(seg))
    x = qkv_ref[...]
    if chained:
        buf = jnp.where(first, cbuf_ref[...], carry_scr[5:8, :].reshape(1, GDN_CONV - 1, GDN_CONV_W))
    else:
        buf = cbuf_ref[...]
    row_id = lax.broadcasted_iota(jnp.int32, (rows, 1), 0)
    tpos = row_id & (seg - 1)

    acc = x * cw_ref[GDN_CONV - 1:GDN_CONV, :]
    for k in range(1, GDN_CONV):
        acc = acc + _shift_rows(x, k, tpos, buf, nseq, seg) * cw_ref[GDN_CONV - 1 - k:GDN_CONV - k, :]
    act = _silu(acc)
    cb_out_ref[...] = x.reshape(nseq, seg, GDN_CONV_W)[:, seg - (GDN_CONV - 1):, :]
    if chained:
        carry_scr[...] = x[rows - 8:, :]

    gate = gate_ref[...]
    xa = gate + dtb_ref[...]
    softplus = jnp.maximum(xa, 0.0) + jnp.log(1.0 + jnp.exp(-jnp.abs(xa)))
    log_a = -jnp.exp(alog_ref[...]) * softplus
    beta_t = _sigmoid(gate)

    same, incl = _seq_masks(rows, seg)
    sum_m = jnp.concatenate([incl.astype(F32), same.astype(F32)], axis=0).astype(BF16)
    a1 = log_a.astype(BF16)
    r1 = log_a - a1.astype(F32)
    a2 = r1.astype(BF16)
    a3 = (r1 - a2.astype(F32)).astype(BF16)
    g_all = _dot(sum_m, a1) + (_dot(sum_m, a2) + _dot(sum_m, a3))
    g = g_all[:rows]
    g_last = g_all[rows:]
    g2_t = jnp.concatenate([g, g], axis=0).T
    e_g = jnp.exp(g)
    e_rest = jnp.exp(g_last - g)
    e_last = jnp.exp(g_last)
    if nseq > 1:
        row_masks = [((row_id >> shift) == s).astype(F32) for s in range(nseq)]
    nw = nw_ref[...]

    srows = GDN_GROUP * rows
    ri = lax.broadcasted_iota(jnp.int32, (srows, srows), 0)
    ci = lax.broadcasted_iota(jnp.int32, (srows, srows), 1)
    same_bd = (ri >> shift) == (ci >> shift)
    incl_bd = same_bd & (ci <= ri)
    strict_bd = same_bd & (ci < ri)
    eye_bd = (ri == ci).astype(F32)
    lane = lax.broadcasted_iota(jnp.int32, (1, LANES), 1)

    def stack_cols(tile, heads, off=0):
        return jnp.concatenate([tile[:, off + h:off + h + 1] for h in heads], axis=0)

    groups = [list(range(g0, g0 + GDN_GROUP)) for g0 in range(0, GDN_HEADS, GDN_GROUP)]
    ks = []
    q_all, k_all, v_all = [], [], []
    for heads in groups:
        qs, kn, vs = [], [], []
        for h in heads:
            lo = h * GDN_DK
            q = act[:, lo:lo + GDN_DK]
            k = act[:, GDN_QK_W + lo:GDN_QK_W + lo + GDN_DK]
            qs.append(q * lax.rsqrt(jnp.sum(q * q, axis=-1, keepdims=True) + EPS) * (GDN_DK ** -0.5))
            kn.append(k * lax.rsqrt(jnp.sum(k * k, axis=-1, keepdims=True) + EPS))
            vs.append(act[:, 2 * GDN_QK_W + lo:2 * GDN_QK_W + lo + GDN_DV])
        ks.append(kn)
        q_all.append(jnp.concatenate(qs, axis=0))
        k_all.append(jnp.concatenate(kn, axis=0))
        v_all.append(jnp.concatenate(vs, axis=0))
    gc = [stack_cols(g, heads) for heads in groups]
    bc = [stack_cols(beta_t, heads, GDN_HEADS) for heads in groups]
    egc = [stack_cols(e_g, heads) for heads in groups]
    gr = [jnp.concatenate(
        [jnp.where(lane < rows, g2_t[heads[j]:heads[j] + 1, :], g2_t[heads[j + 1]:heads[j + 1] + 1, :])
         for j in range(0, GDN_GROUP, 2)], axis=1) for heads in groups]
    decay = [jnp.where(incl_bd, jnp.exp(jnp.where(incl_bd, c - r, 0.0)), 0.0) for c, r in zip(gc, gr)]
    kb = [k.astype(BF16) for k in k_all]
    kk = [_dot_nt(k, k) for k in kb]
    qk = [_dot_nt(q.astype(BF16), k) * d for q, k, d in zip(q_all, kb, decay)]
    lms = [jnp.where(strict_bd, b * m * d, 0.0) for b, m, d in zip(bc, kk, decay)]
    inv = _unit_lower_inverse(lms, ri, ci, eye_bd, seg)
    rhs = [jnp.concatenate([v * b, k * (b * e)], axis=-1) for v, k, b, e in zip(v_all, k_all, bc, egc)]
    sol = [r + _dot((t - eye_bd).astype(BF16), r.astype(BF16)) for t, r in zip(inv, rhs)]
    qd_all = [q * e for q, e in zip(q_all, egc)]

    states = [[[s_in_ref[s, h] for s in range(nseq)] for h in heads] for heads in groups]
    v_new, o_all = [], []
    for gi, heads in enumerate(groups):
        ws_parts, qs_parts = [], []
        for j, h in enumerate(heads):
            r0 = j * rows
            lhs = jnp.concatenate([sol[gi][r0:r0 + rows, GDN_DV:], qd_all[gi][r0:r0 + rows]], axis=0).astype(BF16)
            ws = None
            for s in range(nseq):
                part = _dot(lhs, states[gi][j][s].astype(BF16))
                if nseq > 1:
                    part = part * jnp.concatenate([row_masks[s], row_masks[s]], axis=0)
                ws = part if ws is None else ws + part
            ws_parts.append(ws[:rows])
            qs_parts.append(ws[rows:])
        v_new.append(sol[gi][:, :GDN_DV] - jnp.concatenate(ws_parts, axis=0))
        o_all.append(jnp.concatenate(qs_parts, axis=0))
    o_all = [o + _dot(a.astype(BF16), v.astype(BF16)) for o, a, v in zip(o_all, qk, v_new)]
    for gi, heads in enumerate(groups):
        for j, h in enumerate(heads):
            r0 = j * rows
            lo = h * GDN_DK
            k_rest = (ks[gi][j] * e_rest[:, h:h + 1]).astype(BF16)
            vn = v_new[gi][r0:r0 + rows]
            for s in range(nseq):
                vsel = vn * row_masks[s] if nseq > 1 else vn
                s_out_ref[s, h] = (states[gi][j][s] * e_last[s * seg:s * seg + 1, h:h + 1]
                                   + _dot_tn(k_rest, vsel.astype(BF16)))
            o = o_all[gi][r0:r0 + rows]
            o = o * lax.rsqrt(jnp.mean(o * o, axis=-1, keepdims=True) + EPS) * nw
            o = o * _silu(z_ref[:, lo:lo + GDN_DV])
            o_ref[:, lo:lo + GDN_DV] = o.astype(BF16)


def _unit_lower_inverse(lms, ri, ci, eye, seg):
    blk = min(16, seg)
    sb = int(math.log2(blk))
    diag = (ri >> sb) == (ci >> sb)
    neg_d = [-jnp.where(diag, lm, 0.0) for lm in lms]
    m_b = [d.astype(BF16) for d in neg_d]
    inv = [eye + d for d in neg_d]
    for _ in range(sb - 1):
        m_b = [_dot(m, m).astype(BF16) for m in m_b]
        inv = [t + _dot(t.astype(BF16), m) for t, m in zip(inv, m_b)]
    size_log = sb
    while (1 << size_log) < seg:
        off = ((ri >> (size_log + 1)) == (ci >> (size_log + 1))) & ((ri >> size_log) != (ci >> size_log))
        c = [jnp.where(off, lm, 0.0) for lm in lms]
        e_b = [(t - eye).astype(BF16) for t in inv]
        y = [x + _dot(e, x.astype(BF16)) for x, e in zip(c, e_b)]
        inv = [t - (v + _dot(v.astype(BF16), e)) for t, v, e in zip(inv, y, e_b)]
        size_log += 1
    return inv


def _gdn(proj, gate, cbuf, conv_w, alog_row, dtb_row, norm_w, state, *, nbatch, nchunk):
    chained = state is None
    rows_total = proj.shape[0]
    if chained:
        nseq, seg = 1, CHUNK_ROWS
        grid = (nbatch, nchunk)
        rb = lambda b, n: b * nchunk + n
        sb = lambda b, n: b
    else:
        nseq, seg = 8, 8
        grid = (rows_total // CHUNK_ROWS,)
        rb = lambda i: i
        sb = lambda i: i
    n_state = nbatch
    wrap = lambda f, *tail: (lambda *g: (f(*g),) + tail)
    const = lambda *tail: (lambda *g: tail)
    in_specs = [
        pl.BlockSpec((CHUNK_ROWS, GDN_CONV_W), wrap(rb, 0)),
        pl.BlockSpec((CHUNK_ROWS, GDN_V_W), wrap(rb, GDN_CONV_W // GDN_V_W)),
        pl.BlockSpec((CHUNK_ROWS, GATE_W), wrap(rb, 0)),
        pl.BlockSpec((nseq, GDN_CONV - 1, GDN_CONV_W), wrap(sb, 0, 0)),
        pl.BlockSpec((GDN_CONV, GDN_CONV_W), const(0, 0)),
        pl.BlockSpec((1, GATE_W), const(0, 0)),
        pl.BlockSpec((1, GATE_W), const(0, 0)),
        pl.BlockSpec((1, GDN_DV), const(0, 0)),
    ]
    args = [proj, proj, gate, cbuf, conv_w, alog_row, dtb_row, norm_w]
    scratch = []
    if chained:
        scratch = [pltpu.VMEM((8, GDN_CONV_W), F32)]
    else:
        in_specs.append(pl.BlockSpec((nseq, GDN_HEADS, GDN_DK, GDN_DV), wrap(sb, 0, 0, 0)))
        args.append(state)
    out_specs = [
        pl.BlockSpec((CHUNK_ROWS, GDN_V_W), wrap(rb, 0)),
        pl.BlockSpec((nseq, GDN_HEADS, GDN_DK, GDN_DV), wrap(sb, 0, 0, 0)),
        pl.BlockSpec((nseq, GDN_CONV - 1, GDN_CONV_W), wrap(sb, 0, 0)),
    ]
    out_shape = [
        jax.ShapeDtypeStruct((rows_total, GDN_V_W), BF16),
        jax.ShapeDtypeStruct((n_state, GDN_HEADS, GDN_DK, GDN_DV), F32),
        jax.ShapeDtypeStruct((n_state, GDN_CONV - 1, GDN_CONV_W), F32),
    ]
    return pl.pallas_call(
        functools.partial(_gdn_kernel, nseq=nseq, seg=seg, chained=chained),
        grid=grid, in_specs=in_specs, out_specs=out_specs, out_shape=out_shape,
        scratch_shapes=scratch, compiler_params=_params(len(grid)),
        name="gdn_chained" if chained else "gdn_step",
    )(*args)


def _ret_kernel(*refs, nseq, seg, chained, g_chunk):
    if chained:
        (q_ref, k_ref, v_ref, gt_ref, cos_ref, sin_ref, dec_ref, qd_ref, kd_ref, gw_ref, gb_ref,
         o_ref, s_out_ref) = refs
        s_in_ref = s_out_ref

        @pl.when(pl.program_id(1) == 0)
        def _():
            s_out_ref[...] = jnp.zeros_like(s_out_ref)
    else:
        (q_ref, k_ref, v_ref, gt_ref, cos_ref, sin_ref, dec_ref, qd_ref, kd_ref, gw_ref, gb_ref,
         s_in_ref, o_ref, s_out_ref) = refs

    rows = nseq * seg
    shift = int(math.log2(seg))
    cosf = cos_ref[...]
    sinf = sin_ref[...]
    if nseq > 1:
        row_id = lax.broadcasted_iota(jnp.int32, (rows, 1), 0)
        row_masks = [((row_id >> shift) == s).astype(F32) for s in range(nseq)]

    for h in range(RET_HEADS):
        lo = h * RET_DK
        vlo = h * RET_DV
        q = q_ref[:, lo:lo + RET_DK]
        k = k_ref[:, lo:lo + RET_DK]
        q = q * cosf + pltpu.roll(q, RET_DK // 2, 1) * sinf
        k = (k * cosf + pltpu.roll(k, RET_DK // 2, 1) * sinf) * (RET_DK ** -0.5)
        vb = v_ref[:, vlo:vlo + RET_DV].astype(BF16)
        kb = k.astype(BF16)
        sc = _dot_nt(q.astype(BF16), kb) * dec_ref[h]
        o = _dot(sc.astype(BF16), vb)
        q_dec = (q * qd_ref[h]).astype(BF16)
        k_dec = (k * kd_ref[h]).astype(BF16)
        for s in range(nseq):
            state = s_in_ref[s, h]
            part = _dot(q_dec, state.astype(BF16))
            if nseq > 1:
                part = part * row_masks[s]
                vs = (v_ref[:, vlo:vlo + RET_DV] * row_masks[s]).astype(BF16)
            else:
                vs = vb
            o = o + part
            s_out_ref[s, h] = state * g_chunk[h] + _dot_tn(k_dec, vs)
        mu = jnp.mean(o, axis=-1, keepdims=True)
        oc = o - mu
        var = jnp.mean(oc * oc, axis=-1, keepdims=True)
        on = oc * lax.rsqrt(var + EPS)
        on = (on * gw_ref[:, vlo:vlo + RET_DV] + gb_ref[:, vlo:vlo + RET_DV]) * _silu(gt_ref[:, vlo:vlo + RET_DV])
        o_ref[:, vlo:vlo + RET_DV] = on.astype(BF16)


def _ret(proj, cosf, sinf, dec, qd, kd, gw, gb, state, g_chunk, *, nbatch, nchunk):
    chained = state is None
    rows_total = proj.shape[0]
    if chained:
        nseq, seg = 1, CHUNK_ROWS
        grid = (nbatch, nchunk)
        rb = lambda b, n: b * nchunk + n
        sb = lambda b, n: b
        pb = lambda b, n: n
    else:
        nseq, seg = 8, 8
        grid = (rows_total // CHUNK_ROWS,)
        rb = lambda i: i
        sb = lambda i: i
        pb = lambda i: 0
    wrap = lambda f, *tail: (lambda *g: (f(*g),) + tail)
    const = lambda *tail: (lambda *g: tail)
    q_col = (GDN_CONV_W + GDN_V_W) // RET_QK_W
    v_col = (GDN_CONV_W + GDN_V_W + 2 * RET_QK_W) // RET_V_W
    in_specs = [
        pl.BlockSpec((CHUNK_ROWS, RET_QK_W), wrap(rb, q_col)),
        pl.BlockSpec((CHUNK_ROWS, RET_QK_W), wrap(rb, q_col + 1)),
        pl.BlockSpec((CHUNK_ROWS, RET_V_W), wrap(rb, v_col)),
        pl.BlockSpec((CHUNK_ROWS, RET_V_W), wrap(rb, v_col + 1)),
        pl.BlockSpec((CHUNK_ROWS, RET_DK), wrap(pb, 0)),
        pl.BlockSpec((CHUNK_ROWS, RET_DK), wrap(pb, 0)),
        pl.BlockSpec((RET_HEADS, CHUNK_ROWS, CHUNK_ROWS), const(0, 0, 0)),
        pl.BlockSpec((RET_HEADS, CHUNK_ROWS, RET_DK), const(0, 0, 0)),
        pl.BlockSpec((RET_HEADS, CHUNK_ROWS, RET_DK), const(0, 0, 0)),
        pl.BlockSpec((1, RET_V_W), const(0, 0)),
        pl.BlockSpec((1, RET_V_W), const(0, 0)),
    ]
    args = [proj, proj, proj, proj, cosf, sinf, dec, qd, kd, gw, gb]
    if not chained:
        in_specs.append(pl.BlockSpec((nseq, RET_HEADS, RET_DK, RET_DV), wrap(sb, 0, 0, 0)))
        args.append(state)
    out_specs = [
        pl.BlockSpec((CHUNK_ROWS, RET_V_W), wrap(rb, 0)),
        pl.BlockSpec((nseq, RET_HEADS, RET_DK, RET_DV), wrap(sb, 0, 0, 0)),
    ]
    out_shape = [
        jax.ShapeDtypeStruct((rows_total, RET_V_W), BF16),
        jax.ShapeDtypeStruct((nbatch, RET_HEADS, RET_DK, RET_DV), F32),
    ]
    return pl.pallas_call(
        functools.partial(_ret_kernel, nseq=nseq, seg=seg, chained=chained, g_chunk=g_chunk),
        grid=grid, in_specs=in_specs, out_specs=out_specs, out_shape=out_shape,
        compiler_params=_params(len(grid)),
        name="ret_chained" if chained else "ret_step",
    )(*args)


def _out_proj_kernel(x_ref, a_ref, b_ref, wa_ref, wb_ref, o_ref):
    o_ref[...] = x_ref[...] + (_dot(a_ref[...], wa_ref[...]) + _dot(b_ref[...], wb_ref[...]))


def _out_proj(x, o_a, o_b, w_a, w_b, tm=512):
    m = x.shape[0]
    return pl.pallas_call(
        _out_proj_kernel,
        grid=(m // tm,),
        in_specs=[
            pl.BlockSpec((tm, D_MODEL), lambda i: (i, 0)),
            pl.BlockSpec((tm, GDN_V_W), lambda i: (i, 0)),
            pl.BlockSpec((tm, RET_V_W), lambda i: (i, 0)),
            pl.BlockSpec((GDN_V_W, D_MODEL), lambda i: (0, 0)),
            pl.BlockSpec((RET_V_W, D_MODEL), lambda i: (0, 0)),
        ],
        out_specs=pl.BlockSpec((tm, D_MODEL), lambda i: (i, 0)),
        out_shape=jax.ShapeDtypeStruct((m, D_MODEL), F32),
        compiler_params=_params(1),
        name="out_proj",
    )(x, o_a, o_b, w_a, w_b)


def _ffn_kernel(*refs, nseq, seg, tiles_per_seq, final_norm):
    (x_ref, nw_ref, wv_ref, wg_ref, cwv_ref, cwg_ref, bv_ref, bg_ref, wd_ref, bufv_ref, bufg_ref,
     fw_ref, o_ref, nbv_ref, nbg_ref, h_scr) = refs[:16]
    chained = tiles_per_seq > 1
    i = pl.program_id(0)
    j = pl.program_id(1)
    nf = pl.num_programs(1)
    rows = nseq * seg
    taps = FFN_CONV - 1

    @pl.when(j == 0)
    def _():
        xv = x_ref[...]
        h_scr[...] = _rms_rows(xv, nw_ref[...]).astype(BF16)
        o_ref[...] = xv

    hb = h_scr[...]

    def branch_chained(w_ref, cw_ref, b_ref, buf_ref, nb_ref, carry_ref, up_scr):
        up = _dot(hb, w_ref[...])
        up_scr[8:8 + rows, :] = up
        up_scr[8 - taps:8, :] = jnp.where(i % tiles_per_seq == 0, buf_ref[0], carry_ref[j])
        tail = up[rows - taps:, :]
        nb_ref[0] = tail
        carry_ref[j] = tail
        out = up * cw_ref[taps:taps + 1, :] + b_ref[...]
        for k in range(1, FFN_CONV):
            out = out + up_scr[8 - k:8 - k + rows, :] * cw_ref[taps - k:taps - k + 1, :]
        return out

    def branch_short(w_ref, cw_ref, b_ref, buf_ref, nb_ref):
        up = _dot(hb, w_ref[...])
        tpos = lax.broadcasted_iota(jnp.int32, (rows, 1), 0) & (seg - 1)
        buf = buf_ref[...]
        out = up * cw_ref[taps:taps + 1, :] + b_ref[...]
        for k in range(1, FFN_CONV):
            out = out + _shift_rows(up, k, tpos, buf, nseq, seg) * cw_ref[taps - k:taps - k + 1, :]
        nb_ref[...] = up.reshape(nseq, seg, up.shape[1])[:, seg - taps:, :]
        return out

    if chained:
        cv_scr, cg_scr, uv_scr, ug_scr = refs[16:20]
        val = branch_chained(wv_ref, cwv_ref, bv_ref, bufv_ref, nbv_ref, cv_scr, uv_scr)
        gate = branch_chained(wg_ref, cwg_ref, bg_ref, bufg_ref, nbg_ref, cg_scr, ug_scr)
    else:
        val = branch_short(wv_ref, cwv_ref, bv_ref, bufv_ref, nbv_ref)
        gate = branch_short(wg_ref, cwg_ref, bg_ref, bufg_ref, nbg_ref)
    act = (_silu(gate) * val).astype(BF16)
    o_ref[...] += _dot(act, wd_ref[...])

    if final_norm:
        @pl.when(j == nf - 1)
        def _():
            o_ref[...] = _rms_rows(o_ref[...], fw_ref[...])


def _ffn(x, nw, w_up, conv_w, conv_b, w_down, buf, final_w, *, layer, nseq_total, final_norm,
         x_time_major=False, out_time_major=False, tm=1024, tf=512):
    m = x.shape[0] * x.shape[1] // D_MODEL
    seq_len = m // nseq_total
    if seq_len >= tm:
        nseq, seg, tiles_per_seq = 1, tm, seq_len // tm
    else:
        nseq, seg, tiles_per_seq = tm // seq_len, seq_len, 1
    nf = D_FF // tf
    sidx = (lambda i, j: i // tiles_per_seq) if tiles_per_seq > 1 else (lambda i, j: i)
    seq_major = lambda i, j: (i, 0)
    time_major = lambda i, j: (i % tiles_per_seq, i // tiles_per_seq)
    once = pl.Buffered(1)
    in_specs = [
        pl.BlockSpec((tm, D_MODEL), time_major if x_time_major else seq_major, pipeline_mode=once),
        pl.BlockSpec((1, D_MODEL), lambda i, j: (0, 0)),
        pl.BlockSpec((None, D_MODEL, tf), lambda i, j: (layer, 0, j)),
        pl.BlockSpec((None, D_MODEL, tf), lambda i, j: (layer, 0, nf + j)),
        pl.BlockSpec((None, FFN_CONV, tf), lambda i, j: (layer, 0, j)),
        pl.BlockSpec((None, FFN_CONV, tf), lambda i, j: (layer, 0, nf + j)),
        pl.BlockSpec((None, 1, tf), lambda i, j: (layer, 0, j)),
        pl.BlockSpec((None, 1, tf), lambda i, j: (layer, 0, nf + j)),
        pl.BlockSpec((None, tf, D_MODEL), lambda i, j: (layer, j, 0)),
        pl.BlockSpec((None, nseq, FFN_CONV - 1, tf), lambda i, j: (layer, sidx(i, j), 0, j)),
        pl.BlockSpec((None, nseq, FFN_CONV - 1, tf), lambda i, j: (layer, sidx(i, j), 0, nf + j)),
        pl.BlockSpec((1, D_MODEL), lambda i, j: (0, 0)),
    ]
    out_specs = [
        pl.BlockSpec((tm, D_MODEL), time_major if out_time_major else seq_major, pipeline_mode=once),
        pl.BlockSpec((nseq, FFN_CONV - 1, tf), lambda i, j: (i, 0, j)),
        pl.BlockSpec((nseq, FFN_CONV - 1, tf), lambda i, j: (i, 0, j)),
    ]
    out_shape = [
        jax.ShapeDtypeStruct((seq_len, nseq_total * D_MODEL) if out_time_major else (m, D_MODEL), F32),
        jax.ShapeDtypeStruct((m // tm * nseq, FFN_CONV - 1, D_FF), F32),
        jax.ShapeDtypeStruct((m // tm * nseq, FFN_CONV - 1, D_FF), F32),
    ]
    scratch = [pltpu.VMEM((tm, D_MODEL), BF16)]
    if tiles_per_seq > 1:
        scratch += [pltpu.VMEM((nf, FFN_CONV - 1, tf), F32), pltpu.VMEM((nf, FFN_CONV - 1, tf), F32),
                    pltpu.VMEM((8 + tm, tf), F32), pltpu.VMEM((8 + tm, tf), F32)]
    y, nbv, nbg = pl.pallas_call(
        functools.partial(_ffn_kernel, nseq=nseq, seg=seg, tiles_per_seq=tiles_per_seq, final_norm=final_norm),
        grid=(m // tm, nf), in_specs=in_specs, out_specs=out_specs, out_shape=out_shape,
        scratch_shapes=scratch, compiler_params=_params(2),
        name="conv_ffn",
    )(x, nw, w_up, w_up, conv_w, conv_w, conv_b, conv_b, w_down, buf, buf, final_w)
    new_buf = jnp.concatenate([nbv, nbg], axis=-1)
    return y, new_buf[tiles_per_seq - 1::tiles_per_seq]


def _s5_disc_kernel(lre_ref, lim_ref, ldt_ref, bre_ref, bim_ref, are_ref, aim_ref, bbre_ref, bbim_ref):
    lam_re = lre_ref[...]
    lam_im = lim_ref[...]
    dt = jnp.exp(ldt_ref[...])
    mag = jnp.exp(lam_re * dt)
    ph = lam_im * dt
    ab_re = mag * jnp.cos(ph)
    ab_im = mag * jnp.sin(ph)
    den = lam_re * lam_re + lam_im * lam_im
    cf_re = ((ab_re - 1.0) * lam_re + ab_im * lam_im) / den
    cf_im = (ab_im * lam_re - (ab_re - 1.0) * lam_im) / den
    are_ref[...] = ab_re
    aim_ref[...] = ab_im
    b_re = bre_ref[...]
    b_im = bim_ref[...]
    bbre_ref[...] = cf_re * b_re - cf_im * b_im
    bbim_ref[...] = cf_re * b_im + cf_im * b_re


def _s5_disc(lam_re, lam_im, log_dt, b_re_t, b_im_t):
    g, c, p = b_re_t.shape
    tile = lambda t: jnp.tile(t, (1, c))
    shp = jax.ShapeDtypeStruct((g, c * p), F32)
    a_re, a_im, bb_re, bb_im = pl.pallas_call(
        _s5_disc_kernel, out_shape=[shp, shp, shp, shp], name="s5_disc",
    )(tile(lam_re), tile(lam_im), jnp.broadcast_to(log_dt, (g, c * p)),
      b_re_t.reshape(g, c * p), b_im_t.reshape(g, c * p))
    return a_re[:, :p], a_im[:, :p], bb_re.reshape(g, c, p), bb_im.reshape(g, c, p)


def _gelu(y):
    return y * (0.5 * (1.0 + jnp.tanh(GELU_C * (y + 0.044715 * (y * y * y)))))


def _s5_step_kernel(x_ref, nw_ref, wb_ref, wcr_ref, wci_ref, are_ref, aim_ref, d_ref, sr_ref, si_ref,
                    g_ref, hr_ref, hi_ref, u_scr, xr_scr, xi_scr, *, nseq, seg):
    rows = nseq * seg
    u = _rms_rows(x_ref[...].reshape(rows, D_MODEL), nw_ref[...])
    u_scr[...] = u
    ub = u.astype(BF16)
    lpb = S5_BW // LANES
    for kb in range(S5_NB):
        res = _dot(ub[:, kb * LANES:(kb + 1) * LANES], wb_ref[kb])
        for q in range(lpb):
            xr_scr[kb * lpb + q] = res[:, q * LANES:(q + 1) * LANES]
            xi_scr[kb * lpb + q] = res[:, S5_BW + q * LANES:S5_BW + (q + 1) * LANES]

    for c in range(S5_STATE // S5_SCAN_W):
        blocks = range(c * S5_SCAN_LB, (c + 1) * S5_SCAN_LB)
        lane = [pl.ds(lb * LANES, LANES) for lb in blocks]
        a_re = [jnp.broadcast_to(are_ref[:, sl], (nseq, LANES)) for sl in lane]
        a_im = [jnp.broadcast_to(aim_ref[:, sl], (nseq, LANES)) for sl in lane]

        def step(t, carry):
            idx = pl.ds(t, nseq, stride=seg)
            out = []
            for n, lb in enumerate(blocks):
                s_re, s_im = carry[2 * n], carry[2 * n + 1]
                n_re = a_re[n] * s_re - a_im[n] * s_im + xr_scr[lb, idx, :]
                n_im = a_re[n] * s_im + a_im[n] * s_re + xi_scr[lb, idx, :]
                xr_scr[lb, idx, :] = n_re
                xi_scr[lb, idx, :] = n_im
                out += [n_re, n_im]
            return tuple(out)

        init = []
        for sl in lane:
            init += [sr_ref[:, sl], si_ref[:, sl]]
        fin = lax.fori_loop(0, seg, step, tuple(init))
        for n, sl in enumerate(lane):
            hr_ref[:, sl] = fin[2 * n]
            hi_ref[:, sl] = fin[2 * n + 1]

    for kb in range(S5_NB):
        xr = jnp.concatenate([xr_scr[kb * lpb + q] for q in range(lpb)], axis=1).astype(BF16)
        xi = jnp.concatenate([xi_scr[kb * lpb + q] for q in range(lpb)], axis=1).astype(BF16)
        lsl = slice(kb * LANES, (kb + 1) * LANES)
        y = _dot(xr, wcr_ref[kb]) - _dot(xi, wci_ref[kb]) + d_ref[:, lsl] * u_scr[:, lsl]
        g_ref[:, lsl] = _gelu(y).astype(BF16)


def _s5_step(x3, nw, wb, wcr, wci, a_re, a_im, d, h0_re, h0_im):
    b, l, _ = x3.shape
    nseq, seg = 8, l
    rows = nseq * seg
    const2 = lambda i: (0, 0)
    const3 = lambda i: (0, 0, 0)
    smap = lambda i: (i, 0)
    return pl.pallas_call(
        functools.partial(_s5_step_kernel, nseq=nseq, seg=seg),
        grid=(b // nseq,),
        in_specs=[
            pl.BlockSpec((nseq, seg, D_MODEL), lambda i: (i, 0, 0)),
            pl.BlockSpec((1, D_MODEL), const2),
            pl.BlockSpec((S5_NB, LANES, 2 * S5_BW), const3),
            pl.BlockSpec((S5_NB, S5_BW, LANES), const3),
            pl.BlockSpec((S5_NB, S5_BW, LANES), const3),
            pl.BlockSpec((1, S5_STATE), const2),
            pl.BlockSpec((1, S5_STATE), const2),
            pl.BlockSpec((1, D_MODEL), const2),
            pl.BlockSpec((nseq, S5_STATE), smap),
            pl.BlockSpec((nseq, S5_STATE), smap),
        ],
        out_specs=[
            pl.BlockSpec((rows, D_MODEL), smap),
            pl.BlockSpec((nseq, S5_STATE), smap),
            pl.BlockSpec((nseq, S5_STATE), smap),
        ],
        out_shape=[jax.ShapeDtypeStruct((b * l, D_MODEL), BF16),
                   jax.ShapeDtypeStruct((b, S5_STATE), F32),
                   jax.ShapeDtypeStruct((b, S5_STATE), F32)],
        scratch_shapes=[pltpu.VMEM((rows, D_MODEL), F32),
                        pltpu.VMEM((S5_STATE // LANES, rows, LANES), F32),
                        pltpu.VMEM((S5_STATE // LANES, rows, LANES), F32)],
        compiler_params=_params(1),
        name="s5_step",
    )(x3, nw, wb, wcr, wci, a_re, a_im, d, h0_re, h0_im)


def _s5_seq_kernel(x_ref, nw_ref, perm_ref, back_ref, wb_ref, wc_ref, are_ref, aim_ref, d_ref, h0r_ref, h0i_ref,
                   g_ref, hr_ref, hi_ref, st_scr, u_scr, ub_scr, x_scr, *, nseq, seg):
    wrows = 2 * nseq * seg

    @pl.when(pl.program_id(0) == 0)
    def _():
        st_scr[0:nseq, :] = h0r_ref[...]
        st_scr[nseq:2 * nseq, :] = h0i_ref[...]

    x = x_ref[...]
    nw = nw_ref[...]
    u = jnp.concatenate([_rms_rows(x[:, s * D_MODEL:(s + 1) * D_MODEL], nw) for s in range(nseq)], axis=0)
    u_hi = u.astype(BF16)
    u_lo = (u - u_hi.astype(F32)).astype(BF16)
    perm = perm_ref[...]
    cw = 512
    for c in range(D_MODEL // cw):
        csl = slice(c * cw, (c + 1) * cw)
        hi = _dot(perm, u_hi[:, csl])
        u_scr[:, csl] = hi + _dot(perm, u_lo[:, csl])
        ub_scr[:, csl] = hi.astype(BF16)

    row = lax.broadcasted_iota(jnp.int32, (wrows, 1), 0)
    is_re = (row & (2 * nseq - 1)) < nseq
    lpb = S5_BW // LANES
    for kb in range(S5_NB):
        blk = ub_scr[:, kb * LANES:(kb + 1) * LANES]
        zero = jnp.zeros_like(blk)
        lhs = jnp.concatenate([jnp.where(is_re, blk, zero), jnp.where(is_re, zero, blk)], axis=1)
        res = _dot(lhs, wb_ref[kb])
        for q in range(lpb):
            x_scr[kb * lpb + q] = res[:, q * LANES:(q + 1) * LANES]

    sign = jnp.where(lax.broadcasted_iota(jnp.int32, (2 * nseq, 1), 0) < nseq, -1.0, 1.0)
    for c in range(S5_STATE // S5_SCAN_W):
        blocks = range(c * S5_SCAN_LB, (c + 1) * S5_SCAN_LB)
        lane = [pl.ds(lb * LANES, LANES) for lb in blocks]
        a1 = [jnp.broadcast_to(are_ref[:, sl], (2 * nseq, LANES)) for sl in lane]
        a2 = [jnp.broadcast_to(aim_ref[:, sl], (2 * nseq, LANES)) * sign for sl in lane]

        def step(t, carry):
            rsl = pl.ds(pl.multiple_of(t * (2 * nseq), 2 * nseq), 2 * nseq)
            out = []
            for n, lb in enumerate(blocks):
                st = carry[n]
                nxt = a1[n] * st + a2[n] * pltpu.roll(st, nseq, 0) + x_scr[lb, rsl, :]
                x_scr[lb, rsl, :] = nxt
                out.append(nxt)
            return tuple(out)

        fin = lax.fori_loop(0, seg, step, tuple(st_scr[:, sl] for sl in lane))
        for n, sl in enumerate(lane):
            st_scr[:, sl] = fin[n]
    hr_ref[...] = st_scr[0:nseq, :]
    hi_ref[...] = st_scr[nseq:2 * nseq, :]

    for kb in range(S5_NB):
        xb = jnp.concatenate([x_scr[kb * lpb + q] for q in range(lpb)], axis=1).astype(BF16)
        y2 = _dot(xb, wc_ref[kb])
        lsl = slice(kb * LANES, (kb + 1) * LANES)
        y = y2[:, :LANES] - pltpu.roll(y2[:, LANES:], wrows - nseq, 0) + d_ref[:, lsl] * u_scr[:, lsl]
        ub_scr[:, lsl] = _gelu(y).astype(BF16)

    back = back_ref[...]
    for c in range(D_MODEL // cw):
        csl = slice(c * cw, (c + 1) * cw)
        g_nat = _dot(back, ub_scr[:, csl])
        for s in range(nseq):
            g_ref[s, :, csl] = g_nat[s * seg:(s + 1) * seg].astype(BF16)


def _s5_seq(x_tm, nw, perm, back, wb2, wc2, a_re, a_im, d, h0_re, h0_im, *, nseq, seg):
    l = x_tm.shape[0]
    wrows = 2 * nseq * seg
    const2 = lambda n: (0, 0)
    const3 = lambda n: (0, 0, 0)
    return pl.pallas_call(
        functools.partial(_s5_seq_kernel, nseq=nseq, seg=seg),
        grid=(l // seg,),
        in_specs=[
            pl.BlockSpec((seg, nseq * D_MODEL), lambda n: (n, 0)),
            pl.BlockSpec((1, D_MODEL), const2),
            pl.BlockSpec((wrows, nseq * seg), const2),
            pl.BlockSpec((nseq * seg, wrows), const2),
            pl.BlockSpec((S5_NB, 2 * LANES, S5_BW), const3),
            pl.BlockSpec((S5_NB, S5_BW, 2 * LANES), const3),
            pl.BlockSpec((1, S5_STATE), const2),
            pl.BlockSpec((1, S5_STATE), const2),
            pl.BlockSpec((1, D_MODEL), const2),
            pl.BlockSpec((nseq, S5_STATE), const2),
            pl.BlockSpec((nseq, S5_STATE), const2),
        ],
        out_specs=[
            pl.BlockSpec((nseq, seg, D_MODEL), lambda n: (0, n, 0)),
            pl.BlockSpec((nseq, S5_STATE), const2),
            pl.BlockSpec((nseq, S5_STATE), const2),
        ],
        out_shape=[jax.ShapeDtypeStruct((nseq, l, D_MODEL), BF16),
                   jax.ShapeDtypeStruct((nseq, S5_STATE), F32),
                   jax.ShapeDtypeStruct((nseq, S5_STATE), F32)],
        scratch_shapes=[pltpu.VMEM((2 * nseq, S5_STATE), F32),
                        pltpu.VMEM((wrows, D_MODEL), F32),
                        pltpu.VMEM((wrows, D_MODEL), BF16),
                        pltpu.VMEM((S5_STATE // LANES, wrows, LANES), F32)],
        compiler_params=_params(1),
        name="s5_seq",
    )(x_tm, nw, perm, back, wb2, wc2, a_re, a_im, d, h0_re, h0_im)


def _s5_row_perm(nseq, seg):
    r = jnp.arange(2 * nseq * seg)
    src = (r % nseq) * seg + r // (2 * nseq)
    hit = src[:, None] == jnp.arange(nseq * seg)[None, :]
    real = (r % (2 * nseq)) < nseq
    return hit.astype(BF16), (hit & real[:, None]).T.astype(BF16)


def _glu_kernel(x_ref, g_ref, wv_ref, wg_ref, o_ref):
    gb = g_ref[...]
    o_ref[...] = x_ref[...] + _dot(gb, wv_ref[...]) * _sigmoid(_dot(gb, wg_ref[...]))


def _glu(x, g, w_glu, *, time_major_seqs=0, tm=1024, tn=512):
    nn = D_MODEL // tn
    if time_major_seqs:
        tps = x.shape[0] // tm
        n_tiles = tps * time_major_seqs
        xmap = lambda i, j: (i % tps, (i // tps) * nn + j)
    else:
        n_tiles = x.shape[0] // tm
        xmap = lambda i, j: (i, j)
    return pl.pallas_call(
        _glu_kernel,
        grid=(n_tiles, nn),
        in_specs=[
            pl.BlockSpec((tm, tn), xmap),
            pl.BlockSpec((tm, D_MODEL), lambda i, j: (i, 0)),
            pl.BlockSpec((D_MODEL, tn), lambda i, j: (0, j)),
            pl.BlockSpec((D_MODEL, tn), lambda i, j: (0, nn + j)),
        ],
        out_specs=pl.BlockSpec((tm, tn), xmap),
        out_shape=jax.ShapeDtypeStruct(x.shape, F32),
        compiler_params=_params(2),
        name="glu",
    )(x, g, w_glu, w_glu)


def _rotary_tables(pos):
    half = RET_DK // 2
    inv = ROPE_BASE ** (-jnp.arange(half, dtype=F32) / half)
    ang = pos[:, None] * inv[None, :]
    cos = jnp.cos(ang)
    sin = jnp.sin(ang)
    return jnp.concatenate([cos, cos], axis=-1), jnp.concatenate([-sin, sin], axis=-1)


def _retention_tables(nseq, seg):
    log_g = jnp.log1p(-jnp.exp2(-5.0 - jnp.arange(RET_HEADS, dtype=F32)))
    row = jnp.arange(nseq * seg)
    t = (row % seg).astype(F32)
    sid = row // seg
    diff = t[:, None] - t[None, :]
    ok = (sid[:, None] == sid[None, :]) & (diff >= 0)
    dec = jnp.where(ok, jnp.exp(log_g[:, None, None] * jnp.where(ok, diff, 0.0)), 0.0)
    qd = jnp.exp(log_g[:, None] * (t + 1.0))[..., None]
    kd = jnp.exp(log_g[:, None] * (seg - 1.0 - t))[..., None]
    qd = jnp.broadcast_to(qd, (RET_HEADS, nseq * seg, RET_DK))
    kd = jnp.broadcast_to(kd, (RET_HEADS, nseq * seg, RET_DK))
    return dec, qd, kd


def _ret_chunk_gains(seg):
    import numpy as np
    log_g = np.log1p(-np.exp2(-5.0 - np.arange(RET_HEADS, dtype=np.float32))).astype(np.float32)
    return tuple(float(v) for v in np.exp(log_g * np.float32(seg)).astype(np.float32))


def _block_diag(blocks):
    nb, gb, r, c = blocks.shape
    eye = jnp.eye(gb, dtype=blocks.dtype)
    return (blocks[:, :, :, None, :] * eye[None, :, None, :, None]).reshape(nb, gb * r, gb * c)


def _trunk(x3, pos, gdn_s, gdn_cb, ret_s, s5_re, s5_im, ffn_cb, prm, *, chained):
    b, l, _ = x3.shape
    m = b * l
    x = x3.reshape(m, D_MODEL)
    nchunk = l // CHUNK_ROWS if chained else 1

    proj, gate = _in_proj(x, prm['norm_mix_w'][0:1], prm['w_in_main'], prm['w_in_gate'])
    o_a, gdn_new, gcb_new = _gdn(proj, gate, gdn_cb, prm['gdn_conv_w'], prm['alog_row'], prm['dtb_row'],
                                 prm['gdn_norm_w'], None if chained else gdn_s, nbatch=b, nchunk=nchunk)
    if chained:
        cosf, sinf = _rotary_tables(pos)
        dec, qd, kd = _retention_tables(1, CHUNK_ROWS)
        g_chunk = _ret_chunk_gains(CHUNK_ROWS)
    else:
        cosf, sinf = _rotary_tables(jnp.tile(pos, CHUNK_ROWS // l))
        dec, qd, kd = _retention_tables(CHUNK_ROWS // l, l)
        g_chunk = _ret_chunk_gains(l)
    o_b, ret_new = _ret(proj, cosf, sinf, dec, qd, kd, prm['ret_gn_w'], prm['ret_gn_b'],
                        None if chained else ret_s, g_chunk, nbatch=b, nchunk=nchunk)
    x = _out_proj(x, o_a, o_b, prm['w_out_a'], prm['w_out_b'])
    x, fcb0 = _ffn(x, prm['norm_ffn_w'][0:1], prm['w_up'], prm['ffn_conv_w'], prm['ffn_conv_b'],
                   prm['w_down'], ffn_cb, prm['norm_final_w'], layer=0, nseq_total=b, final_norm=False,
                   out_time_major=chained)

    if chained:
        g, s5r_new, s5i_new = _s5_seq(x, prm['norm_mix_w'][1:2], *_s5_row_perm(b, CHUNK_ROWS), prm['s5_wb2'],
                                      prm['s5_wc2'], prm['s5_a_re'], prm['s5_a_im'], prm['s5_d'],
                                      s5_re, s5_im, nseq=b, seg=CHUNK_ROWS)
        x = _glu(x, g.reshape(m, D_MODEL), prm['w_glu'], time_major_seqs=b)
    else:
        g, s5r_new, s5i_new = _s5_step(x.reshape(b, l, D_MODEL), prm['norm_mix_w'][1:2], prm['s5_wb'],
                                       prm['s5_wc_re'], prm['s5_wc_im'], prm['s5_a_re'], prm['s5_a_im'],
                                       prm['s5_d'], s5_re, s5_im)
        x = _glu(x, g, prm['w_glu'])
    y, fcb1 = _ffn(x, prm['norm_ffn_w'][1:2], prm['w_up'], prm['ffn_conv_w'], prm['ffn_conv_b'],
                   prm['w_down'], ffn_cb, prm['norm_final_w'], layer=1, nseq_total=b, final_norm=True,
                   x_time_major=chained)
    return (y.reshape(b, l, D_MODEL), gdn_new[None], gcb_new[None], ret_new[None],
            s5r_new.reshape(1, b, S5_GROUPS, S5_P), s5i_new.reshape(1, b, S5_GROUPS, S5_P),
            jnp.stack([fcb0, fcb1]))


def kernel(x_prompt, x_sample, state_gdn, state_gdn_conv, state_ret, state_s5_re, state_s5_im,
           state_ffn_conv, norm_mix_w, norm_ffn_w, norm_final_w,
           w_in, gdn_conv_w, gdn_a_log, gdn_dt_bias, gdn_norm_w, ret_gn_w, ret_gn_b, w_out,
           s5_lam_re, s5_lam_im, s5_log_dt, s5_b_re, s5_b_im, s5_c_re, s5_c_im, s5_d, w_glu,
           w_up, ffn_conv_w, ffn_conv_b, w_down):
    bp, lp, _ = x_prompt.shape
    bs, ls, _ = x_sample.shape
    past_len = 16384

    w = w_in[0]
    o_z = GDN_CONV_W
    o_a = o_z + GDN_V_W
    o_b = o_a + GDN_HEADS
    o_rest = o_b + GDN_HEADS
    w_in_main = jnp.concatenate([w[:, :o_a], w[:, o_rest:]], axis=1).astype(BF16)
    w_in_gate = jnp.pad(w[:, o_a:o_rest], ((0, 0), (0, GATE_W - 2 * GDN_HEADS))).astype(BF16)
    pad_row = lambda v: jnp.pad(v.reshape(1, -1), ((0, 0), (0, GATE_W - v.shape[-1])))

    a_re, a_im, bb_re, bb_im = _s5_disc(s5_lam_re[0], s5_lam_im[0], s5_log_dt[0].reshape(S5_GROUPS, 1),
                                        s5_b_re[0].transpose(0, 2, 1), s5_b_im[0].transpose(0, 2, 1))
    blk = lambda t: t.reshape(S5_NB, S5_GB, t.shape[1], t.shape[2])
    wb_re, wb_im = _block_diag(blk(bb_re)).astype(BF16), _block_diag(blk(bb_im)).astype(BF16)
    s5_wb = jnp.concatenate([wb_re, wb_im], axis=-1)
    s5_wb2 = jnp.concatenate([wb_re, wb_im], axis=1)
    s5_wc_re = _block_diag(blk(s5_c_re[0].transpose(0, 2, 1))).astype(BF16)
    s5_wc_im = _block_diag(blk(s5_c_im[0].transpose(0, 2, 1))).astype(BF16)
    s5_wc2 = jnp.concatenate([s5_wc_re, s5_wc_im], axis=-1)

    prm = dict(
        norm_mix_w=norm_mix_w, norm_ffn_w=norm_ffn_w, norm_final_w=norm_final_w.reshape(1, D_MODEL),
        w_in_main=w_in_main, w_in_gate=w_in_gate,
        gdn_conv_w=gdn_conv_w[0], alog_row=pad_row(gdn_a_log[0]), dtb_row=pad_row(gdn_dt_bias[0]),
        gdn_norm_w=gdn_norm_w[0].reshape(1, GDN_DV),
        ret_gn_w=ret_gn_w[0].reshape(1, RET_V_W), ret_gn_b=ret_gn_b[0].reshape(1, RET_V_W),
        w_out_a=w_out[0, :GDN_V_W].astype(BF16), w_out_b=w_out[0, GDN_V_W:].astype(BF16),
        s5_wb=s5_wb, s5_wc_re=s5_wc_re, s5_wc_im=s5_wc_im, s5_wb2=s5_wb2, s5_wc2=s5_wc2,
        s5_a_re=a_re.reshape(1, S5_STATE), s5_a_im=a_im.reshape(1, S5_STATE), s5_d=s5_d[0].reshape(1, D_MODEL),
        w_glu=w_glu[0].astype(BF16),
        w_up=w_up.astype(BF16), ffn_conv_w=ffn_conv_w, ffn_conv_b=ffn_conv_b[:, None, :],
        w_down=w_down.astype(BF16),
    )

    z_gcb = jnp.zeros((bp, GDN_CONV - 1, GDN_CONV_W), F32)
    z_s5 = jnp.zeros((bp, S5_STATE), F32)
    z_fcb = jnp.zeros((2, bp, FFN_CONV - 1, 2 * D_FF), F32)
    pos_p = jnp.arange(lp, dtype=F32)
    pos_s = past_len + jnp.arange(ls, dtype=F32)

    y_p, gdn_p, gcb_p, ret_p, s5r_p, s5i_p, fcb_p = _trunk(
        x_prompt, pos_p, None, z_gcb, None, z_s5, z_s5, z_fcb, prm, chained=True)
    y_s, gdn_s, gcb_s, ret_s, s5r_s, s5i_s, fcb_s = _trunk(
        x_sample, pos_s, state_gdn[0], state_gdn_conv[0], state_ret[0],
        state_s5_re[0].reshape(bs, S5_STATE), state_s5_im[0].reshape(bs, S5_STATE),
        state_ffn_conv, prm, chained=False)
    return (y_p, y_s, gdn_p, gdn_s, gcb_p, gcb_s, ret_p, ret_s,
            s5r_p, s5r_s, s5i_p, s5i_s, fcb_p, fcb_s)
```

```python
import functools
import math

import jax
import jax.numpy as jnp
import numpy as np
from jax import lax
from jax.experimental import pallas as pl
from jax.experimental.pallas import tpu as pltpu

F32 = jnp.float32
BF16 = jnp.bfloat16

D_MODEL = 2048
GDN_HEADS = 8
GDN_DK = 128
GDN_DV = 128
GDN_CONV = 4
RET_HEADS = 4
RET_DK = 128
RET_DV = 256
ROPE_BASE = 10000.0
S5_GROUP = 16
S5_GROUPS = 128
S5_P = 64
S5_STATE = S5_GROUPS * S5_P
D_FF = 5632
FFN_CONV = 3
EPS = 1e-6

GDN_QK_W = GDN_HEADS * GDN_DK
GDN_V_W = GDN_HEADS * GDN_DV
GDN_CONV_W = 2 * GDN_QK_W + GDN_V_W
RET_QK_W = RET_HEADS * RET_DK
RET_V_W = RET_HEADS * RET_DV
PROJ_W = GDN_CONV_W + GDN_V_W + 2 * RET_QK_W + 2 * RET_V_W
GATE_W = 128

CHUNK_ROWS = 64
GDN_GROUP = 4
GDN_STEP_CHUNKS = 2
GDN_DIAG = 16
RET_STEP_CHUNKS = 4
LANES = 128
VMEM_LIMIT = 56 * 1024 * 1024

S5_GB = 8
S5_NB = S5_GROUPS // S5_GB
S5_BW = S5_GB * S5_P
S5_SCAN_W = 1024
S5_SCAN_LB = S5_SCAN_W // LANES
GELU_C = math.sqrt(2.0 / math.pi)


def _params(n_grid):
    return pltpu.CompilerParams(dimension_semantics=("arbitrary",) * n_grid,
                                vmem_limit_bytes=VMEM_LIMIT)


def _dot(a, b):
    return jnp.dot(a, b, preferred_element_type=F32)


def _dot_nt(a, b):
    return lax.dot_general(a, b, (((1,), (1,)), ((), ())), preferred_element_type=F32)


def _dot_tn(a, b):
    return lax.dot_general(a, b, (((0,), (0,)), ((), ())), preferred_element_type=F32)


def _sigmoid(x):
    return 0.5 + 0.5 * jnp.tanh(0.5 * x)


def _silu(x):
    h = 0.5 * x
    return h + h * jnp.tanh(h)


def _rms_rows(x, w):
    return x * lax.rsqrt(jnp.mean(x * x, axis=-1, keepdims=True) + EPS) * w


def _shift_rows(x, k, tpos, buf, nseq, seg):
    rows, width = x.shape
    nb = buf.shape[1]
    prev = pltpu.roll(x, k, 0)
    for t in range(k):
        src = buf[:, nb - k + t:nb - k + t + 1, :]
        srcb = jnp.broadcast_to(src, (nseq, seg, width)).reshape(rows, width)
        prev = jnp.where(tpos == t, srcb, prev)
    return prev


def _in_proj_kernel(x_ref, nw_ref, w_ref, wg_ref, o_ref, og_ref, h_scr):
    @pl.when(pl.program_id(1) == 0)
    def _():
        hb = _rms_rows(x_ref[...], nw_ref[...]).astype(BF16)
        h_scr[...] = hb
        og_ref[...] = _dot(hb, wg_ref[...])

    o_ref[...] = _dot(h_scr[...], w_ref[...])


def _in_proj(x, nw, w_main, w_gate, tm=1024, tn=1024):
    m = x.shape[0]
    return pl.pallas_call(
        _in_proj_kernel,
        grid=(m // tm, PROJ_W // tn),
        in_specs=[
            pl.BlockSpec((tm, D_MODEL), lambda i, j: (i, 0), pipeline_mode=pl.Buffered(1)),
            pl.BlockSpec((1, D_MODEL), lambda i, j: (0, 0)),
            pl.BlockSpec((D_MODEL, tn), lambda i, j: (0, j)),
            pl.BlockSpec((D_MODEL, GATE_W), lambda i, j: (0, 0)),
        ],
        out_specs=[
            pl.BlockSpec((tm, tn), lambda i, j: (i, j)),
            pl.BlockSpec((tm, GATE_W), lambda i, j: (i, 0)),
        ],
        out_shape=[jax.ShapeDtypeStruct((m, PROJ_W), F32),
                   jax.ShapeDtypeStruct((m, GATE_W), F32)],
        scratch_shapes=[pltpu.VMEM((tm, D_MODEL), BF16)],
        compiler_params=_params(2),
        name="in_proj",
    )(x, nw, w_main, w_gate)


def _gdn_kernel(*refs, nseq, seg, chained):
    if chained:
        (qkv_ref, z_ref, gate_ref, cbuf_ref, cw_ref, alog_ref, dtb_ref, nw_ref, sum_ref, mask_ref,
         o_ref, s_out_ref, cb_out_ref, carry_scr) = refs
        s_in_ref = s_out_ref
        first = pl.program_id(1) == 0

        @pl.when(first)
        def _():
            s_out_ref[...] = jnp.zeros_like(s_out_ref)
    else:
        (qkv_ref, z_ref, gate_ref, cbuf_ref, cw_ref, alog_ref, dtb_ref, nw_ref, sum_ref, mask_ref, s_in_ref,
         o_ref, s_out_ref, cb_out_ref) = refs

    rows = nseq * seg
    step_rows = qkv_ref.shape[0]
    cps = step_rows // rows
    shift = int(math.log2(seg))
    x = qkv_ref[...]
    if chained:
        buf = jnp.where(first, cbuf_ref[...], carry_scr[5:8, :].reshape(1, GDN_CONV - 1, GDN_CONV_W))
        conv_seg = step_rows
    else:
        buf = cbuf_ref[...]
        conv_seg = seg
    row_id = lax.broadcasted_iota(jnp.int32, (step_rows, 1), 0)
    tpos = row_id & (conv_seg - 1)

    acc = x * cw_ref[GDN_CONV - 1:GDN_CONV, :]
    for k in range(1, GDN_CONV):
        acc = acc + _shift_rows(x, k, tpos, buf, nseq, conv_seg) * cw_ref[GDN_CONV - 1 - k:GDN_CONV - k, :]
    act = _silu(acc)
    cb_out_ref[...] = x.reshape(nseq, conv_seg, GDN_CONV_W)[:, conv_seg - (GDN_CONV - 1):, :]
    if chained:
        carry_scr[...] = x[step_rows - 8:, :]

    gate = gate_ref[...]
    xa = gate + dtb_ref[...]
    softplus = jnp.maximum(xa, 0.0) + jnp.log(1.0 + jnp.exp(-jnp.abs(xa)))
    log_a = -jnp.exp(alog_ref[...]) * softplus
    beta_all = _sigmoid(gate)

    sum_m = sum_ref[...]
    a1 = log_a.astype(BF16)
    r1 = log_a - a1.astype(F32)
    a2 = r1.astype(BF16)
    a3 = (r1 - a2.astype(F32)).astype(BF16)
    if nseq > 1:
        row_masks = [((row_id >> shift) == s).astype(F32) for s in range(nseq)]
    nw = nw_ref[...]
    chunk_sl = [slice(c * rows, (c + 1) * rows) for c in range(cps)]
    g_all = [_dot(sum_m, a1[sl]) + (_dot(sum_m, a2[sl]) + _dot(sum_m, a3[sl])) for sl in chunk_sl]
    g = [t[:rows] for t in g_all]
    g_last = [t[rows:] for t in g_all]
    g2_t = [jnp.concatenate([t, t], axis=0).T for t in g]
    e_g = [jnp.exp(t) for t in g]
    e_rest = [jnp.exp(tl - t) for tl, t in zip(g_last, g)]
    e_last = [jnp.exp(t) for t in g_last]

    incl_f = mask_ref[0]
    strict_f = mask_ref[1]
    lane = lax.broadcasted_iota(jnp.int32, (1, LANES), 1)

    def stack_cols(tile, heads, off=0):
        return jnp.concatenate([tile[:, off + h:off + h + 1] for h in heads], axis=0)

    groups = [list(range(g0, g0 + GDN_GROUP)) for g0 in range(0, GDN_HEADS, GDN_GROUP)]
    units = [(c, heads) for c in range(cps) for heads in groups]
    ks, q_all, k_all, v_all = [], [], [], []
    for c, heads in units:
        a_c = act[chunk_sl[c]]
        qs, kn, vs = [], [], []
        for h in heads:
            lo = h * GDN_DK
            q = a_c[:, lo:lo + GDN_DK]
            k = a_c[:, GDN_QK_W + lo:GDN_QK_W + lo + GDN_DK]
            qs.append(q * lax.rsqrt(jnp.sum(q * q, axis=-1, keepdims=True) + EPS) * (GDN_DK ** -0.5))
            kn.append(k * lax.rsqrt(jnp.sum(k * k, axis=-1, keepdims=True) + EPS))
            vs.append(a_c[:, 2 * GDN_QK_W + lo:2 * GDN_QK_W + lo + GDN_DV])
        ks.append(kn)
        q_all.append(jnp.concatenate(qs, axis=0))
        k_all.append(jnp.concatenate(kn, axis=0))
        v_all.append(jnp.concatenate(vs, axis=0))
    gc = [stack_cols(g[c], heads) for c, heads in units]
    bc = [stack_cols(beta_all[chunk_sl[c]], heads, GDN_HEADS) for c, heads in units]
    egc = [stack_cols(e_g[c], heads) for c, heads in units]
    gr = [jnp.concatenate(
        [jnp.where(lane < rows, g2_t[c][heads[j]:heads[j] + 1, :], g2_t[c][heads[j + 1]:heads[j + 1] + 1, :])
         for j in range(0, GDN_GROUP, 2)], axis=1) for c, heads in units]
    decay = [jnp.exp(jnp.minimum(a - r, 0.0)) * incl_f for a, r in zip(gc, gr)]
    kb = [k.astype(BF16) for k in k_all]
    kk = [_dot_nt(k, k) for k in kb]
    qk = [_dot_nt(q.astype(BF16), k) * d for q, k, d in zip(q_all, kb, decay)]
    lms = [b * m * d * strict_f for b, m, d in zip(bc, kk, decay)]
    inv_e = _unit_lower_inverse_minus_eye(lms, mask_ref, seg)
    rhs = [jnp.concatenate([v * b, k * (b * e)], axis=-1) for v, k, b, e in zip(v_all, k_all, bc, egc)]
    sol = [r + _dot(e.astype(BF16), r.astype(BF16)) for e, r in zip(inv_e, rhs)]
    qd_all = [q * e for q, e in zip(q_all, egc)]

    states = [[s_in_ref[s, h] for s in range(nseq)] for h in range(GDN_HEADS)]
    for c in range(cps):
        unit_ids = [u for u, (uc, _) in enumerate(units) if uc == c]
        v_new, o_all = {}, {}
        for u in unit_ids:
            ws_parts, qs_parts = [], []
            for j, h in enumerate(units[u][1]):
                r0 = j * rows
                lhs = jnp.concatenate([sol[u][r0:r0 + rows, GDN_DV:], qd_all[u][r0:r0 + rows]], axis=0).astype(BF16)
                ws = None
                for s in range(nseq):
                    part = _dot(lhs, states[h][s].astype(BF16))
                    if nseq > 1:
                        part = part * jnp.concatenate([row_masks[s], row_masks[s]], axis=0)
                    ws = part if ws is None else ws + part
                ws_parts.append(ws[:rows])
                qs_parts.append(ws[rows:])
            v_new[u] = sol[u][:, :GDN_DV] - jnp.concatenate(ws_parts, axis=0)
            o_all[u] = jnp.concatenate(qs_parts, axis=0)
        for u in unit_ids:
            o_all[u] = o_all[u] + _dot(qk[u].astype(BF16), v_new[u].astype(BF16))
        for u in unit_ids:
            for j, h in enumerate(units[u][1]):
                r0 = j * rows
                lo = h * GDN_DK
                k_rest = (ks[u][j] * e_rest[c][:, h:h + 1]).astype(BF16)
                vn = v_new[u][r0:r0 + rows]
                for s in range(nseq):
                    vsel = vn * row_masks[s] if nseq > 1 else vn
                    states[h][s] = (states[h][s] * e_last[c][s * seg:s * seg + 1, h:h + 1]
                                    + _dot_tn(k_rest, vsel.astype(BF16)))
                o = o_all[u][r0:r0 + rows]
                o = o * lax.rsqrt(jnp.mean(o * o, axis=-1, keepdims=True) + EPS) * nw
                o = o * _silu(z_ref[chunk_sl[c], lo:lo + GDN_DV])
                o_ref[chunk_sl[c], lo:lo + GDN_DV] = o.astype(BF16)
    for h in range(GDN_HEADS):
        for s in range(nseq):
            s_out_ref[s, h] = states[h][s]


def _gdn_masks(nseq, seg):
    n = GDN_GROUP * nseq * seg
    ri, ci = np.arange(n)[:, None], np.arange(n)[None, :]
    same = (ri // seg) == (ci // seg)
    blk = min(GDN_DIAG, seg)
    masks = [same & (ci <= ri), same & (ci < ri), -1.0 * ((ri // blk) == (ci // blk))]
    size = blk
    while size < seg:
        masks.append(((ri // (2 * size)) == (ci // (2 * size))) & ((ri // size) != (ci // size)))
        size *= 2
    return jnp.asarray(np.stack([np.asarray(m, np.float32) for m in masks]), F32)


def _gdn_sum_matrix(nseq, seg):
    n = nseq * seg
    ri, ci = np.arange(n)[:, None], np.arange(n)[None, :]
    same = (ri // seg) == (ci // seg)
    return jnp.asarray(np.concatenate([same & (ci <= ri), same]), BF16)


def _unit_lower_inverse_minus_eye(lms, mask_ref, seg):
    blk = min(GDN_DIAG, seg)
    neg_diag = mask_ref[2]
    err = [lm * neg_diag for lm in lms]
    m_b = [e.astype(BF16) for e in err]
    for _ in range(int(math.log2(blk)) - 1):
        m_f = [_dot(m, m) for m in m_b]
        m_b = [m.astype(BF16) for m in m_f]
        err = [e + m + _dot(e.astype(BF16), mb) for e, m, mb in zip(err, m_f, m_b)]
    level = 3
    size = blk
    while size < seg:
        c = [lm * mask_ref[level] for lm in lms]
        e_b = [e.astype(BF16) for e in err]
        y = [x + _dot(e, x.astype(BF16)) for x, e in zip(c, e_b)]
        err = [e - (v + _dot(v.astype(BF16), eb)) for e, v, eb in zip(err, y, e_b)]
        level += 1
        size *= 2
    return err


def _gdn(proj, gate, cbuf, conv_w, alog_row, dtb_row, norm_w, state, *, nbatch, nchunk):
    chained = state is None
    rows_total = proj.shape[0]
    if chained:
        nseq, seg = 1, CHUNK_ROWS
        step_rows = GDN_STEP_CHUNKS * CHUNK_ROWS
        nstep = nchunk // GDN_STEP_CHUNKS
        grid = (nbatch, nstep)
        rb = lambda b, n: b * nstep + n
        sb = lambda b, n: b
    else:
        nseq, seg = 8, 8
        step_rows = CHUNK_ROWS
        grid = (rows_total // CHUNK_ROWS,)
        rb = lambda i: i
        sb = lambda i: i
    n_state = nbatch
    sum_m = _gdn_sum_matrix(nseq, seg)
    masks = _gdn_masks(nseq, seg)
    wrap = lambda f, *tail: (lambda *g: (f(*g),) + tail)
    const = lambda *tail: (lambda *g: tail)
    in_specs = [
        pl.BlockSpec((step_rows, GDN_CONV_W), wrap(rb, 0)),
        pl.BlockSpec((step_rows, GDN_V_W), wrap(rb, GDN_CONV_W // GDN_V_W)),
        pl.BlockSpec((step_rows, GATE_W), wrap(rb, 0)),
        pl.BlockSpec((nseq, GDN_CONV - 1, GDN_CONV_W), wrap(sb, 0, 0)),
        pl.BlockSpec((GDN_CONV, GDN_CONV_W), const(0, 0)),
        pl.BlockSpec((1, GATE_W), const(0, 0)),
        pl.BlockSpec((1, GATE_W), const(0, 0)),
        pl.BlockSpec((1, GDN_DV), const(0, 0)),
        pl.BlockSpec(sum_m.shape, const(0, 0)),
        pl.BlockSpec(masks.shape, const(0, 0, 0)),
    ]
    args = [proj, proj, gate, cbuf, conv_w, alog_row, dtb_row, norm_w, sum_m, masks]
    scratch = []
    if chained:
        scratch = [pltpu.VMEM((8, GDN_CONV_W), F32)]
    else:
        in_specs.append(pl.BlockSpec((nseq, GDN_HEADS, GDN_DK, GDN_DV), wrap(sb, 0, 0, 0)))
        args.append(state)
    out_specs = [
        pl.BlockSpec((step_rows, GDN_V_W), wrap(rb, 0)),
        pl.BlockSpec((nseq, GDN_HEADS, GDN_DK, GDN_DV), wrap(sb, 0, 0, 0)),
        pl.BlockSpec((nseq, GDN_CONV - 1, GDN_CONV_W), wrap(sb, 0, 0)),
    ]
    out_shape = [
        jax.ShapeDtypeStruct((rows_total, GDN_V_W), BF16),
        jax.ShapeDtypeStruct((n_state, GDN_HEADS, GDN_DK, GDN_DV), F32),
        jax.ShapeDtypeStruct((n_state, GDN_CONV - 1, GDN_CONV_W), F32),
    ]
    return pl.pallas_call(
        functools.partial(_gdn_kernel, nseq=nseq, seg=seg, chained=chained),
        grid=grid, in_specs=in_specs, out_specs=out_specs, out_shape=out_shape,
        scratch_shapes=scratch, compiler_params=_params(len(grid)),
        name="gdn_chained" if chained else "gdn_step",
    )(*args)


def _ret_kernel(*refs, nseq, seg, chained, g_chunk):
    if chained:
        (q_ref, k_ref, v_ref, gt_ref, cos_ref, sin_ref, dec_ref, qd_ref, kd_ref, gw_ref, gb_ref,
         o_ref, s_out_ref) = refs
        s_in_ref = s_out_ref

        @pl.when(pl.program_id(1) == 0)
        def _():
            s_out_ref[...] = jnp.zeros_like(s_out_ref)
    else:
        (q_ref, k_ref, v_ref, gt_ref, cos_ref, sin_ref, dec_ref, qd_ref, kd_ref, gw_ref, gb_ref,
         s_in_ref, o_ref, s_out_ref) = refs

    rows = nseq * seg
    cps = q_ref.shape[0] // rows
    shift = int(math.log2(seg))
    if nseq > 1:
        row_id = lax.broadcasted_iota(jnp.int32, (rows, 1), 0)
        row_masks = [((row_id >> shift) == s).astype(F32) for s in range(nseq)]

    units = [(c, h) for c in range(cps) for h in range(RET_HEADS)]
    rsl = lambda c: slice(c * rows, (c + 1) * rows)
    qr, kr = [], []
    for c, h in units:
        cosf = cos_ref[rsl(c), :]
        sinf = sin_ref[rsl(c), :]
        q = q_ref[rsl(c), h * RET_DK:(h + 1) * RET_DK]
        k = k_ref[rsl(c), h * RET_DK:(h + 1) * RET_DK]
        qr.append(q * cosf + pltpu.roll(q, RET_DK // 2, 1) * sinf)
        kr.append((k * cosf + pltpu.roll(k, RET_DK // 2, 1) * sinf) * (RET_DK ** -0.5))
    vf = [v_ref[rsl(c), h * RET_DV:(h + 1) * RET_DV] for c, h in units]
    vb = [v.astype(BF16) for v in vf]
    kb = [k.astype(BF16) for k in kr]
    sc = [_dot_nt(q.astype(BF16), k) * dec_ref[h] for q, k, (c, h) in zip(qr, kb, units)]
    o = [_dot(s.astype(BF16), v) for s, v in zip(sc, vb)]
    q_dec = [(q * qd_ref[h]).astype(BF16) for q, (c, h) in zip(qr, units)]
    k_dec = [(k * kd_ref[h]).astype(BF16) for k, (c, h) in zip(kr, units)]
    if nseq > 1:
        for u, (c, h) in enumerate(units):
            for s in range(nseq):
                state = s_in_ref[s, h]
                o[u] = o[u] + _dot(q_dec[u], state.astype(BF16)) * row_masks[s]
                s_out_ref[s, h] = state * g_chunk[h] + _dot_tn(k_dec[u], (vf[u] * row_masks[s]).astype(BF16))
    else:
        kv = [_dot_tn(k, v) for k, v in zip(k_dec, vb)]
        state = [s_in_ref[0, h] for h in range(RET_HEADS)]
        for u, (c, h) in enumerate(units):
            o[u] = o[u] + _dot(q_dec[u], state[h].astype(BF16))
            state[h] = state[h] * g_chunk[h] + kv[u]
        for h in range(RET_HEADS):
            s_out_ref[0, h] = state[h]
    for u, (c, h) in enumerate(units):
        vsl = slice(h * RET_DV, (h + 1) * RET_DV)
        mu = jnp.mean(o[u], axis=-1, keepdims=True)
        oc = o[u] - mu
        var = jnp.mean(oc * oc, axis=-1, keepdims=True)
        on = oc * lax.rsqrt(var + EPS)
        on = (on * gw_ref[:, vsl] + gb_ref[:, vsl]) * _silu(gt_ref[rsl(c), vsl])
        o_ref[rsl(c), vsl] = on.astype(BF16)


def _ret(proj, cosf, sinf, dec, qd, kd, gw, gb, state, g_chunk, *, nbatch, nchunk):
    chained = state is None
    rows_total = proj.shape[0]
    if chained:
        nseq, seg = 1, CHUNK_ROWS
        step_rows = RET_STEP_CHUNKS * CHUNK_ROWS
        nstep = nchunk // RET_STEP_CHUNKS
        grid = (nbatch, nstep)
        rb = lambda b, n: b * nstep + n
        sb = lambda b, n: b
        pb = lambda b, n: n
    else:
        nseq, seg = 8, 8
        step_rows = CHUNK_ROWS
        grid = (rows_total // CHUNK_ROWS,)
        rb = lambda i: i
        sb = lambda i: i
        pb = lambda i: 0
    wrap = lambda f, *tail: (lambda *g: (f(*g),) + tail)
    const = lambda *tail: (lambda *g: tail)
    q_col = (GDN_CONV_W + GDN_V_W) // RET_QK_W
    v_col = (GDN_CONV_W + GDN_V_W + 2 * RET_QK_W) // RET_V_W
    in_specs = [
        pl.BlockSpec((step_rows, RET_QK_W), wrap(rb, q_col)),
        pl.BlockSpec((step_rows, RET_QK_W), wrap(rb, q_col + 1)),
        pl.BlockSpec((step_rows, RET_V_W), wrap(rb, v_col)),
        pl.BlockSpec((step_rows, RET_V_W), wrap(rb, v_col + 1)),
        pl.BlockSpec((step_rows, RET_DK), wrap(pb, 0)),
        pl.BlockSpec((step_rows, RET_DK), wrap(pb, 0)),
        pl.BlockSpec((RET_HEADS, CHUNK_ROWS, CHUNK_ROWS), const(0, 0, 0)),
        pl.BlockSpec((RET_HEADS, CHUNK_ROWS, RET_DK), const(0, 0, 0)),
        pl.BlockSpec((RET_HEADS, CHUNK_ROWS, RET_DK), const(0, 0, 0)),
        pl.BlockSpec((1, RET_V_W), const(0, 0)),
        pl.BlockSpec((1, RET_V_W), const(0, 0)),
    ]
    args = [proj, proj, proj, proj, cosf, sinf, dec, qd, kd, gw, gb]
    if not chained:
        in_specs.append(pl.BlockSpec((nseq, RET_HEADS, RET_DK, RET_DV), wrap(sb, 0, 0, 0)))
        args.append(state)
    out_specs = [
        pl.BlockSpec((step_rows, RET_V_W), wrap(rb, 0)),
        pl.BlockSpec((nseq, RET_HEADS, RET_DK, RET_DV), wrap(sb, 0, 0, 0)),
    ]
    out_shape = [
        jax.ShapeDtypeStruct((rows_total, RET_V_W), BF16),
        jax.ShapeDtypeStruct((nbatch, RET_HEADS, RET_DK, RET_DV), F32),
    ]
    return pl.pallas_call(
        functools.partial(_ret_kernel, nseq=nseq, seg=seg, chained=chained, g_chunk=g_chunk),
        grid=grid, in_specs=in_specs, out_specs=out_specs, out_shape=out_shape,
        compiler_params=_params(len(grid)),
        name="ret_chained" if chained else "ret_step",
    )(*args)


def _out_proj_kernel(x_ref, a_ref, b_ref, wa_ref, wb_ref, o_ref):
    o_ref[...] = x_ref[...] + (_dot(a_ref[...], wa_ref[...]) + _dot(b_ref[...], wb_ref[...]))


def _out_proj(x, o_a, o_b, w_a, w_b, tm=512):
    m = x.shape[0]
    return pl.pallas_call(
        _out_proj_kernel,
        grid=(m // tm,),
        in_specs=[
            pl.BlockSpec((tm, D_MODEL), lambda i: (i, 0)),
            pl.BlockSpec((tm, GDN_V_W), lambda i: (i, 0)),
            pl.BlockSpec((tm, RET_V_W), lambda i: (i, 0)),
            pl.BlockSpec((GDN_V_W, D_MODEL), lambda i: (0, 0)),
            pl.BlockSpec((RET_V_W, D_MODEL), lambda i: (0, 0)),
        ],
        out_specs=pl.BlockSpec((tm, D_MODEL), lambda i: (i, 0)),
        out_shape=jax.ShapeDtypeStruct((m, D_MODEL), F32),
        compiler_params=_params(1),
        name="out_proj",
    )(x, o_a, o_b, w_a, w_b)


def _ffn_kernel(*refs, nseq, seg, tiles_per_seq, final_norm):
    (x_ref, nw_ref, wv_ref, wg_ref, cwv_ref, cwg_ref, bv_ref, bg_ref, wd_ref, bufv_ref, bufg_ref,
     fw_ref, o_ref, nbv_ref, nbg_ref, h_scr) = refs[:16]
    chained = tiles_per_seq > 1
    i = pl.program_id(0)
    j = pl.program_id(1)
    nf = pl.num_programs(1)
    rows = nseq * seg
    taps = FFN_CONV - 1

    @pl.when(j == 0)
    def _():
        xv = x_ref[...]
        h_scr[...] = _rms_rows(xv, nw_ref[...]).astype(BF16)
        o_ref[...] = xv

    hb = h_scr[...]

    def branch_chained(w_ref, cw_ref, b_ref, buf_ref, nb_ref, carry_ref, up_scr):
        up = _dot(hb, w_ref[...])
        up_scr[8:8 + rows, :] = up
        up_scr[8 - taps:8, :] = jnp.where(i % tiles_per_seq == 0, buf_ref[0], carry_ref[j])
        tail = up[rows - taps:, :]
        nb_ref[0] = tail
        carry_ref[j] = tail
        out = up * cw_ref[taps:taps + 1, :] + b_ref[...]
        for k in range(1, FFN_CONV):
            out = out + up_scr[8 - k:8 - k + rows, :] * cw_ref[taps - k:taps - k + 1, :]
        return out

    def branch_short(w_ref, cw_ref, b_ref, buf_ref, nb_ref):
        up = _dot(hb, w_ref[...])
        tpos = lax.broadcasted_iota(jnp.int32, (rows, 1), 0) & (seg - 1)
        buf = buf_ref[...]
        out = up * cw_ref[taps:taps + 1, :] + b_ref[...]
        for k in range(1, FFN_CONV):
            out = out + _shift_rows(up, k, tpos, buf, nseq, seg) * cw_ref[taps - k:taps - k + 1, :]
        nb_ref[...] = up.reshape(nseq, seg, up.shape[1])[:, seg - taps:, :]
        return out

    if chained:
        cv_scr, cg_scr, uv_scr, ug_scr = refs[16:20]
        val = branch_chained(wv_ref, cwv_ref, bv_ref, bufv_ref, nbv_ref, cv_scr, uv_scr)
        gate = branch_chained(wg_ref, cwg_ref, bg_ref, bufg_ref, nbg_ref, cg_scr, ug_scr)
    else:
        val = branch_short(wv_ref, cwv_ref, bv_ref, bufv_ref, nbv_ref)
        gate = branch_short(wg_ref, cwg_ref, bg_ref, bufg_ref, nbg_ref)
    act = (_silu(gate) * val).astype(BF16)
    o_ref[...] += _dot(act, wd_ref[...])

    if final_norm:
        @pl.when(j == nf - 1)
        def _():
            o_ref[...] = _rms_rows(o_ref[...], fw_ref[...])


def _ffn(x, nw, w_up, conv_w, conv_b, w_down, buf, final_w, *, layer, nseq_total, final_norm,
         x_time_major=False, out_time_major=False, tm=1024, tf=512):
    m = x.shape[0] * x.shape[1] // D_MODEL
    seq_len = m // nseq_total
    if seq_len >= tm:
        nseq, seg, tiles_per_seq = 1, tm, seq_len // tm
    else:
        nseq, seg, tiles_per_seq = tm // seq_len, seq_len, 1
    nf = D_FF // tf
    sidx = (lambda i, j: i // tiles_per_seq) if tiles_per_seq > 1 else (lambda i, j: i)
    seq_major = lambda i, j: (i, 0)
    time_major = lambda i, j: (i % tiles_per_seq, i // tiles_per_seq)
    once = pl.Buffered(1)
    in_specs = [
        pl.BlockSpec((tm, D_MODEL), time_major if x_time_major else seq_major),
        pl.BlockSpec((1, D_MODEL), lambda i, j: (0, 0)),
        pl.BlockSpec((None, D_MODEL, tf), lambda i, j: (layer, 0, j)),
        pl.BlockSpec((None, D_MODEL, tf), lambda i, j: (layer, 0, nf + j)),
        pl.BlockSpec((None, FFN_CONV, tf), lambda i, j: (layer, 0, j)),
        pl.BlockSpec((None, FFN_CONV, tf), lambda i, j: (layer, 0, nf + j)),
        pl.BlockSpec((None, 1, tf), lambda i, j: (layer, 0, j)),
        pl.BlockSpec((None, 1, tf), lambda i, j: (layer, 0, nf + j)),
        pl.BlockSpec((None, tf, D_MODEL), lambda i, j: (layer, j, 0)),
        pl.BlockSpec((None, nseq, FFN_CONV - 1, tf), lambda i, j: (layer, sidx(i, j), 0, j)),
        pl.BlockSpec((None, nseq, FFN_CONV - 1, tf), lambda i, j: (layer, sidx(i, j), 0, nf + j)),
        pl.BlockSpec((1, D_MODEL), lambda i, j: (0, 0)),
    ]
    out_specs = [
        pl.BlockSpec((tm, D_MODEL), time_major if out_time_major else seq_major, pipeline_mode=once),
        pl.BlockSpec((nseq, FFN_CONV - 1, tf), lambda i, j: (i, 0, j)),
        pl.BlockSpec((nseq, FFN_CONV - 1, tf), lambda i, j: (i, 0, j)),
    ]
    out_shape = [
        jax.ShapeDtypeStruct((seq_len, nseq_total * D_MODEL) if out_time_major else (m, D_MODEL), F32),
        jax.ShapeDtypeStruct((m // tm * nseq, FFN_CONV - 1, D_FF), F32),
        jax.ShapeDtypeStruct((m // tm * nseq, FFN_CONV - 1, D_FF), F32),
    ]
    scratch = [pltpu.VMEM((tm, D_MODEL), BF16)]
    if tiles_per_seq > 1:
        scratch += [pltpu.VMEM((nf, FFN_CONV - 1, tf), F32), pltpu.VMEM((nf, FFN_CONV - 1, tf), F32),
                    pltpu.VMEM((8 + tm, tf), F32), pltpu.VMEM((8 + tm, tf), F32)]
    y, nbv, nbg = pl.pallas_call(
        functools.partial(_ffn_kernel, nseq=nseq, seg=seg, tiles_per_seq=tiles_per_seq, final_norm=final_norm),
        grid=(m // tm, nf), in_specs=in_specs, out_specs=out_specs, out_shape=out_shape,
        scratch_shapes=scratch, compiler_params=_params(2),
        name="conv_ffn",
    )(x, nw, w_up, w_up, conv_w, conv_w, conv_b, conv_b, w_down, buf, buf, final_w)
    new_buf = jnp.concatenate([nbv, nbg], axis=-1)
    return y, new_buf[tiles_per_seq - 1::tiles_per_seq]


def _s5_disc_kernel(lre_ref, lim_ref, ldt_ref, bre_ref, bim_ref, are_ref, aim_ref, bbre_ref, bbim_ref):
    lam_re = lre_ref[...]
    lam_im = lim_ref[...]
    dt = jnp.exp(ldt_ref[...])
    mag = jnp.exp(lam_re * dt)
    ph = lam_im * dt
    ab_re = mag * jnp.cos(ph)
    ab_im = mag * jnp.sin(ph)
    den = lam_re * lam_re + lam_im * lam_im
    cf_re = ((ab_re - 1.0) * lam_re + ab_im * lam_im) / den
    cf_im = (ab_im * lam_re - (ab_re - 1.0) * lam_im) / den
    are_ref[...] = ab_re
    aim_ref[...] = ab_im
    b_re = bre_ref[...]
    b_im = bim_ref[...]
    bbre_ref[...] = cf_re * b_re - cf_im * b_im
    bbim_ref[...] = cf_re * b_im + cf_im * b_re


def _s5_disc(lam_re, lam_im, log_dt, b_re_t, b_im_t):
    g, c, p = b_re_t.shape
    tile = lambda t: jnp.tile(t, (1, c))
    shp = jax.ShapeDtypeStruct((g, c * p), F32)
    a_re, a_im, bb_re, bb_im = pl.pallas_call(
        _s5_disc_kernel, out_shape=[shp, shp, shp, shp], name="s5_disc",
    )(tile(lam_re), tile(lam_im), jnp.broadcast_to(log_dt, (g, c * p)),
      b_re_t.reshape(g, c * p), b_im_t.reshape(g, c * p))
    return a_re[:, :p], a_im[:, :p], bb_re.reshape(g, c, p), bb_im.reshape(g, c, p)


def _gelu(y):
    return y * (0.5 * (1.0 + jnp.tanh(GELU_C * (y + 0.044715 * (y * y * y)))))


def _s5_step_kernel(x_ref, nw_ref, wb_ref, wcr_ref, wci_ref, are_ref, aim_ref, d_ref, sr_ref, si_ref,
                    g_ref, hr_ref, hi_ref, u_scr, xr_scr, xi_scr, *, nseq, seg):
    rows = nseq * seg
    u = _rms_rows(x_ref[...].reshape(rows, D_MODEL), nw_ref[...])
    u_scr[...] = u
    ub = u.astype(BF16)
    lpb = S5_BW // LANES
    for kb in range(S5_NB):
        res = _dot(ub[:, kb * LANES:(kb + 1) * LANES], wb_ref[kb])
        for q in range(lpb):
            xr_scr[kb * lpb + q] = res[:, q * LANES:(q + 1) * LANES]
            xi_scr[kb * lpb + q] = res[:, S5_BW + q * LANES:S5_BW + (q + 1) * LANES]

    for c in range(S5_STATE // S5_SCAN_W):
        blocks = range(c * S5_SCAN_LB, (c + 1) * S5_SCAN_LB)
        lane = [pl.ds(lb * LANES, LANES) for lb in blocks]
        a_re = [jnp.broadcast_to(are_ref[:, sl], (nseq, LANES)) for sl in lane]
        a_im = [jnp.broadcast_to(aim_ref[:, sl], (nseq, LANES)) for sl in lane]

        def step(t, carry):
            idx = pl.ds(t, nseq, stride=seg)
            out = []
            for n, lb in enumerate(blocks):
                s_re, s_im = carry[2 * n], carry[2 * n + 1]
                n_re = a_re[n] * s_re - a_im[n] * s_im + xr_scr[lb, idx, :]
                n_im = a_re[n] * s_im + a_im[n] * s_re + xi_scr[lb, idx, :]
                xr_scr[lb, idx, :] = n_re
                xi_scr[lb, idx, :] = n_im
                out += [n_re, n_im]
            return tuple(out)

        init = []
        for sl in lane:
            init += [sr_ref[:, sl], si_ref[:, sl]]
        fin = lax.fori_loop(0, seg, step, tuple(init))
        for n, sl in enumerate(lane):
            hr_ref[:, sl] = fin[2 * n]
            hi_ref[:, sl] = fin[2 * n + 1]

    for kb in range(S5_NB):
        xr = jnp.concatenate([xr_scr[kb * lpb + q] for q in range(lpb)], axis=1).astype(BF16)
        xi = jnp.concatenate([xi_scr[kb * lpb + q] for q in range(lpb)], axis=1).astype(BF16)
        lsl = slice(kb * LANES, (kb + 1) * LANES)
        y = _dot(xr, wcr_ref[kb]) - _dot(xi, wci_ref[kb]) + d_ref[:, lsl] * u_scr[:, lsl]
        g_ref[:, lsl] = _gelu(y).astype(BF16)


def _s5_step(x3, nw, wb, wcr, wci, a_re, a_im, d, h0_re, h0_im):
    b, l, _ = x3.shape
    nseq, seg = 8, l
    rows = nseq * seg
    const2 = lambda i: (0, 0)
    const3 = lambda i: (0, 0, 0)
    smap = lambda i: (i, 0)
    return pl.pallas_call(
        functools.partial(_s5_step_kernel, nseq=nseq, seg=seg),
        grid=(b // nseq,),
        in_specs=[
            pl.BlockSpec((nseq, seg, D_MODEL), lambda i: (i, 0, 0)),
            pl.BlockSpec((1, D_MODEL), const2),
            pl.BlockSpec((S5_NB, LANES, 2 * S5_BW), const3),
            pl.BlockSpec((S5_NB, S5_BW, LANES), const3),
            pl.BlockSpec((S5_NB, S5_BW, LANES), const3),
            pl.BlockSpec((1, S5_STATE), const2),
            pl.BlockSpec((1, S5_STATE), const2),
            pl.BlockSpec((1, D_MODEL), const2),
            pl.BlockSpec((nseq, S5_STATE), smap),
            pl.BlockSpec((nseq, S5_STATE), smap),
        ],
        out_specs=[
            pl.BlockSpec((rows, D_MODEL), smap),
            pl.BlockSpec((nseq, S5_STATE), smap),
            pl.BlockSpec((nseq, S5_STATE), smap),
        ],
        out_shape=[jax.ShapeDtypeStruct((b * l, D_MODEL), BF16),
                   jax.ShapeDtypeStruct((b, S5_STATE), F32),
                   jax.ShapeDtypeStruct((b, S5_STATE), F32)],
        scratch_shapes=[pltpu.VMEM((rows, D_MODEL), F32),
                        pltpu.VMEM((S5_STATE // LANES, rows, LANES), F32),
                        pltpu.VMEM((S5_STATE // LANES, rows, LANES), F32)],
        compiler_params=_params(1),
        name="s5_step",
    )(x3, nw, wb, wcr, wci, a_re, a_im, d, h0_re, h0_im)


def _s5_seq_kernel(x_ref, nw_ref, perm_ref, back_ref, wb_ref, wc_ref, are_ref, aim_ref, d_ref, h0r_ref, h0i_ref,
                   g_ref, hr_ref, hi_ref, st_scr, u_scr, ub_scr, x_scr, *, nseq, seg):
    wrows = 2 * nseq * seg

    @pl.when(pl.program_id(0) == 0)
    def _():
        st_scr[0:nseq, :] = h0r_ref[...]
        st_scr[nseq:2 * nseq, :] = h0i_ref[...]

    x = x_ref[...]
    nw = nw_ref[...]
    u = jnp.concatenate([_rms_rows(x[:, s * D_MODEL:(s + 1) * D_MODEL], nw) for s in range(nseq)], axis=0)
    u_hi = u.astype(BF16)
    u_lo = (u - u_hi.astype(F32)).astype(BF16)
    perm = perm_ref[...]
    cw = 512
    for c in range(D_MODEL // cw):
        csl = slice(c * cw, (c + 1) * cw)
        hi = _dot(perm, u_hi[:, csl])
        u_scr[:, csl] = hi + _dot(perm, u_lo[:, csl])
        ub_scr[:, csl] = hi.astype(BF16)

    row = lax.broadcasted_iota(jnp.int32, (wrows, 1), 0)
    is_re = (row & (2 * nseq - 1)) < nseq
    lpb = S5_BW // LANES
    for kb in range(S5_NB):
        blk = ub_scr[:, kb * LANES:(kb + 1) * LANES]
        zero = jnp.zeros_like(blk)
        lhs = jnp.concatenate([jnp.where(is_re, blk, zero), jnp.where(is_re, zero, blk)], axis=1)
        res = _dot(lhs, wb_ref[kb])
        for q in range(lpb):
            x_scr[kb * lpb + q] = res[:, q * LANES:(q + 1) * LANES]

    sign = jnp.where(lax.broadcasted_iota(jnp.int32, (2 * nseq, 1), 0) < nseq, -1.0, 1.0)
    for c in range(S5_STATE // S5_SCAN_W):
        blocks = range(c * S5_SCAN_LB, (c + 1) * S5_SCAN_LB)
        lane = [pl.ds(lb * LANES, LANES) for lb in blocks]
        a1 = [jnp.broadcast_to(are_ref[:, sl], (2 * nseq, LANES)) for sl in lane]
        a2 = [jnp.broadcast_to(aim_ref[:, sl], (2 * nseq, LANES)) * sign for sl in lane]

        def step(t, carry):
            rsl = pl.ds(pl.multiple_of(t * (2 * nseq), 2 * nseq), 2 * nseq)
            out = []
            for n, lb in enumerate(blocks):
                st = carry[n]
                nxt = a1[n] * st + a2[n] * pltpu.roll(st, nseq, 0) + x_scr[lb, rsl, :]
                x_scr[lb, rsl, :] = nxt
                out.append(nxt)
            return tuple(out)

        fin = lax.fori_loop(0, seg, step, tuple(st_scr[:, sl] for sl in lane))
        for n, sl in enumerate(lane):
            st_scr[:, sl] = fin[n]
    hr_ref[...] = st_scr[0:nseq, :]
    hi_ref[...] = st_scr[nseq:2 * nseq, :]

    for kb in range(S5_NB):
        xb = jnp.concatenate([x_scr[kb * lpb + q] for q in range(lpb)], axis=1).astype(BF16)
        y2 = _dot(xb, wc_ref[kb])
        lsl = slice(kb * LANES, (kb + 1) * LANES)
        y = y2[:, :LANES] - pltpu.roll(y2[:, LANES:], wrows - nseq, 0) + d_ref[:, lsl] * u_scr[:, lsl]
        ub_scr[:, lsl] = _gelu(y).astype(BF16)

    back = back_ref[...]
    for c in range(D_MODEL // cw):
        csl = slice(c * cw, (c + 1) * cw)
        g_nat = _dot(back, ub_scr[:, csl])
        for s in range(nseq):
            g_ref[s, :, csl] = g_nat[s * seg:(s + 1) * seg].astype(BF16)


def _s5_seq(x_tm, nw, perm, back, wb2, wc2, a_re, a_im, d, h0_re, h0_im, *, nseq, seg):
    l = x_tm.shape[0]
    wrows = 2 * nseq * seg
    const2 = lambda n: (0, 0)
    const3 = lambda n: (0, 0, 0)
    return pl.pallas_call(
        functools.partial(_s5_seq_kernel, nseq=nseq, seg=seg),
        grid=(l // seg,),
        in_specs=[
            pl.BlockSpec((seg, nseq * D_MODEL), lambda n: (n, 0)),
            pl.BlockSpec((1, D_MODEL), const2),
            pl.BlockSpec((wrows, nseq * seg), const2),
            pl.BlockSpec((nseq * seg, wrows), const2),
            pl.BlockSpec((S5_NB, 2 * LANES, S5_BW), const3),
            pl.BlockSpec((S5_NB, S5_BW, 2 * LANES), const3),
            pl.BlockSpec((1, S5_STATE), const2),
            pl.BlockSpec((1, S5_STATE), const2),
            pl.BlockSpec((1, D_MODEL), const2),
            pl.BlockSpec((nseq, S5_STATE), const2),
            pl.BlockSpec((nseq, S5_STATE), const2),
        ],
        out_specs=[
            pl.BlockSpec((nseq, seg, D_MODEL), lambda n: (0, n, 0)),
            pl.BlockSpec((nseq, S5_STATE), const2),
            pl.BlockSpec((nseq, S5_STATE), const2),
        ],
        out_shape=[jax.ShapeDtypeStruct((nseq, l, D_MODEL), BF16),
                   jax.ShapeDtypeStruct((nseq, S5_STATE), F32),
                   jax.ShapeDtypeStruct((nseq, S5_STATE), F32)],
        scratch_shapes=[pltpu.VMEM((2 * nseq, S5_STATE), F32),
                        pltpu.VMEM((wrows, D_MODEL), F32),
                        pltpu.VMEM((wrows, D_MODEL), BF16),
                        pltpu.VMEM((S5_STATE // LANES, wrows, LANES), F32)],
        compiler_params=_params(1),
        name="s5_seq",
    )(x_tm, nw, perm, back, wb2, wc2, a_re, a_im, d, h0_re, h0_im)


def _s5_row_perm(nseq, seg):
    r = np.arange(2 * nseq * seg)
    src = (r % nseq) * seg + r // (2 * nseq)
    hit = src[:, None] == np.arange(nseq * seg)[None, :]
    real = (r % (2 * nseq)) < nseq
    return jnp.asarray(hit, BF16), jnp.asarray((hit & real[:, None]).T, BF16)


def _glu_kernel(x_ref, g_ref, wv_ref, wg_ref, o_ref):
    gb = g_ref[...]
    o_ref[...] = x_ref[...] + _dot(gb, wv_ref[...]) * _sigmoid(_dot(gb, wg_ref[...]))


def _glu(x, g, w_glu, *, time_major_seqs=0, tm=1024, tn=512):
    nn = D_MODEL // tn
    if time_major_seqs:
        tps = x.shape[0] // tm
        n_tiles = tps * time_major_seqs
        xmap = lambda i, j: (i % tps, (i // tps) * nn + j)
    else:
        n_tiles = x.shape[0] // tm
        xmap = lambda i, j: (i, j)
    return pl.pallas_call(
        _glu_kernel,
        grid=(n_tiles, nn),
        in_specs=[
            pl.BlockSpec((tm, tn), xmap),
            pl.BlockSpec((tm, D_MODEL), lambda i, j: (i, 0)),
            pl.BlockSpec((D_MODEL, tn), lambda i, j: (0, j)),
            pl.BlockSpec((D_MODEL, tn), lambda i, j: (0, nn + j)),
        ],
        out_specs=pl.BlockSpec((tm, tn), xmap),
        out_shape=jax.ShapeDtypeStruct(x.shape, F32),
        compiler_params=_params(2),
        name="glu",
    )(x, g, w_glu, w_glu)


def _rotary_tables(pos):
    half = RET_DK // 2
    inv = ROPE_BASE ** (-jnp.arange(half, dtype=F32) / half)
    ang = pos[:, None] * inv[None, :]
    cos = jnp.cos(ang)
    sin = jnp.sin(ang)
    return jnp.concatenate([cos, cos], axis=-1), jnp.concatenate([-sin, sin], axis=-1)


def _retention_tables(nseq, seg):
    log_g = np.log1p(-np.exp2(-5.0 - np.arange(RET_HEADS)))
    row = np.arange(nseq * seg)
    t = (row % seg).astype(np.float64)
    sid = row // seg
    diff = t[:, None] - t[None, :]
    ok = (sid[:, None] == sid[None, :]) & (diff >= 0)
    dec = np.where(ok, np.exp(log_g[:, None, None] * np.where(ok, diff, 0.0)), 0.0)
    qd = np.broadcast_to(np.exp(log_g[:, None] * (t + 1.0))[..., None], (RET_HEADS, nseq * seg, RET_DK))
    kd = np.broadcast_to(np.exp(log_g[:, None] * (seg - 1.0 - t))[..., None], (RET_HEADS, nseq * seg, RET_DK))
    gains = tuple(float(v) for v in np.exp(log_g * seg))
    return jnp.asarray(dec, F32), jnp.asarray(qd, F32), jnp.asarray(kd, F32), gains


def _block_diag(blocks, rows_per, cols_per):
    tiled = jnp.tile(blocks.reshape(S5_NB, S5_GB * rows_per, cols_per), (1, 1, S5_GB))
    rg = np.arange(S5_GB * rows_per)[:, None] // rows_per
    cg = np.arange(S5_GB * cols_per)[None, :] // cols_per
    return (tiled * jnp.asarray(rg == cg, F32)).astype(BF16)


def _trunk(x3, pos, gdn_s, gdn_cb, ret_s, s5_re, s5_im, ffn_cb, prm, *, chained):
    b, l, _ = x3.shape
    m = b * l
    x = x3.reshape(m, D_MODEL)
    nchunk = l // CHUNK_ROWS if chained else 1

    proj, gate = _in_proj(x, prm['norm_mix_w'][0:1], prm['w_in_main'], prm['w_in_gate'])
    o_a, gdn_new, gcb_new = _gdn(proj, gate, gdn_cb, prm['gdn_conv_w'], prm['alog_row'], prm['dtb_row'],
                                 prm['gdn_norm_w'], None if chained else gdn_s, nbatch=b, nchunk=nchunk)
    if chained:
        cosf, sinf = _rotary_tables(pos)
        dec, qd, kd, g_chunk = _retention_tables(1, CHUNK_ROWS)
    else:
        cosf, sinf = _rotary_tables(jnp.tile(pos, CHUNK_ROWS // l))
        dec, qd, kd, g_chunk = _retention_tables(CHUNK_ROWS // l, l)
    o_b, ret_new = _ret(proj, cosf, sinf, dec, qd, kd, prm['ret_gn_w'], prm['ret_gn_b'],
                        None if chained else ret_s, g_chunk, nbatch=b, nchunk=nchunk)
    x = _out_proj(x, o_a, o_b, prm['w_out_a'], prm['w_out_b'])
    x, fcb0 = _ffn(x, prm['norm_ffn_w'][0:1], prm['w_up'], prm['ffn_conv_w'], prm['ffn_conv_b'],
                   prm['w_down'], ffn_cb, prm['norm_final_w'], layer=0, nseq_total=b, final_norm=False,
                   out_time_major=chained)

    if chained:
        g, s5r_new, s5i_new = _s5_seq(x, prm['norm_mix_w'][1:2], *_s5_row_perm(b, CHUNK_ROWS), prm['s5_wb2'],
                                      prm['s5_wc2'], prm['s5_a_re'], prm['s5_a_im'], prm['s5_d'],
                                      s5_re, s5_im, nseq=b, seg=CHUNK_ROWS)
        x = _glu(x, g.reshape(m, D_MODEL), prm['w_glu'], time_major_seqs=b)
    else:
        g, s5r_new, s5i_new = _s5_step(x.reshape(b, l, D_MODEL), prm['norm_mix_w'][1:2], prm['s5_wb'],
                                       prm['s5_wc_re'], prm['s5_wc_im'], prm['s5_a_re'], prm['s5_a_im'],
                                       prm['s5_d'], s5_re, s5_im)
        x = _glu(x, g, prm['w_glu'])
    y, fcb1 = _ffn(x, prm['norm_ffn_w'][1:2], prm['w_up'], prm['ffn_conv_w'], prm['ffn_conv_b'],
                   prm['w_down'], ffn_cb, prm['norm_final_w'], layer=1, nseq_total=b, final_norm=True,
                   x_time_major=chained)
    return (y.reshape(b, l, D_MODEL), gdn_new[None], gcb_new[None], ret_new[None],
            s5r_new.reshape(1, b, S5_GROUPS, S5_P), s5i_new.reshape(1, b, S5_GROUPS, S5_P),
            jnp.stack([fcb0, fcb1]))


def kernel(x_prompt, x_sample, state_gdn, state_gdn_conv, state_ret, state_s5_re, state_s5_im,
           state_ffn_conv, norm_mix_w, norm_ffn_w, norm_final_w,
           w_in, gdn_conv_w, gdn_a_log, gdn_dt_bias, gdn_norm_w, ret_gn_w, ret_gn_b, w_out,
           s5_lam_re, s5_lam_im, s5_log_dt, s5_b_re, s5_b_im, s5_c_re, s5_c_im, s5_d, w_glu,
           w_up, ffn_conv_w, ffn_conv_b, w_down):
    bp, lp, _ = x_prompt.shape
    bs, ls, _ = x_sample.shape
    past_len = 16384

    w = w_in[0]
    o_z = GDN_CONV_W
    o_a = o_z + GDN_V_W
    o_b = o_a + GDN_HEADS
    o_rest = o_b + GDN_HEADS
    w_in_main = jnp.concatenate([w[:, :o_a], w[:, o_rest:]], axis=1).astype(BF16)
    w_in_gate = jnp.pad(w[:, o_a:o_rest], ((0, 0), (0, GATE_W - 2 * GDN_HEADS))).astype(BF16)
    pad_row = lambda v: jnp.pad(v.reshape(1, -1), ((0, 0), (0, GATE_W - v.shape[-1])))

    a_re, a_im, bb_re, bb_im = _s5_disc(s5_lam_re[0], s5_lam_im[0], s5_log_dt[0].reshape(S5_GROUPS, 1),
                                        s5_b_re[0].transpose(0, 2, 1), s5_b_im[0].transpose(0, 2, 1))
    wb_re, wb_im = _block_diag(bb_re, S5_GROUP, S5_P), _block_diag(bb_im, S5_GROUP, S5_P)
    s5_wb = jnp.concatenate([wb_re, wb_im], axis=-1)
    s5_wb2 = jnp.concatenate([wb_re, wb_im], axis=1)
    s5_wc_re = _block_diag(s5_c_re[0].transpose(0, 2, 1), S5_P, S5_GROUP)
    s5_wc_im = _block_diag(s5_c_im[0].transpose(0, 2, 1), S5_P, S5_GROUP)
    s5_wc2 = jnp.concatenate([s5_wc_re, s5_wc_im], axis=-1)

    prm = dict(
        norm_mix_w=norm_mix_w, norm_ffn_w=norm_ffn_w, norm_final_w=norm_final_w.reshape(1, D_MODEL),
        w_in_main=w_in_main, w_in_gate=w_in_gate,
        gdn_conv_w=gdn_conv_w[0], alog_row=pad_row(gdn_a_log[0]), dtb_row=pad_row(gdn_dt_bias[0]),
        gdn_norm_w=gdn_norm_w[0].reshape(1, GDN_DV),
        ret_gn_w=ret_gn_w[0].reshape(1, RET_V_W), ret_gn_b=ret_gn_b[0].reshape(1, RET_V_W),
        w_out_a=w_out[0, :GDN_V_W].astype(BF16), w_out_b=w_out[0, GDN_V_W:].astype(BF16),
        s5_wb=s5_wb, s5_wc_re=s5_wc_re, s5_wc_im=s5_wc_im, s5_wb2=s5_wb2, s5_wc2=s5_wc2,
        s5_a_re=a_re.reshape(1, S5_STATE), s5_a_im=a_im.reshape(1, S5_STATE), s5_d=s5_d[0].reshape(1, D_MODEL),
        w_glu=w_glu[0].astype(BF16),
        w_up=w_up.astype(BF16), ffn_conv_w=ffn_conv_w, ffn_conv_b=ffn_conv_b[:, None, :],
        w_down=w_down.astype(BF16),
    )

    z_gcb = jnp.zeros((bp, GDN_CONV - 1, GDN_CONV_W), F32)
    z_s5 = jnp.zeros((bp, S5_STATE), F32)
    z_fcb = jnp.zeros((2, bp, FFN_CONV - 1, 2 * D_FF), F32)
    pos_p = jnp.arange(lp, dtype=F32)
    pos_s = past_len + jnp.arange(ls, dtype=F32)

    y_p, gdn_p, gcb_p, ret_p, s5r_p, s5i_p, fcb_p = _trunk(
        x_prompt, pos_p, None, z_gcb, None, z_s5, z_s5, z_fcb, prm, chained=True)
    y_s, gdn_s, gcb_s, ret_s, s5r_s, s5i_s, fcb_s = _trunk(
        x_sample, pos_s, state_gdn[0], state_gdn_conv[0], state_ret[0],
        state_s5_re[0].reshape(bs, S5_STATE), state_s5_im[0].reshape(bs, S5_STATE),
        state_ffn_conv, prm, chained=False)
    return (y_p, y_s, gdn_p, gdn_s, gcb_p, gcb_s, ret_p, ret_s,
            s5r_p, s5r_s, s5i_p, s5i_s, fcb_p, fcb_s)
```

```python
import functools
import math

import jax
import jax.numpy as jnp
import numpy as np
from jax import lax
from jax.experimental import pallas as pl
from jax.experimental.pallas import tpu as pltpu

F32 = jnp.float32
BF16 = jnp.bfloat16

D_MODEL = 2048
GDN_HEADS = 8
GDN_DK = 128
GDN_DV = 128
GDN_CONV = 4
RET_HEADS = 4
RET_DK = 128
RET_DV = 256
ROPE_BASE = 10000.0
S5_GROUP = 16
S5_GROUPS = 128
S5_P = 64
S5_STATE = S5_GROUPS * S5_P
D_FF = 5632
FFN_CONV = 3
EPS = 1e-6

GDN_QK_W = GDN_HEADS * GDN_DK
GDN_V_W = GDN_HEADS * GDN_DV
GDN_CONV_W = 2 * GDN_QK_W + GDN_V_W
RET_QK_W = RET_HEADS * RET_DK
RET_V_W = RET_HEADS * RET_DV
PROJ_W = GDN_CONV_W + GDN_V_W + 2 * RET_QK_W + 2 * RET_V_W
GATE_W = 128

CHUNK_ROWS = 64
GDN_GROUP = 4
GDN_STEP_CHUNKS = 2
GDN_DIAG = 16
RET_STEP_CHUNKS = 4
LANES = 128
VMEM_LIMIT = 56 * 1024 * 1024

S5_GB = 8
S5_NB = S5_GROUPS // S5_GB
S5_BW = S5_GB * S5_P
S5_SCAN_W = 1024
S5_SCAN_LB = S5_SCAN_W // LANES
GELU_C = math.sqrt(2.0 / math.pi)


def _params(n_grid):
    return pltpu.CompilerParams(dimension_semantics=("arbitrary",) * n_grid,
                                vmem_limit_bytes=VMEM_LIMIT)


def _dot(a, b):
    return jnp.dot(a, b, preferred_element_type=F32)


def _dot_nt(a, b):
    return lax.dot_general(a, b, (((1,), (1,)), ((), ())), preferred_element_type=F32)


def _dot_tn(a, b):
    return lax.dot_general(a, b, (((0,), (0,)), ((), ())), preferred_element_type=F32)


def _sigmoid(x):
    return 0.5 + 0.5 * jnp.tanh(0.5 * x)


def _silu(x):
    h = 0.5 * x
    return h + h * jnp.tanh(h)


def _rms_rows(x, w):
    return x * lax.rsqrt(jnp.mean(x * x, axis=-1, keepdims=True) + EPS) * w


def _shift_rows(x, k, tpos, buf, nseq, seg):
    rows, width = x.shape
    nb = buf.shape[1]
    prev = pltpu.roll(x, k, 0)
    for t in range(k):
        src = buf[:, nb - k + t:nb - k + t + 1, :]
        srcb = jnp.broadcast_to(src, (nseq, seg, width)).reshape(rows, width)
        prev = jnp.where(tpos == t, srcb, prev)
    return prev


def _w_in_prep_kernel(w_ref, main_ref, gate_ref):
    o_gate = GDN_CONV_W + GDN_V_W
    o_rest = o_gate + 2 * GDN_HEADS
    main_ref[:, :o_gate] = w_ref[:, :o_gate].astype(BF16)
    main_ref[:, o_gate:] = w_ref[:, o_rest:].astype(BF16)
    lane = lax.broadcasted_iota(jnp.int32, (1, GATE_W), 1)
    gate_ref[...] = jnp.where(lane < 2 * GDN_HEADS, w_ref[:, o_gate:o_gate + GATE_W], 0.0).astype(BF16)


def _w_in_prep(w, tr=256):
    k, n = w.shape
    return pl.pallas_call(
        _w_in_prep_kernel,
        grid=(k // tr,),
        in_specs=[pl.BlockSpec((tr, n), lambda i: (i, 0))],
        out_specs=[pl.BlockSpec((tr, PROJ_W), lambda i: (i, 0)),
                   pl.BlockSpec((tr, GATE_W), lambda i: (i, 0))],
        out_shape=[jax.ShapeDtypeStruct((k, PROJ_W), BF16), jax.ShapeDtypeStruct((k, GATE_W), BF16)],
        compiler_params=_params(1),
        name="w_in_prep",
    )(w)


def _in_proj_kernel(x_ref, nw_ref, w_ref, wg_ref, o_ref, og_ref, h_scr):
    @pl.when(pl.program_id(1) == 0)
    def _():
        hb = _rms_rows(x_ref[...], nw_ref[...]).astype(BF16)
        h_scr[...] = hb
        og_ref[...] = _dot(hb, wg_ref[...])

    o_ref[...] = _dot(h_scr[...], w_ref[...])


def _in_proj(x, nw, w_main, w_gate, tm=1024, tn=1024):
    m = x.shape[0]
    return pl.pallas_call(
        _in_proj_kernel,
        grid=(m // tm, PROJ_W // tn),
        in_specs=[
            pl.BlockSpec((tm, D_MODEL), lambda i, j: (i, 0), pipeline_mode=pl.Buffered(1)),
            pl.BlockSpec((1, D_MODEL), lambda i, j: (0, 0)),
            pl.BlockSpec((D_MODEL, tn), lambda i, j: (0, j)),
            pl.BlockSpec((D_MODEL, GATE_W), lambda i, j: (0, 0)),
        ],
        out_specs=[
            pl.BlockSpec((tm, tn), lambda i, j: (i, j)),
            pl.BlockSpec((tm, GATE_W), lambda i, j: (i, 0)),
        ],
        out_shape=[jax.ShapeDtypeStruct((m, PROJ_W), F32),
                   jax.ShapeDtypeStruct((m, GATE_W), F32)],
        scratch_shapes=[pltpu.VMEM((tm, D_MODEL), BF16)],
        compiler_params=_params(2),
        name="in_proj",
    )(x, nw, w_main, w_gate)


def _gdn_kernel(*refs, nseq, seg, chained):
    if chained:
        (qkv_ref, z_ref, gate_ref, cbuf_ref, cw_ref, alog_ref, dtb_ref, nw_ref, sum_ref, mask_ref,
         o_ref, s_out_ref, cb_out_ref, carry_scr) = refs
        s_in_ref = s_out_ref
        first = pl.program_id(1) == 0

        @pl.when(first)
        def _():
            s_out_ref[...] = jnp.zeros_like(s_out_ref)
    else:
        (qkv_ref, z_ref, gate_ref, cbuf_ref, cw_ref, alog_ref, dtb_ref, nw_ref, sum_ref, mask_ref, s_in_ref,
         o_ref, s_out_ref, cb_out_ref) = refs

    rows = nseq * seg
    step_rows = qkv_ref.shape[0]
    cps = step_rows // rows
    shift = int(math.log2(seg))
    x = qkv_ref[...]
    if chained:
        buf = jnp.where(first, cbuf_ref[...], carry_scr[5:8, :].reshape(1, GDN_CONV - 1, GDN_CONV_W))
        conv_seg = step_rows
    else:
        buf = cbuf_ref[...]
        conv_seg = seg
    row_id = lax.broadcasted_iota(jnp.int32, (step_rows, 1), 0)
    tpos = row_id & (conv_seg - 1)

    acc = x * cw_ref[GDN_CONV - 1:GDN_CONV, :]
    for k in range(1, GDN_CONV):
        acc = acc + _shift_rows(x, k, tpos, buf, nseq, conv_seg) * cw_ref[GDN_CONV - 1 - k:GDN_CONV - k, :]
    act = _silu(acc)
    cb_out_ref[...] = x.reshape(nseq, conv_seg, GDN_CONV_W)[:, conv_seg - (GDN_CONV - 1):, :]
    if chained:
        carry_scr[...] = x[step_rows - 8:, :]

    gate = gate_ref[...]
    xa = gate + dtb_ref[...]
    softplus = jnp.maximum(xa, 0.0) + jnp.log(1.0 + jnp.exp(-jnp.abs(xa)))
    log_a = -jnp.exp(alog_ref[...]) * softplus
    beta_all = _sigmoid(gate)

    sum_m = sum_ref[...]
    a1 = log_a.astype(BF16)
    r1 = log_a - a1.astype(F32)
    a2 = r1.astype(BF16)
    a3 = (r1 - a2.astype(F32)).astype(BF16)
    if nseq > 1:
        row_masks = [((row_id >> shift) == s).astype(F32) for s in range(nseq)]
    nw = nw_ref[...]
    chunk_sl = [slice(c * rows, (c + 1) * rows) for c in range(cps)]
    g_all = [_dot(sum_m, a1[sl]) + (_dot(sum_m, a2[sl]) + _dot(sum_m, a3[sl])) for sl in chunk_sl]
    g = [t[:rows] for t in g_all]
    g_last = [t[rows:] for t in g_all]
    g2_t = [jnp.concatenate([t, t], axis=0).T for t in g]
    e_g = [jnp.exp(t) for t in g]
    e_rest = [jnp.exp(tl - t) for tl, t in zip(g_last, g)]
    e_last = [jnp.exp(t) for t in g_last]

    incl_f = mask_ref[0]
    strict_f = mask_ref[1]
    lane = lax.broadcasted_iota(jnp.int32, (1, LANES), 1)

    def stack_cols(tile, heads, off=0):
        return jnp.concatenate([tile[:, off + h:off + h + 1] for h in heads], axis=0)

    groups = [list(range(g0, g0 + GDN_GROUP)) for g0 in range(0, GDN_HEADS, GDN_GROUP)]
    units = [(c, heads) for c in range(cps) for heads in groups]
    ks, q_all, k_all, v_all = [], [], [], []
    for c, heads in units:
        a_c = act[chunk_sl[c]]
        qs, kn, vs = [], [], []
        for h in heads:
            lo = h * GDN_DK
            q = a_c[:, lo:lo + GDN_DK]
            k = a_c[:, GDN_QK_W + lo:GDN_QK_W + lo + GDN_DK]
            qs.append(q * lax.rsqrt(jnp.sum(q * q, axis=-1, keepdims=True) + EPS) * (GDN_DK ** -0.5))
            kn.append(k * lax.rsqrt(jnp.sum(k * k, axis=-1, keepdims=True) + EPS))
            vs.append(a_c[:, 2 * GDN_QK_W + lo:2 * GDN_QK_W + lo + GDN_DV])
        ks.append(kn)
        q_all.append(jnp.concatenate(qs, axis=0))
        k_all.append(jnp.concatenate(kn, axis=0))
        v_all.append(jnp.concatenate(vs, axis=0))
    gc = [stack_cols(g[c], heads) for c, heads in units]
    bc = [stack_cols(beta_all[chunk_sl[c]], heads, GDN_HEADS) for c, heads in units]
    egc = [stack_cols(e_g[c], heads) for c, heads in units]
    gr = [jnp.concatenate(
        [jnp.where(lane < rows, g2_t[c][heads[j]:heads[j] + 1, :], g2_t[c][heads[j + 1]:heads[j + 1] + 1, :])
         for j in range(0, GDN_GROUP, 2)], axis=1) for c, heads in units]
    decay = [jnp.exp(jnp.minimum(a - r, 0.0)) * incl_f for a, r in zip(gc, gr)]
    kb = [k.astype(BF16) for k in k_all]
    kk = [_dot_nt(k, k) for k in kb]
    qk = [_dot_nt(q.astype(BF16), k) * d for q, k, d in zip(q_all, kb, decay)]
    lms = [b * m * d * strict_f for b, m, d in zip(bc, kk, decay)]
    inv_e = _unit_lower_inverse_minus_eye(lms, mask_ref, seg)
    rhs = [jnp.concatenate([v * b, k * (b * e)], axis=-1) for v, k, b, e in zip(v_all, k_all, bc, egc)]
    sol = [r + _dot(e.astype(BF16), r.astype(BF16)) for e, r in zip(inv_e, rhs)]
    qd_all = [q * e for q, e in zip(q_all, egc)]

    states = [[s_in_ref[s, h] for s in range(nseq)] for h in range(GDN_HEADS)]
    for c in range(cps):
        unit_ids = [u for u, (uc, _) in enumerate(units) if uc == c]
        v_new, o_all = {}, {}
        for u in unit_ids:
            ws_parts, qs_parts = [], []
            for j, h in enumerate(units[u][1]):
                r0 = j * rows
                lhs = jnp.concatenate([sol[u][r0:r0 + rows, GDN_DV:], qd_all[u][r0:r0 + rows]], axis=0).astype(BF16)
                ws = None
                for s in range(nseq):
                    part = _dot(lhs, states[h][s].astype(BF16))
                    if nseq > 1:
                        part = part * jnp.concatenate([row_masks[s], row_masks[s]], axis=0)
                    ws = part if ws is None else ws + part
                ws_parts.append(ws[:rows])
                qs_parts.append(ws[rows:])
            v_new[u] = sol[u][:, :GDN_DV] - jnp.concatenate(ws_parts, axis=0)
            o_all[u] = jnp.concatenate(qs_parts, axis=0)
        for u in unit_ids:
            o_all[u] = o_all[u] + _dot(qk[u].astype(BF16), v_new[u].astype(BF16))
        for u in unit_ids:
            for j, h in enumerate(units[u][1]):
                r0 = j * rows
                lo = h * GDN_DK
                k_rest = (ks[u][j] * e_rest[c][:, h:h + 1]).astype(BF16)
                vn = v_new[u][r0:r0 + rows]
                for s in range(nseq):
                    vsel = vn * row_masks[s] if nseq > 1 else vn
                    states[h][s] = (states[h][s] * e_last[c][s * seg:s * seg + 1, h:h + 1]
                                    + _dot_tn(k_rest, vsel.astype(BF16)))
                o = o_all[u][r0:r0 + rows]
                o = o * lax.rsqrt(jnp.mean(o * o, axis=-1, keepdims=True) + EPS) * nw
                o = o * _silu(z_ref[chunk_sl[c], lo:lo + GDN_DV])
                o_ref[chunk_sl[c], lo:lo + GDN_DV] = o.astype(BF16)
    for h in range(GDN_HEADS):
        for s in range(nseq):
            s_out_ref[s, h] = states[h][s]


def _gdn_masks(nseq, seg):
    n = GDN_GROUP * nseq * seg
    ri, ci = np.arange(n)[:, None], np.arange(n)[None, :]
    same = (ri // seg) == (ci // seg)
    blk = min(GDN_DIAG, seg)
    masks = [same & (ci <= ri), same & (ci < ri), -1.0 * ((ri // blk) == (ci // blk))]
    size = blk
    while size < seg:
        masks.append(((ri // (2 * size)) == (ci // (2 * size))) & ((ri // size) != (ci // size)))
        size *= 2
    return jnp.asarray(np.stack([np.asarray(m, np.float32) for m in masks]), F32)


def _gdn_sum_matrix(nseq, seg):
    n = nseq * seg
    ri, ci = np.arange(n)[:, None], np.arange(n)[None, :]
    same = (ri // seg) == (ci // seg)
    return jnp.asarray(np.concatenate([same & (ci <= ri), same]), BF16)


def _unit_lower_inverse_minus_eye(lms, mask_ref, seg):
    blk = min(GDN_DIAG, seg)
    neg_diag = mask_ref[2]
    err = [lm * neg_diag for lm in lms]
    m_b = [e.astype(BF16) for e in err]
    for _ in range(int(math.log2(blk)) - 1):
        m_f = [_dot(m, m) for m in m_b]
        m_b = [m.astype(BF16) for m in m_f]
        err = [e + m + _dot(e.astype(BF16), mb) for e, m, mb in zip(err, m_f, m_b)]
    level = 3
    size = blk
    while size < seg:
        c = [lm * mask_ref[level] for lm in lms]
        e_b = [e.astype(BF16) for e in err]
        y = [x + _dot(e, x.astype(BF16)) for x, e in zip(c, e_b)]
        err = [e - (v + _dot(v.astype(BF16), eb)) for e, v, eb in zip(err, y, e_b)]
        level += 1
        size *= 2
    return err


def _gdn(proj, gate, cbuf, conv_w, alog_row, dtb_row, norm_w, state, *, nbatch, nchunk):
    chained = state is None
    rows_total = proj.shape[0]
    if chained:
        nseq, seg = 1, CHUNK_ROWS
        step_rows = GDN_STEP_CHUNKS * CHUNK_ROWS
        nstep = nchunk // GDN_STEP_CHUNKS
        grid = (nbatch, nstep)
        rb = lambda b, n: b * nstep + n
        sb = lambda b, n: b
    else:
        nseq, seg = 8, 8
        step_rows = CHUNK_ROWS
        grid = (rows_total // CHUNK_ROWS,)
        rb = lambda i: i
        sb = lambda i: i
    n_state = nbatch
    sum_m = _gdn_sum_matrix(nseq, seg)
    masks = _gdn_masks(nseq, seg)
    wrap = lambda f, *tail: (lambda *g: (f(*g),) + tail)
    const = lambda *tail: (lambda *g: tail)
    in_specs = [
        pl.BlockSpec((step_rows, GDN_CONV_W), wrap(rb, 0)),
        pl.BlockSpec((step_rows, GDN_V_W), wrap(rb, GDN_CONV_W // GDN_V_W)),
        pl.BlockSpec((step_rows, GATE_W), wrap(rb, 0)),
        pl.BlockSpec((nseq, GDN_CONV - 1, GDN_CONV_W), wrap(sb, 0, 0)),
        pl.BlockSpec((GDN_CONV, GDN_CONV_W), const(0, 0)),
        pl.BlockSpec((1, GATE_W), const(0, 0)),
        pl.BlockSpec((1, GATE_W), const(0, 0)),
        pl.BlockSpec((1, GDN_DV), const(0, 0)),
        pl.BlockSpec(sum_m.shape, const(0, 0)),
        pl.BlockSpec(masks.shape, const(0, 0, 0)),
    ]
    args = [proj, proj, gate, cbuf, conv_w, alog_row, dtb_row, norm_w, sum_m, masks]
    scratch = []
    if chained:
        scratch = [pltpu.VMEM((8, GDN_CONV_W), F32)]
    else:
        in_specs.append(pl.BlockSpec((nseq, GDN_HEADS, GDN_DK, GDN_DV), wrap(sb, 0, 0, 0)))
        args.append(state)
    out_specs = [
        pl.BlockSpec((step_rows, GDN_V_W), wrap(rb, 0)),
        pl.BlockSpec((nseq, GDN_HEADS, GDN_DK, GDN_DV), wrap(sb, 0, 0, 0)),
        pl.BlockSpec((nseq, GDN_CONV - 1, GDN_CONV_W), wrap(sb, 0, 0)),
    ]
    out_shape = [
        jax.ShapeDtypeStruct((rows_total, GDN_V_W), BF16),
        jax.ShapeDtypeStruct((n_state, GDN_HEADS, GDN_DK, GDN_DV), F32),
        jax.ShapeDtypeStruct((n_state, GDN_CONV - 1, GDN_CONV_W), F32),
    ]
    return pl.pallas_call(
        functools.partial(_gdn_kernel, nseq=nseq, seg=seg, chained=chained),
        grid=grid, in_specs=in_specs, out_specs=out_specs, out_shape=out_shape,
        scratch_shapes=scratch, compiler_params=_params(len(grid)),
        name="gdn_chained" if chained else "gdn_step",
    )(*args)


def _ret_kernel(*refs, nseq, seg, chained, g_chunk):
    if chained:
        (q_ref, k_ref, v_ref, gt_ref, cos_ref, sin_ref, dec_ref, qd_ref, kd_ref, gw_ref, gb_ref,
         o_ref, s_out_ref) = refs
        s_in_ref = s_out_ref

        @pl.when(pl.program_id(1) == 0)
        def _():
            s_out_ref[...] = jnp.zeros_like(s_out_ref)
    else:
        (q_ref, k_ref, v_ref, gt_ref, cos_ref, sin_ref, dec_ref, qd_ref, kd_ref, gw_ref, gb_ref,
         s_in_ref, o_ref, s_out_ref) = refs

    rows = nseq * seg
    cps = q_ref.shape[0] // rows
    shift = int(math.log2(seg))
    if nseq > 1:
        row_id = lax.broadcasted_iota(jnp.int32, (rows, 1), 0)
        row_masks = [((row_id >> shift) == s).astype(F32) for s in range(nseq)]

    units = [(c, h) for c in range(cps) for h in range(RET_HEADS)]
    rsl = lambda c: slice(c * rows, (c + 1) * rows)
    qr, kr = [], []
    for c, h in units:
        cosf = cos_ref[rsl(c), :]
        sinf = sin_ref[rsl(c), :]
        q = q_ref[rsl(c), h * RET_DK:(h + 1) * RET_DK]
        k = k_ref[rsl(c), h * RET_DK:(h + 1) * RET_DK]
        qr.append(q * cosf + pltpu.roll(q, RET_DK // 2, 1) * sinf)
        kr.append((k * cosf + pltpu.roll(k, RET_DK // 2, 1) * sinf) * (RET_DK ** -0.5))
    vf = [v_ref[rsl(c), h * RET_DV:(h + 1) * RET_DV] for c, h in units]
    vb = [v.astype(BF16) for v in vf]
    kb = [k.astype(BF16) for k in kr]
    sc = [_dot_nt(q.astype(BF16), k) * dec_ref[h] for q, k, (c, h) in zip(qr, kb, units)]
    o = [_dot(s.astype(BF16), v) for s, v in zip(sc, vb)]
    q_dec = [(q * qd_ref[h]).astype(BF16) for q, (c, h) in zip(qr, units)]
    k_dec = [(k * kd_ref[h]).astype(BF16) for k, (c, h) in zip(kr, units)]
    if nseq > 1:
        for u, (c, h) in enumerate(units):
            for s in range(nseq):
                state = s_in_ref[s, h]
                o[u] = o[u] + _dot(q_dec[u], state.astype(BF16)) * row_masks[s]
                s_out_ref[s, h] = state * g_chunk[h] + _dot_tn(k_dec[u], (vf[u] * row_masks[s]).astype(BF16))
    else:
        kv = [_dot_tn(k, v) for k, v in zip(k_dec, vb)]
        state = [s_in_ref[0, h] for h in range(RET_HEADS)]
        for u, (c, h) in enumerate(units):
            o[u] = o[u] + _dot(q_dec[u], state[h].astype(BF16))
            state[h] = state[h] * g_chunk[h] + kv[u]
        for h in range(RET_HEADS):
            s_out_ref[0, h] = state[h]
    for u, (c, h) in enumerate(units):
        vsl = slice(h * RET_DV, (h + 1) * RET_DV)
        mu = jnp.mean(o[u], axis=-1, keepdims=True)
        oc = o[u] - mu
        var = jnp.mean(oc * oc, axis=-1, keepdims=True)
        on = oc * lax.rsqrt(var + EPS)
        on = (on * gw_ref[:, vsl] + gb_ref[:, vsl]) * _silu(gt_ref[rsl(c), vsl])
        o_ref[rsl(c), vsl] = on.astype(BF16)


def _ret(proj, cosf, sinf, dec, qd, kd, gw, gb, state, g_chunk, *, nbatch, nchunk):
    chained = state is None
    rows_total = proj.shape[0]
    if chained:
        nseq, seg = 1, CHUNK_ROWS
        step_rows = RET_STEP_CHUNKS * CHUNK_ROWS
        nstep = nchunk // RET_STEP_CHUNKS
        grid = (nbatch, nstep)
        rb = lambda b, n: b * nstep + n
        sb = lambda b, n: b
        pb = lambda b, n: n
    else:
        nseq, seg = 8, 8
        step_rows = CHUNK_ROWS
        grid = (rows_total // CHUNK_ROWS,)
        rb = lambda i: i
        sb = lambda i: i
        pb = lambda i: 0
    wrap = lambda f, *tail: (lambda *g: (f(*g),) + tail)
    const = lambda *tail: (lambda *g: tail)
    q_col = (GDN_CONV_W + GDN_V_W) // RET_QK_W
    v_col = (GDN_CONV_W + GDN_V_W + 2 * RET_QK_W) // RET_V_W
    in_specs = [
        pl.BlockSpec((step_rows, RET_QK_W), wrap(rb, q_col)),
        pl.BlockSpec((step_rows, RET_QK_W), wrap(rb, q_col + 1)),
        pl.BlockSpec((step_rows, RET_V_W), wrap(rb, v_col)),
        pl.BlockSpec((step_rows, RET_V_W), wrap(rb, v_col + 1)),
        pl.BlockSpec((step_rows, RET_DK), wrap(pb, 0)),
        pl.BlockSpec((step_rows, RET_DK), wrap(pb, 0)),
        pl.BlockSpec((RET_HEADS, CHUNK_ROWS, CHUNK_ROWS), const(0, 0, 0)),
        pl.BlockSpec((RET_HEADS, CHUNK_ROWS, RET_DK), const(0, 0, 0)),
        pl.BlockSpec((RET_HEADS, CHUNK_ROWS, RET_DK), const(0, 0, 0)),
        pl.BlockSpec((1, RET_V_W), const(0, 0)),
        pl.BlockSpec((1, RET_V_W), const(0, 0)),
    ]
    args = [proj, proj, proj, proj, cosf, sinf, dec, qd, kd, gw, gb]
    if not chained:
        in_specs.append(pl.BlockSpec((nseq, RET_HEADS, RET_DK, RET_DV), wrap(sb, 0, 0, 0)))
        args.append(state)
    out_specs = [
        pl.BlockSpec((step_rows, RET_V_W), wrap(rb, 0)),
        pl.BlockSpec((nseq, RET_HEADS, RET_DK, RET_DV), wrap(sb, 0, 0, 0)),
    ]
    out_shape = [
        jax.ShapeDtypeStruct((rows_total, RET_V_W), BF16),
        jax.ShapeDtypeStruct((nbatch, RET_HEADS, RET_DK, RET_DV), F32),
    ]
    return pl.pallas_call(
        functools.partial(_ret_kernel, nseq=nseq, seg=seg, chained=chained, g_chunk=g_chunk),
        grid=grid, in_specs=in_specs, out_specs=out_specs, out_shape=out_shape,
        compiler_params=_params(len(grid)),
        name="ret_chained" if chained else "ret_step",
    )(*args)


def _out_proj_kernel(x_ref, a_ref, b_ref, wa_ref, wb_ref, o_ref):
    o_ref[...] = x_ref[...] + (_dot(a_ref[...], wa_ref[...]) + _dot(b_ref[...], wb_ref[...]))


def _out_proj(x, o_a, o_b, w_a, w_b, tm=512):
    m = x.shape[0]
    return pl.pallas_call(
        _out_proj_kernel,
        grid=(m // tm,),
        in_specs=[
            pl.BlockSpec((tm, D_MODEL), lambda i: (i, 0)),
            pl.BlockSpec((tm, GDN_V_W), lambda i: (i, 0)),
            pl.BlockSpec((tm, RET_V_W), lambda i: (i, 0)),
            pl.BlockSpec((GDN_V_W, D_MODEL), lambda i: (0, 0)),
            pl.BlockSpec((RET_V_W, D_MODEL), lambda i: (0, 0)),
        ],
        out_specs=pl.BlockSpec((tm, D_MODEL), lambda i: (i, 0)),
        out_shape=jax.ShapeDtypeStruct((m, D_MODEL), F32),
        compiler_params=_params(1),
        name="out_proj",
    )(x, o_a, o_b, w_a, w_b)


def _ffn_kernel(*refs, nseq, seg, tiles_per_seq, final_norm):
    (x_ref, nw_ref, wv_ref, wg_ref, cwv_ref, cwg_ref, bv_ref, bg_ref, wd_ref, bufv_ref, bufg_ref,
     fw_ref, o_ref, nbv_ref, nbg_ref, h_scr) = refs[:16]
    chained = tiles_per_seq > 1
    i = pl.program_id(0)
    j = pl.program_id(1)
    nf = pl.num_programs(1)
    rows = nseq * seg
    taps = FFN_CONV - 1

    @pl.when(j == 0)
    def _():
        xv = x_ref[...]
        h_scr[...] = _rms_rows(xv, nw_ref[...]).astype(BF16)
        o_ref[...] = xv

    hb = h_scr[...]

    def branch_chained(w_ref, cw_ref, b_ref, buf_ref, nb_ref, carry_ref, up_scr):
        up = _dot(hb, w_ref[...])
        up_scr[8:8 + rows, :] = up
        up_scr[8 - taps:8, :] = jnp.where(i % tiles_per_seq == 0, buf_ref[0], carry_ref[j])
        tail = up[rows - taps:, :]
        nb_ref[0] = tail
        carry_ref[j] = tail
        out = up * cw_ref[taps:taps + 1, :] + b_ref[...]
        for k in range(1, FFN_CONV):
            out = out + up_scr[8 - k:8 - k + rows, :] * cw_ref[taps - k:taps - k + 1, :]
        return out

    def branch_short(w_ref, cw_ref, b_ref, buf_ref, nb_ref):
        up = _dot(hb, w_ref[...])
        tpos = lax.broadcasted_iota(jnp.int32, (rows, 1), 0) & (seg - 1)
        buf = buf_ref[...]
        out = up * cw_ref[taps:taps + 1, :] + b_ref[...]
        for k in range(1, FFN_CONV):
            out = out + _shift_rows(up, k, tpos, buf, nseq, seg) * cw_ref[taps - k:taps - k + 1, :]
        nb_ref[...] = up.reshape(nseq, seg, up.shape[1])[:, seg - taps:, :]
        return out

    if chained:
        cv_scr, cg_scr, uv_scr, ug_scr = refs[16:20]
        val = branch_chained(wv_ref, cwv_ref, bv_ref, bufv_ref, nbv_ref, cv_scr, uv_scr)
        gate = branch_chained(wg_ref, cwg_ref, bg_ref, bufg_ref, nbg_ref, cg_scr, ug_scr)
    else:
        val = branch_short(wv_ref, cwv_ref, bv_ref, bufv_ref, nbv_ref)
        gate = branch_short(wg_ref, cwg_ref, bg_ref, bufg_ref, nbg_ref)
    act = (_silu(gate) * val).astype(BF16)
    o_ref[...] += _dot(act, wd_ref[...])

    if final_norm:
        @pl.when(j == nf - 1)
        def _():
            o_ref[...] = _rms_rows(o_ref[...], fw_ref[...])


def _ffn(x, nw, w_up, conv_w, conv_b, w_down, buf, final_w, *, layer, nseq_total, final_norm,
         x_time_major=False, out_time_major=False, tm=1024, tf=512):
    m = x.shape[0] * x.shape[1] // D_MODEL
    seq_len = m // nseq_total
    if seq_len >= tm:
        nseq, seg, tiles_per_seq = 1, tm, seq_len // tm
    else:
        nseq, seg, tiles_per_seq = tm // seq_len, seq_len, 1
    nf = D_FF // tf
    sidx = (lambda i, j: i // tiles_per_seq) if tiles_per_seq > 1 else (lambda i, j: i)
    seq_major = lambda i, j: (i, 0)
    time_major = lambda i, j: (i % tiles_per_seq, i // tiles_per_seq)
    once = pl.Buffered(1)
    in_specs = [
        pl.BlockSpec((tm, D_MODEL), time_major if x_time_major else seq_major),
        pl.BlockSpec((1, D_MODEL), lambda i, j: (0, 0)),
        pl.BlockSpec((None, D_MODEL, tf), lambda i, j: (layer, 0, j)),
        pl.BlockSpec((None, D_MODEL, tf), lambda i, j: (layer, 0, nf + j)),
        pl.BlockSpec((None, FFN_CONV, tf), lambda i, j: (layer, 0, j)),
        pl.BlockSpec((None, FFN_CONV, tf), lambda i, j: (layer, 0, nf + j)),
        pl.BlockSpec((None, 1, tf), lambda i, j: (layer, 0, j)),
        pl.BlockSpec((None, 1, tf), lambda i, j: (layer, 0, nf + j)),
        pl.BlockSpec((None, tf, D_MODEL), lambda i, j: (layer, j, 0)),
        pl.BlockSpec((None, nseq, FFN_CONV - 1, tf), lambda i, j: (layer, sidx(i, j), 0, j)),
        pl.BlockSpec((None, nseq, FFN_CONV - 1, tf), lambda i, j: (layer, sidx(i, j), 0, nf + j)),
        pl.BlockSpec((1, D_MODEL), lambda i, j: (0, 0)),
    ]
    out_specs = [
        pl.BlockSpec((tm, D_MODEL), time_major if out_time_major else seq_major, pipeline_mode=once),
        pl.BlockSpec((nseq, FFN_CONV - 1, tf), lambda i, j: (i, 0, j)),
        pl.BlockSpec((nseq, FFN_CONV - 1, tf), lambda i, j: (i, 0, j)),
    ]
    out_shape = [
        jax.ShapeDtypeStruct((seq_len, nseq_total * D_MODEL) if out_time_major else (m, D_MODEL), F32),
        jax.ShapeDtypeStruct((m // tm * nseq, FFN_CONV - 1, D_FF), F32),
        jax.ShapeDtypeStruct((m // tm * nseq, FFN_CONV - 1, D_FF), F32),
    ]
    scratch = [pltpu.VMEM((tm, D_MODEL), BF16)]
    if tiles_per_seq > 1:
        scratch += [pltpu.VMEM((nf, FFN_CONV - 1, tf), F32), pltpu.VMEM((nf, FFN_CONV - 1, tf), F32),
                    pltpu.VMEM((8 + tm, tf), F32), pltpu.VMEM((8 + tm, tf), F32)]
    y, nbv, nbg = pl.pallas_call(
        functools.partial(_ffn_kernel, nseq=nseq, seg=seg, tiles_per_seq=tiles_per_seq, final_norm=final_norm),
        grid=(m // tm, nf), in_specs=in_specs, out_specs=out_specs, out_shape=out_shape,
        scratch_shapes=scratch, compiler_params=_params(2),
        name="conv_ffn",
    )(x, nw, w_up, w_up, conv_w, conv_w, conv_b, conv_b, w_down, buf, buf, final_w)
    new_buf = jnp.concatenate([nbv, nbg], axis=-1)
    return y, new_buf[tiles_per_seq - 1::tiles_per_seq]


def _s5_disc_kernel(lre_ref, lim_ref, ldt_ref, bre_ref, bim_ref, are_ref, aim_ref, bbre_ref, bbim_ref):
    lam_re = lre_ref[...]
    lam_im = lim_ref[...]
    dt = jnp.exp(ldt_ref[...])
    mag = jnp.exp(lam_re * dt)
    ph = lam_im * dt
    ab_re = mag * jnp.cos(ph)
    ab_im = mag * jnp.sin(ph)
    den = lam_re * lam_re + lam_im * lam_im
    cf_re = ((ab_re - 1.0) * lam_re + ab_im * lam_im) / den
    cf_im = (ab_im * lam_re - (ab_re - 1.0) * lam_im) / den
    are_ref[...] = ab_re
    aim_ref[...] = ab_im
    b_re = bre_ref[...]
    b_im = bim_ref[...]
    bbre_ref[...] = cf_re * b_re - cf_im * b_im
    bbim_ref[...] = cf_re * b_im + cf_im * b_re


def _s5_disc(lam_re, lam_im, log_dt, b_re_t, b_im_t):
    g, c, p = b_re_t.shape
    tile = lambda t: jnp.tile(t, (1, c))
    shp = jax.ShapeDtypeStruct((g, c * p), F32)
    a_re, a_im, bb_re, bb_im = pl.pallas_call(
        _s5_disc_kernel, out_shape=[shp, shp, shp, shp], name="s5_disc",
    )(tile(lam_re), tile(lam_im), jnp.broadcast_to(log_dt, (g, c * p)),
      b_re_t.reshape(g, c * p), b_im_t.reshape(g, c * p))
    return a_re[:, :p], a_im[:, :p], bb_re.reshape(g, c, p), bb_im.reshape(g, c, p)


def _gelu(y):
    return y * (0.5 * (1.0 + jnp.tanh(GELU_C * (y + 0.044715 * (y * y * y)))))


def _s5_step_kernel(x_ref, nw_ref, wb_ref, wcr_ref, wci_ref, are_ref, aim_ref, d_ref, sr_ref, si_ref,
                    g_ref, hr_ref, hi_ref, u_scr, xr_scr, xi_scr, *, nseq, seg):
    rows = nseq * seg
    u = _rms_rows(x_ref[...].reshape(rows, D_MODEL), nw_ref[...])
    u_scr[...] = u
    ub = u.astype(BF16)
    lpb = S5_BW // LANES
    for kb in range(S5_NB):
        res = _dot(ub[:, kb * LANES:(kb + 1) * LANES], wb_ref[kb])
        for q in range(lpb):
            xr_scr[kb * lpb + q] = res[:, q * LANES:(q + 1) * LANES]
            xi_scr[kb * lpb + q] = res[:, S5_BW + q * LANES:S5_BW + (q + 1) * LANES]

    for c in range(S5_STATE // S5_SCAN_W):
        blocks = range(c * S5_SCAN_LB, (c + 1) * S5_SCAN_LB)
        lane = [pl.ds(lb * LANES, LANES) for lb in blocks]
        a_re = [jnp.broadcast_to(are_ref[:, sl], (nseq, LANES)) for sl in lane]
        a_im = [jnp.broadcast_to(aim_ref[:, sl], (nseq, LANES)) for sl in lane]

        def step(t, carry):
            idx = pl.ds(t, nseq, stride=seg)
            out = []
            for n, lb in enumerate(blocks):
                s_re, s_im = carry[2 * n], carry[2 * n + 1]
                n_re = a_re[n] * s_re - a_im[n] * s_im + xr_scr[lb, idx, :]
                n_im = a_re[n] * s_im + a_im[n] * s_re + xi_scr[lb, idx, :]
                xr_scr[lb, idx, :] = n_re
                xi_scr[lb, idx, :] = n_im
                out += [n_re, n_im]
            return tuple(out)

        init = []
        for sl in lane:
            init += [sr_ref[:, sl], si_ref[:, sl]]
        fin = lax.fori_loop(0, seg, step, tuple(init))
        for n, sl in enumerate(lane):
            hr_ref[:, sl] = fin[2 * n]
            hi_ref[:, sl] = fin[2 * n + 1]

    for kb in range(S5_NB):
        xr = jnp.concatenate([xr_scr[kb * lpb + q] for q in range(lpb)], axis=1).astype(BF16)
        xi = jnp.concatenate([xi_scr[kb * lpb + q] for q in range(lpb)], axis=1).astype(BF16)
        lsl = slice(kb * LANES, (kb + 1) * LANES)
        y = _dot(xr, wcr_ref[kb]) - _dot(xi, wci_ref[kb]) + d_ref[:, lsl] * u_scr[:, lsl]
        g_ref[:, lsl] = _gelu(y).astype(BF16)


def _s5_step(x3, nw, wb, wcr, wci, a_re, a_im, d, h0_re, h0_im):
    b, l, _ = x3.shape
    nseq, seg = 8, l
    rows = nseq * seg
    const2 = lambda i: (0, 0)
    const3 = lambda i: (0, 0, 0)
    smap = lambda i: (i, 0)
    return pl.pallas_call(
        functools.partial(_s5_step_kernel, nseq=nseq, seg=seg),
        grid=(b // nseq,),
        in_specs=[
            pl.BlockSpec((nseq, seg, D_MODEL), lambda i: (i, 0, 0)),
            pl.BlockSpec((1, D_MODEL), const2),
            pl.BlockSpec((S5_NB, LANES, 2 * S5_BW), const3),
            pl.BlockSpec((S5_NB, S5_BW, LANES), const3),
            pl.BlockSpec((S5_NB, S5_BW, LANES), const3),
            pl.BlockSpec((1, S5_STATE), const2),
            pl.BlockSpec((1, S5_STATE), const2),
            pl.BlockSpec((1, D_MODEL), const2),
            pl.BlockSpec((nseq, S5_STATE), smap),
            pl.BlockSpec((nseq, S5_STATE), smap),
        ],
        out_specs=[
            pl.BlockSpec((rows, D_MODEL), smap),
            pl.BlockSpec((nseq, S5_STATE), smap),
            pl.BlockSpec((nseq, S5_STATE), smap),
        ],
        out_shape=[jax.ShapeDtypeStruct((b * l, D_MODEL), BF16),
                   jax.ShapeDtypeStruct((b, S5_STATE), F32),
                   jax.ShapeDtypeStruct((b, S5_STATE), F32)],
        scratch_shapes=[pltpu.VMEM((rows, D_MODEL), F32),
                        pltpu.VMEM((S5_STATE // LANES, rows, LANES), F32),
                        pltpu.VMEM((S5_STATE // LANES, rows, LANES), F32)],
        compiler_params=_params(1),
        name="s5_step",
    )(x3, nw, wb, wcr, wci, a_re, a_im, d, h0_re, h0_im)


def _s5_seq_kernel(x_ref, nw_ref, perm_ref, back_ref, wb_ref, wc_ref, are_ref, aim_ref, d_ref, h0r_ref, h0i_ref,
                   g_ref, hr_ref, hi_ref, st_scr, u_scr, ub_scr, x_scr, *, nseq, seg):
    wrows = 2 * nseq * seg

    @pl.when(pl.program_id(0) == 0)
    def _():
        st_scr[0:nseq, :] = h0r_ref[...]
        st_scr[nseq:2 * nseq, :] = h0i_ref[...]

    x = x_ref[...]
    nw = nw_ref[...]
    u = jnp.concatenate([_rms_rows(x[:, s * D_MODEL:(s + 1) * D_MODEL], nw) for s in range(nseq)], axis=0)
    u_hi = u.astype(BF16)
    u_lo = (u - u_hi.astype(F32)).astype(BF16)
    perm = perm_ref[...]
    cw = 512
    for c in range(D_MODEL // cw):
        csl = slice(c * cw, (c + 1) * cw)
        hi = _dot(perm, u_hi[:, csl])
        u_scr[:, csl] = hi + _dot(perm, u_lo[:, csl])
        ub_scr[:, csl] = hi.astype(BF16)

    row = lax.broadcasted_iota(jnp.int32, (wrows, 1), 0)
    is_re = (row & (2 * nseq - 1)) < nseq
    lpb = S5_BW // LANES
    tile = 2 * nseq
    sign = jnp.where(lax.broadcasted_iota(jnp.int32, (tile, 1), 0) < nseq, -1.0, 1.0)

    def project_in(kb):
        blk = ub_scr[:, kb * LANES:(kb + 1) * LANES]
        zero = jnp.zeros_like(blk)
        lhs = jnp.concatenate([jnp.where(is_re, blk, zero), jnp.where(is_re, zero, blk)], axis=1)
        res = _dot(lhs, wb_ref[kb])
        for q in range(lpb):
            x_scr[kb * lpb + q] = res[:, q * LANES:(q + 1) * LANES]

    def scan(kb):
        blocks = range(kb * lpb, (kb + 1) * lpb)
        lane = [slice(lb * LANES, (lb + 1) * LANES) for lb in blocks]
        a1 = [jnp.broadcast_to(are_ref[:, sl], (tile, LANES)) for sl in lane]
        a2 = [jnp.broadcast_to(aim_ref[:, sl], (tile, LANES)) * sign for sl in lane]
        st = [st_scr[:, sl] for sl in lane]
        for t in range(seg):
            for n, lb in enumerate(blocks):
                st[n] = a1[n] * st[n] + a2[n] * pltpu.roll(st[n], nseq, 0) + x_scr[lb, t * tile:(t + 1) * tile, :]
                x_scr[lb, t * tile:(t + 1) * tile, :] = st[n]
        for n, sl in enumerate(lane):
            st_scr[:, sl] = st[n]

    def read_out(kb):
        xb = jnp.concatenate([x_scr[kb * lpb + q] for q in range(lpb)], axis=1).astype(BF16)
        y2 = _dot(xb, wc_ref[kb])
        lsl = slice(kb * LANES, (kb + 1) * LANES)
        y = y2[:, :LANES] - pltpu.roll(y2[:, LANES:], wrows - nseq, 0) + d_ref[:, lsl] * u_scr[:, lsl]
        ub_scr[:, lsl] = _gelu(y).astype(BF16)

    for kb in range(S5_NB + 2):
        if kb < S5_NB:
            project_in(kb)
        if 1 <= kb <= S5_NB:
            scan(kb - 1)
        if kb >= 2:
            read_out(kb - 2)
    hr_ref[...] = st_scr[0:nseq, :]
    hi_ref[...] = st_scr[nseq:2 * nseq, :]

    back = back_ref[...]
    for c in range(D_MODEL // cw):
        csl = slice(c * cw, (c + 1) * cw)
        g_nat = _dot(back, ub_scr[:, csl])
        for s in range(nseq):
            g_ref[s, :, csl] = g_nat[s * seg:(s + 1) * seg].astype(BF16)


def _s5_seq(x_tm, nw, perm, back, wb2, wc2, a_re, a_im, d, h0_re, h0_im, *, nseq, seg):
    l = x_tm.shape[0]
    wrows = 2 * nseq * seg
    const2 = lambda n: (0, 0)
    const3 = lambda n: (0, 0, 0)
    return pl.pallas_call(
        functools.partial(_s5_seq_kernel, nseq=nseq, seg=seg),
        grid=(l // seg,),
        in_specs=[
            pl.BlockSpec((seg, nseq * D_MODEL), lambda n: (n, 0)),
            pl.BlockSpec((1, D_MODEL), const2),
            pl.BlockSpec((wrows, nseq * seg), const2),
            pl.BlockSpec((nseq * seg, wrows), const2),
            pl.BlockSpec((S5_NB, 2 * LANES, S5_BW), const3),
            pl.BlockSpec((S5_NB, S5_BW, 2 * LANES), const3),
            pl.BlockSpec((1, S5_STATE), const2),
            pl.BlockSpec((1, S5_STATE), const2),
            pl.BlockSpec((1, D_MODEL), const2),
            pl.BlockSpec((nseq, S5_STATE), const2),
            pl.BlockSpec((nseq, S5_STATE), const2),
        ],
        out_specs=[
            pl.BlockSpec((nseq, seg, D_MODEL), lambda n: (0, n, 0)),
            pl.BlockSpec((nseq, S5_STATE), const2),
            pl.BlockSpec((nseq, S5_STATE), const2),
        ],
        out_shape=[jax.ShapeDtypeStruct((nseq, l, D_MODEL), BF16),
                   jax.ShapeDtypeStruct((nseq, S5_STATE), F32),
                   jax.ShapeDtypeStruct((nseq, S5_STATE), F32)],
        scratch_shapes=[pltpu.VMEM((2 * nseq, S5_STATE), F32),
                        pltpu.VMEM((wrows, D_MODEL), F32),
                        pltpu.VMEM((wrows, D_MODEL), BF16),
                        pltpu.VMEM((S5_STATE // LANES, wrows, LANES), F32)],
        compiler_params=_params(1),
        name="s5_seq",
    )(x_tm, nw, perm, back, wb2, wc2, a_re, a_im, d, h0_re, h0_im)


def _s5_row_perm(nseq, seg):
    r = np.arange(2 * nseq * seg)
    src = (r % nseq) * seg + r // (2 * nseq)
    hit = src[:, None] == np.arange(nseq * seg)[None, :]
    real = (r % (2 * nseq)) < nseq
    return jnp.asarray(hit, BF16), jnp.asarray((hit & real[:, None]).T, BF16)


def _glu_kernel(x_ref, g_ref, wv_ref, wg_ref, o_ref):
    gb = g_ref[...]
    o_ref[...] = x_ref[...] + _dot(gb, wv_ref[...]) * _sigmoid(_dot(gb, wg_ref[...]))


def _glu(x, g, w_glu, *, time_major_seqs=0, tm=1024, tn=512):
    nn = D_MODEL // tn
    if time_major_seqs:
        tps = x.shape[0] // tm
        n_tiles = tps * time_major_seqs
        xmap = lambda i, j: (i % tps, (i // tps) * nn + j)
    else:
        n_tiles = x.shape[0] // tm
        xmap = lambda i, j: (i, j)
    return pl.pallas_call(
        _glu_kernel,
        grid=(n_tiles, nn),
        in_specs=[
            pl.BlockSpec((tm, tn), xmap),
            pl.BlockSpec((tm, D_MODEL), lambda i, j: (i, 0)),
            pl.BlockSpec((D_MODEL, tn), lambda i, j: (0, j)),
            pl.BlockSpec((D_MODEL, tn), lambda i, j: (0, nn + j)),
        ],
        out_specs=pl.BlockSpec((tm, tn), xmap),
        out_shape=jax.ShapeDtypeStruct(x.shape, F32),
        compiler_params=_params(2),
        name="glu",
    )(x, g, w_glu, w_glu)


def _rotary_tables(pos):
    half = RET_DK // 2
    inv = ROPE_BASE ** (-jnp.arange(half, dtype=F32) / half)
    ang = pos[:, None] * inv[None, :]
    cos = jnp.cos(ang)
    sin = jnp.sin(ang)
    return jnp.concatenate([cos, cos], axis=-1), jnp.concatenate([-sin, sin], axis=-1)


def _retention_tables(nseq, seg):
    log_g = np.log1p(-np.exp2(-5.0 - np.arange(RET_HEADS)))
    row = np.arange(nseq * seg)
    t = (row % seg).astype(np.float64)
    sid = row // seg
    diff = t[:, None] - t[None, :]
    ok = (sid[:, None] == sid[None, :]) & (diff >= 0)
    dec = np.where(ok, np.exp(log_g[:, None, None] * np.where(ok, diff, 0.0)), 0.0)
    qd = np.broadcast_to(np.exp(log_g[:, None] * (t + 1.0))[..., None], (RET_HEADS, nseq * seg, RET_DK))
    kd = np.broadcast_to(np.exp(log_g[:, None] * (seg - 1.0 - t))[..., None], (RET_HEADS, nseq * seg, RET_DK))
    gains = tuple(float(v) for v in np.exp(log_g * seg))
    return jnp.asarray(dec, F32), jnp.asarray(qd, F32), jnp.asarray(kd, F32), gains


def _block_diag(blocks, rows_per, cols_per):
    tiled = jnp.tile(blocks.reshape(S5_NB, S5_GB * rows_per, cols_per), (1, 1, S5_GB))
    rg = np.arange(S5_GB * rows_per)[:, None] // rows_per
    cg = np.arange(S5_GB * cols_per)[None, :] // cols_per
    return (tiled * jnp.asarray(rg == cg, F32)).astype(BF16)


def _trunk(x3, pos, gdn_s, gdn_cb, ret_s, s5_re, s5_im, ffn_cb, prm, *, chained):
    b, l, _ = x3.shape
    m = b * l
    x = x3.reshape(m, D_MODEL)
    nchunk = l // CHUNK_ROWS if chained else 1

    proj, gate = _in_proj(x, prm['norm_mix_w'][0:1], prm['w_in_main'], prm['w_in_gate'])
    o_a, gdn_new, gcb_new = _gdn(proj, gate, gdn_cb, prm['gdn_conv_w'], prm['alog_row'], prm['dtb_row'],
                                 prm['gdn_norm_w'], None if chained else gdn_s, nbatch=b, nchunk=nchunk)
    if chained:
        cosf, sinf = _rotary_tables(pos)
        dec, qd, kd, g_chunk = _retention_tables(1, CHUNK_ROWS)
    else:
        cosf, sinf = _rotary_tables(jnp.tile(pos, CHUNK_ROWS // l))
        dec, qd, kd, g_chunk = _retention_tables(CHUNK_ROWS // l, l)
    o_b, ret_new = _ret(proj, cosf, sinf, dec, qd, kd, prm['ret_gn_w'], prm['ret_gn_b'],
                        None if chained else ret_s, g_chunk, nbatch=b, nchunk=nchunk)
    x = _out_proj(x, o_a, o_b, prm['w_out_a'], prm['w_out_b'])
    x, fcb0 = _ffn(x, prm['norm_ffn_w'][0:1], prm['w_up'], prm['ffn_conv_w'], prm['ffn_conv_b'],
                   prm['w_down'], ffn_cb, prm['norm_final_w'], layer=0, nseq_total=b, final_norm=False,
                   out_time_major=chained)

    if chained:
        g, s5r_new, s5i_new = _s5_seq(x, prm['norm_mix_w'][1:2], *_s5_row_perm(b, CHUNK_ROWS), prm['s5_wb2'],
                                      prm['s5_wc2'], prm['s5_a_re'], prm['s5_a_im'], prm['s5_d'],
                                      s5_re, s5_im, nseq=b, seg=CHUNK_ROWS)
        x = _glu(x, g.reshape(m, D_MODEL), prm['w_glu'], time_major_seqs=b)
    else:
        g, s5r_new, s5i_new = _s5_step(x.reshape(b, l, D_MODEL), prm['norm_mix_w'][1:2], prm['s5_wb'],
                                       prm['s5_wc_re'], prm['s5_wc_im'], prm['s5_a_re'], prm['s5_a_im'],
                                       prm['s5_d'], s5_re, s5_im)
        x = _glu(x, g, prm['w_glu'])
    y, fcb1 = _ffn(x, prm['norm_ffn_w'][1:2], prm['w_up'], prm['ffn_conv_w'], prm['ffn_conv_b'],
                   prm['w_down'], ffn_cb, prm['norm_final_w'], layer=1, nseq_total=b, final_norm=True,
                   x_time_major=chained)
    return (y.reshape(b, l, D_MODEL), gdn_new[None], gcb_new[None], ret_new[None],
            s5r_new.reshape(1, b, S5_GROUPS, S5_P), s5i_new.reshape(1, b, S5_GROUPS, S5_P),
            jnp.stack([fcb0, fcb1]))


def kernel(x_prompt, x_sample, state_gdn, state_gdn_conv, state_ret, state_s5_re, state_s5_im,
           state_ffn_conv, norm_mix_w, norm_ffn_w, norm_final_w,
           w_in, gdn_conv_w, gdn_a_log, gdn_dt_bias, gdn_norm_w, ret_gn_w, ret_gn_b, w_out,
           s5_lam_re, s5_lam_im, s5_log_dt, s5_b_re, s5_b_im, s5_c_re, s5_c_im, s5_d, w_glu,
           w_up, ffn_conv_w, ffn_conv_b, w_down):
    bp, lp, _ = x_prompt.shape
    bs, ls, _ = x_sample.shape
    past_len = 16384

    w_in_main, w_in_gate = _w_in_prep(w_in[0])
    pad_row = lambda v: jnp.pad(v.reshape(1, -1), ((0, 0), (0, GATE_W - v.shape[-1])))

    a_re, a_im, bb_re, bb_im = _s5_disc(s5_lam_re[0], s5_lam_im[0], s5_log_dt[0].reshape(S5_GROUPS, 1),
                                        s5_b_re[0].transpose(0, 2, 1), s5_b_im[0].transpose(0, 2, 1))
    wb_re, wb_im = _block_diag(bb_re, S5_GROUP, S5_P), _block_diag(bb_im, S5_GROUP, S5_P)
    s5_wb = jnp.concatenate([wb_re, wb_im], axis=-1)
    s5_wb2 = jnp.concatenate([wb_re, wb_im], axis=1)
    s5_wc_re = _block_diag(s5_c_re[0].transpose(0, 2, 1), S5_P, S5_GROUP)
    s5_wc_im = _block_diag(s5_c_im[0].transpose(0, 2, 1), S5_P, S5_GROUP)
    s5_wc2 = jnp.concatenate([s5_wc_re, s5_wc_im], axis=-1)

    prm = dict(
        norm_mix_w=norm_mix_w, norm_ffn_w=norm_ffn_w, norm_final_w=norm_final_w.reshape(1, D_MODEL),
        w_in_main=w_in_main, w_in_gate=w_in_gate,
        gdn_conv_w=gdn_conv_w[0], alog_row=pad_row(gdn_a_log[0]), dtb_row=pad_row(gdn_dt_bias[0]),
        gdn_norm_w=gdn_norm_w[0].reshape(1, GDN_DV),
        ret_gn_w=ret_gn_w[0].reshape(1, RET_V_W), ret_gn_b=ret_gn_b[0].reshape(1, RET_V_W),
        w_out_a=w_out[0, :GDN_V_W].astype(BF16), w_out_b=w_out[0, GDN_V_W:].astype(BF16),
        s5_wb=s5_wb, s5_wc_re=s5_wc_re, s5_wc_im=s5_wc_im, s5_wb2=s5_wb2, s5_wc2=s5_wc2,
        s5_a_re=a_re.reshape(1, S5_STATE), s5_a_im=a_im.reshape(1, S5_STATE), s5_d=s5_d[0].reshape(1, D_MODEL),
        w_glu=w_glu[0].astype(BF16),
        w_up=w_up.astype(BF16), ffn_conv_w=ffn_conv_w, ffn_conv_b=ffn_conv_b[:, None, :],
        w_down=w_down.astype(BF16),
    )

    z_gcb = jnp.zeros((bp, GDN_CONV - 1, GDN_CONV_W), F32)
    z_s5 = jnp.zeros((bp, S5_STATE), F32)
    z_fcb = jnp.zeros((2, bp, FFN_CONV - 1, 2 * D_FF), F32)
    pos_p = jnp.arange(lp, dtype=F32)
    pos_s = past_len + jnp.arange(ls, dtype=F32)

    y_p, gdn_p, gcb_p, ret_p, s5r_p, s5i_p, fcb_p = _trunk(
        x_prompt, pos_p, None, z_gcb, None, z_s5, z_s5, z_fcb, prm, chained=True)
    y_s, gdn_s, gcb_s, ret_s, s5r_s, s5i_s, fcb_s = _trunk(
        x_sample, pos_s, state_gdn[0], state_gdn_conv[0], state_ret[0],
        state_s5_re[0].reshape(bs, S5_STATE), state_s5_im[0].reshape(bs, S5_STATE),
        state_ffn_conv, prm, chained=False)
    return (y_p, y_s, gdn_p, gdn_s, gcb_p, gcb_s, ret_p, ret_s,
            s5r_p, s5r_s, s5i_p, s5i_s, fcb_p, fcb_s)
```

```python
import functools
import math

import jax
import jax.numpy as jnp
import numpy as np
from jax import lax
from jax.experimental import pallas as pl
from jax.experimental.pallas import tpu as pltpu

F32 = jnp.float32
BF16 = jnp.bfloat16

D_MODEL = 2048
GDN_HEADS = 8
GDN_DK = 128
GDN_DV = 128
GDN_CONV = 4
RET_HEADS = 4
RET_DK = 128
RET_DV = 256
ROPE_BASE = 10000.0
S5_GROUP = 16
S5_GROUPS = 128
S5_P = 64
S5_STATE = S5_GROUPS * S5_P
D_FF = 5632
FFN_CONV = 3
EPS = 1e-6

GDN_QK_W = GDN_HEADS * GDN_DK
GDN_V_W = GDN_HEADS * GDN_DV
GDN_CONV_W = 2 * GDN_QK_W + GDN_V_W
RET_QK_W = RET_HEADS * RET_DK
RET_V_W = RET_HEADS * RET_DV
PROJ_W = GDN_CONV_W + GDN_V_W + 2 * RET_QK_W + 2 * RET_V_W
GATE_W = 128

CHUNK_ROWS = 64
GDN_GROUP = 4
GDN_STEP_CHUNKS = 2
GDN_DIAG = 16
RET_STEP_CHUNKS = 4
LANES = 128
VMEM_LIMIT = 56 * 1024 * 1024

S5_GB = 8
S5_NB = S5_GROUPS // S5_GB
S5_BW = S5_GB * S5_P
GELU_C = math.sqrt(2.0 / math.pi)


def _params(n_grid):
    return pltpu.CompilerParams(dimension_semantics=("arbitrary",) * n_grid,
                                vmem_limit_bytes=VMEM_LIMIT)


def _dot(a, b):
    return jnp.dot(a, b, preferred_element_type=F32)


def _dot_nt(a, b):
    return lax.dot_general(a, b, (((1,), (1,)), ((), ())), preferred_element_type=F32)


def _dot_tn(a, b):
    return lax.dot_general(a, b, (((0,), (0,)), ((), ())), preferred_element_type=F32)


def _sigmoid(x):
    return 0.5 + 0.5 * jnp.tanh(0.5 * x)


def _silu(x):
    h = 0.5 * x
    return h + h * jnp.tanh(h)


def _rms_rows(x, w):
    return x * lax.rsqrt(jnp.mean(x * x, axis=-1, keepdims=True) + EPS) * w


def _shift_rows(x, k, tpos, buf, nseq, seg):
    rows, width = x.shape
    nb = buf.shape[1]
    prev = pltpu.roll(x, k, 0)
    for t in range(k):
        src = buf[:, nb - k + t:nb - k + t + 1, :]
        srcb = jnp.broadcast_to(src, (nseq, seg, width)).reshape(rows, width)
        prev = jnp.where(tpos == t, srcb, prev)
    return prev


def _w_in_prep_kernel(w_ref, main_ref, gate_ref):
    o_gate = GDN_CONV_W + GDN_V_W
    o_rest = o_gate + 2 * GDN_HEADS
    main_ref[:, :o_gate] = w_ref[:, :o_gate].astype(BF16)
    main_ref[:, o_gate:] = w_ref[:, o_rest:].astype(BF16)
    lane = lax.broadcasted_iota(jnp.int32, (1, GATE_W), 1)
    gate_ref[...] = jnp.where(lane < 2 * GDN_HEADS, w_ref[:, o_gate:o_gate + GATE_W], 0.0).astype(BF16)


def _w_in_prep(w, tr=256):
    _, k, n = w.shape
    return pl.pallas_call(
        _w_in_prep_kernel,
        grid=(k // tr,),
        in_specs=[pl.BlockSpec((None, tr, n), lambda i: (0, i, 0))],
        out_specs=[pl.BlockSpec((tr, PROJ_W), lambda i: (i, 0)),
                   pl.BlockSpec((tr, GATE_W), lambda i: (i, 0))],
        out_shape=[jax.ShapeDtypeStruct((k, PROJ_W), BF16), jax.ShapeDtypeStruct((k, GATE_W), BF16)],
        compiler_params=_params(1),
        name="w_in_prep",
    )(w)


def _in_proj_kernel(x_ref, nw_ref, w_ref, wg_ref, o_ref, og_ref, h_scr):
    @pl.when(pl.program_id(1) == 0)
    def _():
        hb = _rms_rows(x_ref[...], nw_ref[...]).astype(BF16)
        h_scr[...] = hb
        og_ref[...] = _dot(hb, wg_ref[...])

    o_ref[...] = _dot(h_scr[...], w_ref[...])


IN_PROJ_ROWS = 2048
IN_PROJ_TILE = 2048 * 512


def _in_proj(x, nw, w_main, w_gate):
    m = x.shape[0]
    tm = min(IN_PROJ_ROWS, m)
    tn = IN_PROJ_TILE // tm
    return pl.pallas_call(
        _in_proj_kernel,
        grid=(m // tm, PROJ_W // tn),
        in_specs=[
            pl.BlockSpec((tm, D_MODEL), lambda i, j: (i, 0), pipeline_mode=pl.Buffered(1)),
            pl.BlockSpec((1, D_MODEL), lambda i, j: (0, 0)),
            pl.BlockSpec((D_MODEL, tn), lambda i, j: (0, j)),
            pl.BlockSpec((D_MODEL, GATE_W), lambda i, j: (0, 0)),
        ],
        out_specs=[
            pl.BlockSpec((tm, tn), lambda i, j: (i, j)),
            pl.BlockSpec((tm, GATE_W), lambda i, j: (i, 0)),
        ],
        out_shape=[jax.ShapeDtypeStruct((m, PROJ_W), F32),
                   jax.ShapeDtypeStruct((m, GATE_W), F32)],
        scratch_shapes=[pltpu.VMEM((tm, D_MODEL), BF16)],
        compiler_params=_params(2),
        name="in_proj",
    )(x, nw, w_main, w_gate)


def _gdn_kernel(*refs, nseq, seg, chained):
    if chained:
        (qkv_ref, z_ref, gate_ref, cbuf_ref, cw_ref, alog_ref, dtb_ref, nw_ref, sum_ref, mask_ref,
         o_ref, s_out_ref, cb_out_ref, carry_scr) = refs
        s_in_ref = s_out_ref
        first = pl.program_id(1) == 0

        @pl.when(first)
        def _():
            s_out_ref[...] = jnp.zeros_like(s_out_ref)
    else:
        (qkv_ref, z_ref, gate_ref, cbuf_ref, cw_ref, alog_ref, dtb_ref, nw_ref, sum_ref, mask_ref, s_in_ref,
         o_ref, s_out_ref, cb_out_ref) = refs

    rows = nseq * seg
    step_rows = qkv_ref.shape[0]
    cps = step_rows // rows
    shift = int(math.log2(seg))
    x = qkv_ref[...]
    if chained:
        buf = jnp.where(first, cbuf_ref[...], carry_scr[5:8, :].reshape(1, GDN_CONV - 1, GDN_CONV_W))
        conv_seg = step_rows
    else:
        buf = cbuf_ref[...]
        conv_seg = seg
    row_id = lax.broadcasted_iota(jnp.int32, (step_rows, 1), 0)
    tpos = row_id & (conv_seg - 1)

    acc = x * cw_ref[GDN_CONV - 1:GDN_CONV, :]
    for k in range(1, GDN_CONV):
        acc = acc + _shift_rows(x, k, tpos, buf, nseq, conv_seg) * cw_ref[GDN_CONV - 1 - k:GDN_CONV - k, :]
    act = _silu(acc)
    cb_out_ref[...] = x.reshape(nseq, conv_seg, GDN_CONV_W)[:, conv_seg - (GDN_CONV - 1):, :]
    if chained:
        carry_scr[...] = x[step_rows - 8:, :]

    gate = gate_ref[...]
    xa = gate + dtb_ref[...]
    softplus = jnp.maximum(xa, 0.0) + jnp.log(1.0 + jnp.exp(-jnp.abs(xa)))
    log_a = -jnp.exp(alog_ref[...]) * softplus
    beta_all = _sigmoid(gate)

    sum_m = sum_ref[...]
    a1 = log_a.astype(BF16)
    r1 = log_a - a1.astype(F32)
    a2 = r1.astype(BF16)
    a3 = (r1 - a2.astype(F32)).astype(BF16)
    if nseq > 1:
        row_masks = [((row_id >> shift) == s).astype(F32) for s in range(nseq)]
    nw = nw_ref[...]
    chunk_sl = [slice(c * rows, (c + 1) * rows) for c in range(cps)]
    g_all = [_dot(sum_m, a1[sl]) + (_dot(sum_m, a2[sl]) + _dot(sum_m, a3[sl])) for sl in chunk_sl]
    g = [t[:rows] for t in g_all]
    g_last = [t[rows:] for t in g_all]
    g2_t = [jnp.concatenate([t, t], axis=0).T for t in g]
    e_g = [jnp.exp(t) for t in g]
    e_rest = [jnp.exp(tl - t) for tl, t in zip(g_last, g)]
    e_last = [jnp.exp(t) for t in g_last]

    incl_f = mask_ref[0]
    strict_f = mask_ref[1]
    lane = lax.broadcasted_iota(jnp.int32, (1, LANES), 1)

    def stack_cols(tile, heads, off=0):
        return jnp.concatenate([tile[:, off + h:off + h + 1] for h in heads], axis=0)

    groups = [list(range(g0, g0 + GDN_GROUP)) for g0 in range(0, GDN_HEADS, GDN_GROUP)]
    units = [(c, heads) for c in range(cps) for heads in groups]
    ks, q_all, k_all, v_all = [], [], [], []
    for c, heads in units:
        a_c = act[chunk_sl[c]]
        qs, kn, vs = [], [], []
        for h in heads:
            lo = h * GDN_DK
            q = a_c[:, lo:lo + GDN_DK]
            k = a_c[:, GDN_QK_W + lo:GDN_QK_W + lo + GDN_DK]
            qs.append(q * lax.rsqrt(jnp.sum(q * q, axis=-1, keepdims=True) + EPS) * (GDN_DK ** -0.5))
            kn.append(k * lax.rsqrt(jnp.sum(k * k, axis=-1, keepdims=True) + EPS))
            vs.append(a_c[:, 2 * GDN_QK_W + lo:2 * GDN_QK_W + lo + GDN_DV])
        ks.append(kn)
        q_all.append(jnp.concatenate(qs, axis=0))
        k_all.append(jnp.concatenate(kn, axis=0))
        v_all.append(jnp.concatenate(vs, axis=0))
    gc = [stack_cols(g[c], heads) for c, heads in units]
    bc = [stack_cols(beta_all[chunk_sl[c]], heads, GDN_HEADS) for c, heads in units]
    egc = [stack_cols(e_g[c], heads) for c, heads in units]
    gr = [jnp.concatenate(
        [jnp.where(lane < rows, g2_t[c][heads[j]:heads[j] + 1, :], g2_t[c][heads[j + 1]:heads[j + 1] + 1, :])
         for j in range(0, GDN_GROUP, 2)], axis=1) for c, heads in units]
    decay = [jnp.exp(jnp.minimum(a - r, 0.0)) * incl_f for a, r in zip(gc, gr)]
    kb = [k.astype(BF16) for k in k_all]
    kk = [_dot_nt(k, k) for k in kb]
    qk = [_dot_nt(q.astype(BF16), k) * d for q, k, d in zip(q_all, kb, decay)]
    lms = [b * m * d * strict_f for b, m, d in zip(bc, kk, decay)]
    inv_e = _unit_lower_inverse_minus_eye(lms, mask_ref, seg)
    rhs = [jnp.concatenate([v * b, k * (b * e)], axis=-1) for v, k, b, e in zip(v_all, k_all, bc, egc)]
    sol = [r + _dot(e.astype(BF16), r.astype(BF16)) for e, r in zip(inv_e, rhs)]
    qd_all = [q * e for q, e in zip(q_all, egc)]

    states = [[s_in_ref[s, h] for s in range(nseq)] for h in range(GDN_HEADS)]
    for c in range(cps):
        unit_ids = [u for u, (uc, _) in enumerate(units) if uc == c]
        v_new, o_all = {}, {}
        for u in unit_ids:
            ws_parts, qs_parts = [], []
            for j, h in enumerate(units[u][1]):
                r0 = j * rows
                lhs = jnp.concatenate([sol[u][r0:r0 + rows, GDN_DV:], qd_all[u][r0:r0 + rows]], axis=0).astype(BF16)
                ws = None
                for s in range(nseq):
                    part = _dot(lhs, states[h][s].astype(BF16))
                    if nseq > 1:
                        part = part * jnp.concatenate([row_masks[s], row_masks[s]], axis=0)
                    ws = part if ws is None else ws + part
                ws_parts.append(ws[:rows])
                qs_parts.append(ws[rows:])
            v_new[u] = sol[u][:, :GDN_DV] - jnp.concatenate(ws_parts, axis=0)
            o_all[u] = jnp.concatenate(qs_parts, axis=0)
        for u in unit_ids:
            o_all[u] = o_all[u] + _dot(qk[u].astype(BF16), v_new[u].astype(BF16))
        for u in unit_ids:
            for j, h in enumerate(units[u][1]):
                r0 = j * rows
                lo = h * GDN_DK
                k_rest = (ks[u][j] * e_rest[c][:, h:h + 1]).astype(BF16)
                vn = v_new[u][r0:r0 + rows]
                for s in range(nseq):
                    vsel = vn * row_masks[s] if nseq > 1 else vn
                    states[h][s] = (states[h][s] * e_last[c][s * seg:s * seg + 1, h:h + 1]
                                    + _dot_tn(k_rest, vsel.astype(BF16)))
                o = o_all[u][r0:r0 + rows]
                o = o * lax.rsqrt(jnp.mean(o * o, axis=-1, keepdims=True) + EPS) * nw
                o = o * _silu(z_ref[chunk_sl[c], lo:lo + GDN_DV])
                o_ref[chunk_sl[c], lo:lo + GDN_DV] = o.astype(BF16)
    for h in range(GDN_HEADS):
        for s in range(nseq):
            s_out_ref[s, h] = states[h][s]


def _gdn_masks(nseq, seg):
    n = GDN_GROUP * nseq * seg
    ri, ci = np.arange(n)[:, None], np.arange(n)[None, :]
    same = (ri // seg) == (ci // seg)
    blk = min(GDN_DIAG, seg)
    masks = [same & (ci <= ri), same & (ci < ri), -1.0 * ((ri // blk) == (ci // blk))]
    size = blk
    while size < seg:
        masks.append(((ri // (2 * size)) == (ci // (2 * size))) & ((ri // size) != (ci // size)))
        size *= 2
    return jnp.asarray(np.stack([np.asarray(m, np.float32) for m in masks]), F32)


def _gdn_sum_matrix(nseq, seg):
    n = nseq * seg
    ri, ci = np.arange(n)[:, None], np.arange(n)[None, :]
    same = (ri // seg) == (ci // seg)
    return jnp.asarray(np.concatenate([same & (ci <= ri), same]), BF16)


def _unit_lower_inverse_minus_eye(lms, mask_ref, seg):
    blk = min(GDN_DIAG, seg)
    neg_diag = mask_ref[2]
    err = [lm * neg_diag for lm in lms]
    m_b = [e.astype(BF16) for e in err]
    for _ in range(int(math.log2(blk)) - 1):
        m_f = [_dot(m, m) for m in m_b]
        m_b = [m.astype(BF16) for m in m_f]
        err = [e + m + _dot(e.astype(BF16), mb) for e, m, mb in zip(err, m_f, m_b)]
    level = 3
    size = blk
    while size < seg:
        c = [lm * mask_ref[level] for lm in lms]
        e_b = [e.astype(BF16) for e in err]
        y = [x + _dot(e, x.astype(BF16)) for x, e in zip(c, e_b)]
        err = [e - (v + _dot(v.astype(BF16), eb)) for e, v, eb in zip(err, y, e_b)]
        level += 1
        size *= 2
    return err


def _gdn(proj, gate, cbuf, conv_w, alog_row, dtb_row, norm_w, state, *, nbatch, nchunk):
    chained = state is None
    rows_total = proj.shape[0]
    if chained:
        nseq, seg = 1, CHUNK_ROWS
        step_rows = GDN_STEP_CHUNKS * CHUNK_ROWS
        nstep = nchunk // GDN_STEP_CHUNKS
        grid = (nbatch, nstep)
        rb = lambda b, n: b * nstep + n
        sb = lambda b, n: b
    else:
        nseq, seg = 8, 8
        step_rows = CHUNK_ROWS
        grid = (rows_total // CHUNK_ROWS,)
        rb = lambda i: i
        sb = lambda i: i
    n_state = nbatch
    sum_m = _gdn_sum_matrix(nseq, seg)
    masks = _gdn_masks(nseq, seg)
    wrap = lambda f, *tail: (lambda *g: (f(*g),) + tail)
    const = lambda *tail: (lambda *g: tail)
    in_specs = [
        pl.BlockSpec((step_rows, GDN_CONV_W), wrap(rb, 0)),
        pl.BlockSpec((step_rows, GDN_V_W), wrap(rb, GDN_CONV_W // GDN_V_W)),
        pl.BlockSpec((step_rows, GATE_W), wrap(rb, 0)),
        pl.BlockSpec((nseq, GDN_CONV - 1, GDN_CONV_W), wrap(sb, 0, 0)),
        pl.BlockSpec((GDN_CONV, GDN_CONV_W), const(0, 0)),
        pl.BlockSpec((1, GATE_W), const(0, 0)),
        pl.BlockSpec((1, GATE_W), const(0, 0)),
        pl.BlockSpec((1, GDN_DV), const(0, 0)),
        pl.BlockSpec(sum_m.shape, const(0, 0)),
        pl.BlockSpec(masks.shape, const(0, 0, 0)),
    ]
    args = [proj, proj, gate, cbuf, conv_w, alog_row, dtb_row, norm_w, sum_m, masks]
    scratch = []
    if chained:
        scratch = [pltpu.VMEM((8, GDN_CONV_W), F32)]
    else:
        in_specs.append(pl.BlockSpec((nseq, GDN_HEADS, GDN_DK, GDN_DV), wrap(sb, 0, 0, 0)))
        args.append(state)
    out_specs = [
        pl.BlockSpec((step_rows, GDN_V_W), wrap(rb, 0)),
        pl.BlockSpec((nseq, GDN_HEADS, GDN_DK, GDN_DV), wrap(sb, 0, 0, 0)),
        pl.BlockSpec((nseq, GDN_CONV - 1, GDN_CONV_W), wrap(sb, 0, 0)),
    ]
    out_shape = [
        jax.ShapeDtypeStruct((rows_total, GDN_V_W), BF16),
        jax.ShapeDtypeStruct((n_state, GDN_HEADS, GDN_DK, GDN_DV), F32),
        jax.ShapeDtypeStruct((n_state, GDN_CONV - 1, GDN_CONV_W), F32),
    ]
    return pl.pallas_call(
        functools.partial(_gdn_kernel, nseq=nseq, seg=seg, chained=chained),
        grid=grid, in_specs=in_specs, out_specs=out_specs, out_shape=out_shape,
        scratch_shapes=scratch, compiler_params=_params(len(grid)),
        name="gdn_chained" if chained else "gdn_step",
    )(*args)


def _ret_kernel(*refs, nseq, seg, chained, g_chunk):
    if chained:
        (q_ref, k_ref, v_ref, gt_ref, cos_ref, sin_ref, dec_ref, qd_ref, kd_ref, gw_ref, gb_ref,
         o_ref, s_out_ref) = refs
        s_in_ref = s_out_ref

        @pl.when(pl.program_id(1) == 0)
        def _():
            s_out_ref[...] = jnp.zeros_like(s_out_ref)
    else:
        (q_ref, k_ref, v_ref, gt_ref, cos_ref, sin_ref, dec_ref, qd_ref, kd_ref, gw_ref, gb_ref,
         s_in_ref, o_ref, s_out_ref) = refs

    rows = nseq * seg
    cps = q_ref.shape[0] // rows
    shift = int(math.log2(seg))
    if nseq > 1:
        row_id = lax.broadcasted_iota(jnp.int32, (rows, 1), 0)
        row_masks = [((row_id >> shift) == s).astype(F32) for s in range(nseq)]

    units = [(c, h) for c in range(cps) for h in range(RET_HEADS)]
    rsl = lambda c: slice(c * rows, (c + 1) * rows)
    qr, kr = [], []
    for c, h in units:
        cosf = cos_ref[rsl(c), :]
        sinf = sin_ref[rsl(c), :]
        q = q_ref[rsl(c), h * RET_DK:(h + 1) * RET_DK]
        k = k_ref[rsl(c), h * RET_DK:(h + 1) * RET_DK]
        qr.append(q * cosf + pltpu.roll(q, RET_DK // 2, 1) * sinf)
        kr.append((k * cosf + pltpu.roll(k, RET_DK // 2, 1) * sinf) * (RET_DK ** -0.5))
    vf = [v_ref[rsl(c), h * RET_DV:(h + 1) * RET_DV] for c, h in units]
    vb = [v.astype(BF16) for v in vf]
    kb = [k.astype(BF16) for k in kr]
    sc = [_dot_nt(q.astype(BF16), k) * dec_ref[h] for q, k, (c, h) in zip(qr, kb, units)]
    o = [_dot(s.astype(BF16), v) for s, v in zip(sc, vb)]
    q_dec = [(q * qd_ref[h]).astype(BF16) for q, (c, h) in zip(qr, units)]
    k_dec = [(k * kd_ref[h]).astype(BF16) for k, (c, h) in zip(kr, units)]
    if nseq > 1:
        for u, (c, h) in enumerate(units):
            for s in range(nseq):
                state = s_in_ref[s, h]
                o[u] = o[u] + _dot(q_dec[u], state.astype(BF16)) * row_masks[s]
                s_out_ref[s, h] = state * g_chunk[h] + _dot_tn(k_dec[u], (vf[u] * row_masks[s]).astype(BF16))
    else:
        kv = [_dot_tn(k, v) for k, v in zip(k_dec, vb)]
        state = [s_in_ref[0, h] for h in range(RET_HEADS)]
        for u, (c, h) in enumerate(units):
            o[u] = o[u] + _dot(q_dec[u], state[h].astype(BF16))
            state[h] = state[h] * g_chunk[h] + kv[u]
        for h in range(RET_HEADS):
            s_out_ref[0, h] = state[h]
    for u, (c, h) in enumerate(units):
        vsl = slice(h * RET_DV, (h + 1) * RET_DV)
        mu = jnp.mean(o[u], axis=-1, keepdims=True)
        oc = o[u] - mu
        var = jnp.mean(oc * oc, axis=-1, keepdims=True)
        on = oc * lax.rsqrt(var + EPS)
        on = (on * gw_ref[:, vsl] + gb_ref[:, vsl]) * _silu(gt_ref[rsl(c), vsl])
        o_ref[rsl(c), vsl] = on.astype(BF16)


def _ret(proj, cosf, sinf, dec, qd, kd, gw, gb, state, g_chunk, *, nbatch, nchunk):
    chained = state is None
    rows_total = proj.shape[0]
    if chained:
        nseq, seg = 1, CHUNK_ROWS
        step_rows = RET_STEP_CHUNKS * CHUNK_ROWS
        nstep = nchunk // RET_STEP_CHUNKS
        grid = (nbatch, nstep)
        rb = lambda b, n: b * nstep + n
        sb = lambda b, n: b
        pb = lambda b, n: n
    else:
        nseq, seg = 8, 8
        step_rows = CHUNK_ROWS
        grid = (rows_total // CHUNK_ROWS,)
        rb = lambda i: i
        sb = lambda i: i
        pb = lambda i: 0
    wrap = lambda f, *tail: (lambda *g: (f(*g),) + tail)
    const = lambda *tail: (lambda *g: tail)
    q_col = (GDN_CONV_W + GDN_V_W) // RET_QK_W
    v_col = (GDN_CONV_W + GDN_V_W + 2 * RET_QK_W) // RET_V_W
    in_specs = [
        pl.BlockSpec((step_rows, RET_QK_W), wrap(rb, q_col)),
        pl.BlockSpec((step_rows, RET_QK_W), wrap(rb, q_col + 1)),
        pl.BlockSpec((step_rows, RET_V_W), wrap(rb, v_col)),
        pl.BlockSpec((step_rows, RET_V_W), wrap(rb, v_col + 1)),
        pl.BlockSpec((step_rows, RET_DK), wrap(pb, 0)),
        pl.BlockSpec((step_rows, RET_DK), wrap(pb, 0)),
        pl.BlockSpec((RET_HEADS, CHUNK_ROWS, CHUNK_ROWS), const(0, 0, 0)),
        pl.BlockSpec((RET_HEADS, CHUNK_ROWS, RET_DK), const(0, 0, 0)),
        pl.BlockSpec((RET_HEADS, CHUNK_ROWS, RET_DK), const(0, 0, 0)),
        pl.BlockSpec((1, RET_V_W), const(0, 0)),
        pl.BlockSpec((1, RET_V_W), const(0, 0)),
    ]
    args = [proj, proj, proj, proj, cosf, sinf, dec, qd, kd, gw, gb]
    if not chained:
        in_specs.append(pl.BlockSpec((nseq, RET_HEADS, RET_DK, RET_DV), wrap(sb, 0, 0, 0)))
        args.append(state)
    out_specs = [
        pl.BlockSpec((step_rows, RET_V_W), wrap(rb, 0)),
        pl.BlockSpec((nseq, RET_HEADS, RET_DK, RET_DV), wrap(sb, 0, 0, 0)),
    ]
    out_shape = [
        jax.ShapeDtypeStruct((rows_total, RET_V_W), BF16),
        jax.ShapeDtypeStruct((nbatch, RET_HEADS, RET_DK, RET_DV), F32),
    ]
    return pl.pallas_call(
        functools.partial(_ret_kernel, nseq=nseq, seg=seg, chained=chained, g_chunk=g_chunk),
        grid=grid, in_specs=in_specs, out_specs=out_specs, out_shape=out_shape,
        compiler_params=_params(len(grid)),
        name="ret_chained" if chained else "ret_step",
    )(*args)


def _out_proj_kernel(x_ref, a_ref, b_ref, wa_ref, wb_ref, o_ref):
    o_ref[...] = x_ref[...] + (_dot(a_ref[...], wa_ref[...]) + _dot(b_ref[...], wb_ref[...]))


def _out_proj(x, o_a, o_b, w_a, w_b, tm=512):
    m = x.shape[0]
    return pl.pallas_call(
        _out_proj_kernel,
        grid=(m // tm,),
        in_specs=[
            pl.BlockSpec((tm, D_MODEL), lambda i: (i, 0)),
            pl.BlockSpec((tm, GDN_V_W), lambda i: (i, 0)),
            pl.BlockSpec((tm, RET_V_W), lambda i: (i, 0)),
            pl.BlockSpec((GDN_V_W, D_MODEL), lambda i: (0, 0)),
            pl.BlockSpec((RET_V_W, D_MODEL), lambda i: (0, 0)),
        ],
        out_specs=pl.BlockSpec((tm, D_MODEL), lambda i: (i, 0)),
        out_shape=jax.ShapeDtypeStruct((m, D_MODEL), F32),
        compiler_params=_params(1),
        name="out_proj",
    )(x, o_a, o_b, w_a, w_b)


FFN_TF = 512


def _w_up_prep_kernel(v_ref, g_ref, o_ref):
    o_ref[:, :FFN_TF] = v_ref[...].astype(BF16)
    o_ref[:, FFN_TF:] = g_ref[...].astype(BF16)


def _w_up_prep(w_up):
    depth = w_up.shape[0]
    nf = D_FF // FFN_TF
    return pl.pallas_call(
        _w_up_prep_kernel,
        grid=(depth, nf),
        in_specs=[pl.BlockSpec((None, D_MODEL, FFN_TF), lambda l, j: (l, 0, j)),
                  pl.BlockSpec((None, D_MODEL, FFN_TF), lambda l, j: (l, 0, nf + j))],
        out_specs=pl.BlockSpec((None, D_MODEL, 2 * FFN_TF), lambda l, j: (l, 0, j)),
        out_shape=jax.ShapeDtypeStruct(w_up.shape, BF16),
        compiler_params=_params(2),
        name="w_up_prep",
    )(w_up, w_up)


def _ffn_kernel(*refs, nseq, seg, tiles_per_seq, final_norm):
    (x_ref, nw_ref, wu_ref, cwv_ref, cwg_ref, bv_ref, bg_ref, wd_ref, bufv_ref, bufg_ref,
     fw_ref, o_ref, nbv_ref, nbg_ref, h_scr) = refs[:15]
    chained = tiles_per_seq > 1
    i = pl.program_id(0)
    j = pl.program_id(1)
    nf = pl.num_programs(1)
    rows = nseq * seg
    taps = FFN_CONV - 1

    @pl.when(j == 0)
    def _():
        xv = x_ref[...]
        h_scr[...] = _rms_rows(xv, nw_ref[...]).astype(BF16)
        o_ref[...] = xv

    up_both = _dot(h_scr[...], wu_ref[...])
    tf = up_both.shape[1] // 2

    def branch_chained(up, cw_ref, b_ref, buf_ref, nb_ref, carry_ref, up_scr):
        up_scr[8:8 + rows, :] = up
        up_scr[8 - taps:8, :] = jnp.where(i % tiles_per_seq == 0, buf_ref[0], carry_ref[j])
        tail = up[rows - taps:, :]
        nb_ref[0] = tail
        carry_ref[j] = tail
        out = up * cw_ref[taps:taps + 1, :] + b_ref[...]
        for k in range(1, FFN_CONV):
            out = out + up_scr[8 - k:8 - k + rows, :] * cw_ref[taps - k:taps - k + 1, :]
        return out

    def branch_short(up, cw_ref, b_ref, buf_ref, nb_ref):
        tpos = lax.broadcasted_iota(jnp.int32, (rows, 1), 0) & (seg - 1)
        buf = buf_ref[...]
        out = up * cw_ref[taps:taps + 1, :] + b_ref[...]
        for k in range(1, FFN_CONV):
            out = out + _shift_rows(up, k, tpos, buf, nseq, seg) * cw_ref[taps - k:taps - k + 1, :]
        nb_ref[...] = up.reshape(nseq, seg, up.shape[1])[:, seg - taps:, :]
        return out

    if chained:
        cv_scr, cg_scr, uv_scr, ug_scr = refs[15:19]
        val = branch_chained(up_both[:, :tf], cwv_ref, bv_ref, bufv_ref, nbv_ref, cv_scr, uv_scr)
        gate = branch_chained(up_both[:, tf:], cwg_ref, bg_ref, bufg_ref, nbg_ref, cg_scr, ug_scr)
    else:
        val = branch_short(up_both[:, :tf], cwv_ref, bv_ref, bufv_ref, nbv_ref)
        gate = branch_short(up_both[:, tf:], cwg_ref, bg_ref, bufg_ref, nbg_ref)
    act = (_silu(gate) * val).astype(BF16)
    o_ref[...] += _dot(act, wd_ref[...])

    if final_norm:
        @pl.when(j == nf - 1)
        def _():
            o_ref[...] = _rms_rows(o_ref[...], fw_ref[...])


def _ffn(x, nw, w_up, conv_w, conv_b, w_down, buf, final_w, *, layer, nseq_total, final_norm,
         x_time_major=False, out_time_major=False, tm=1024, tf=FFN_TF):
    m = x.shape[0] * x.shape[1] // D_MODEL
    seq_len = m // nseq_total
    if seq_len >= tm:
        nseq, seg, tiles_per_seq = 1, tm, seq_len // tm
    else:
        nseq, seg, tiles_per_seq = tm // seq_len, seq_len, 1
    assert tf == FFN_TF, "the up-projection weight is laid out in FFN_TF chunks"
    nf = D_FF // tf
    sidx = (lambda i, j: i // tiles_per_seq) if tiles_per_seq > 1 else (lambda i, j: i)
    seq_major = lambda i, j: (i, 0)
    time_major = lambda i, j: (i % tiles_per_seq, i // tiles_per_seq)
    once = pl.Buffered(1)
    in_specs = [
        pl.BlockSpec((tm, D_MODEL), time_major if x_time_major else seq_major),
        pl.BlockSpec((1, D_MODEL), lambda i, j: (0, 0)),
        pl.BlockSpec((None, D_MODEL, 2 * tf), lambda i, j: (layer, 0, j)),
        pl.BlockSpec((None, FFN_CONV, tf), lambda i, j: (layer, 0, j)),
        pl.BlockSpec((None, FFN_CONV, tf), lambda i, j: (layer, 0, nf + j)),
        pl.BlockSpec((None, 1, tf), lambda i, j: (layer, 0, j)),
        pl.BlockSpec((None, 1, tf), lambda i, j: (layer, 0, nf + j)),
        pl.BlockSpec((None, tf, D_MODEL), lambda i, j: (layer, j, 0)),
        pl.BlockSpec((None, nseq, FFN_CONV - 1, tf), lambda i, j: (layer, sidx(i, j), 0, j)),
        pl.BlockSpec((None, nseq, FFN_CONV - 1, tf), lambda i, j: (layer, sidx(i, j), 0, nf + j)),
        pl.BlockSpec((1, D_MODEL), lambda i, j: (0, 0)),
    ]
    out_specs = [
        pl.BlockSpec((tm, D_MODEL), time_major if out_time_major else seq_major, pipeline_mode=once),
        pl.BlockSpec((nseq, FFN_CONV - 1, tf), lambda i, j: (i, 0, j)),
        pl.BlockSpec((nseq, FFN_CONV - 1, tf), lambda i, j: (i, 0, j)),
    ]
    out_shape = [
        jax.ShapeDtypeStruct((seq_len, nseq_total * D_MODEL) if out_time_major else (m, D_MODEL), F32),
        jax.ShapeDtypeStruct((m // tm * nseq, FFN_CONV - 1, D_FF), F32),
        jax.ShapeDtypeStruct((m // tm * nseq, FFN_CONV - 1, D_FF), F32),
    ]
    scratch = [pltpu.VMEM((tm, D_MODEL), BF16)]
    if tiles_per_seq > 1:
        scratch += [pltpu.VMEM((nf, FFN_CONV - 1, tf), F32), pltpu.VMEM((nf, FFN_CONV - 1, tf), F32),
                    pltpu.VMEM((8 + tm, tf), F32), pltpu.VMEM((8 + tm, tf), F32)]
    y, nbv, nbg = pl.pallas_call(
        functools.partial(_ffn_kernel, nseq=nseq, seg=seg, tiles_per_seq=tiles_per_seq, final_norm=final_norm),
        grid=(m // tm, nf), in_specs=in_specs, out_specs=out_specs, out_shape=out_shape,
        scratch_shapes=scratch, compiler_params=_params(2),
        name="conv_ffn",
    )(x, nw, w_up, conv_w, conv_w, conv_b, conv_b, w_down, buf, buf, final_w)
    new_buf = jnp.concatenate([nbv, nbg], axis=-1)
    return y, new_buf[tiles_per_seq - 1::tiles_per_seq]


def _s5_disc_kernel(lre_ref, lim_ref, ldt_ref, bre_ref, bim_ref, are_ref, aim_ref, bbre_ref, bbim_ref):
    lam_re = lre_ref[...]
    lam_im = lim_ref[...]
    dt = jnp.exp(ldt_ref[...])
    mag = jnp.exp(lam_re * dt)
    ph = lam_im * dt
    ab_re = mag * jnp.cos(ph)
    ab_im = mag * jnp.sin(ph)
    den = lam_re * lam_re + lam_im * lam_im
    cf_re = ((ab_re - 1.0) * lam_re + ab_im * lam_im) / den
    cf_im = (ab_im * lam_re - (ab_re - 1.0) * lam_im) / den
    are_ref[...] = ab_re
    aim_ref[...] = ab_im
    b_re = bre_ref[...]
    b_im = bim_ref[...]
    bbre_ref[...] = cf_re * b_re - cf_im * b_im
    bbim_ref[...] = cf_re * b_im + cf_im * b_re


def _s5_disc(lam_re, lam_im, log_dt, b_re_t, b_im_t):
    g, c, p = b_re_t.shape
    tile = lambda t: jnp.tile(t, (1, c))
    shp = jax.ShapeDtypeStruct((g, c * p), F32)
    a_re, a_im, bb_re, bb_im = pl.pallas_call(
        _s5_disc_kernel, out_shape=[shp, shp, shp, shp], name="s5_disc",
    )(tile(lam_re), tile(lam_im), jnp.broadcast_to(log_dt, (g, c * p)),
      b_re_t.reshape(g, c * p), b_im_t.reshape(g, c * p))
    return a_re[:, :p], a_im[:, :p], bb_re.reshape(g, c, p), bb_im.reshape(g, c, p)


def _gelu(y):
    return y * (0.5 * (1.0 + jnp.tanh(GELU_C * (y + 0.044715 * (y * y * y)))))


def _s5_seq_kernel(x_ref, nw_ref, perm_ref, back_ref, wb_ref, wc_ref, are_ref, aim_ref, d_ref, h0r_ref, h0i_ref,
                   g_ref, hr_ref, hi_ref, st_scr, u_scr, ub_scr, x_scr, *, nseq, seg, chained):
    wrows = 2 * nseq * seg

    def load_state():
        st_scr[0:nseq, :] = h0r_ref[...]
        st_scr[nseq:2 * nseq, :] = h0i_ref[...]

    nw = nw_ref[...]
    if chained:
        pl.when(pl.program_id(0) == 0)(load_state)
        x = x_ref[...]
        u = jnp.concatenate([_rms_rows(x[:, s * D_MODEL:(s + 1) * D_MODEL], nw) for s in range(nseq)], axis=0)
    else:
        load_state()
        u = _rms_rows(x_ref[...], nw)
    u_hi = u.astype(BF16)
    u_lo = (u - u_hi.astype(F32)).astype(BF16)
    perm = perm_ref[...]
    cw = 512
    for c in range(D_MODEL // cw):
        csl = slice(c * cw, (c + 1) * cw)
        hi = _dot(perm, u_hi[:, csl])
        u_scr[:, csl] = hi + _dot(perm, u_lo[:, csl])
        ub_scr[:, csl] = hi.astype(BF16)

    row = lax.broadcasted_iota(jnp.int32, (wrows, 1), 0)
    is_re = (row & (2 * nseq - 1)) < nseq
    lpb = S5_BW // LANES
    tile = 2 * nseq
    sign = jnp.where(lax.broadcasted_iota(jnp.int32, (tile, 1), 0) < nseq, -1.0, 1.0)

    def project_in(kb):
        blk = ub_scr[:, kb * LANES:(kb + 1) * LANES]
        zero = jnp.zeros_like(blk)
        lhs = jnp.concatenate([jnp.where(is_re, blk, zero), jnp.where(is_re, zero, blk)], axis=1)
        res = _dot(lhs, wb_ref[kb])
        for q in range(lpb):
            x_scr[kb * lpb + q] = res[:, q * LANES:(q + 1) * LANES]

    def scan(kb):
        blocks = range(kb * lpb, (kb + 1) * lpb)
        lane = [slice(lb * LANES, (lb + 1) * LANES) for lb in blocks]
        a1 = [jnp.broadcast_to(are_ref[:, sl], (tile, LANES)) for sl in lane]
        a2 = [jnp.broadcast_to(aim_ref[:, sl], (tile, LANES)) * sign for sl in lane]
        st = [st_scr[:, sl] for sl in lane]
        for t in range(seg):
            for n, lb in enumerate(blocks):
                st[n] = a1[n] * st[n] + a2[n] * pltpu.roll(st[n], nseq, 0) + x_scr[lb, t * tile:(t + 1) * tile, :]
                x_scr[lb, t * tile:(t + 1) * tile, :] = st[n]
        for n, sl in enumerate(lane):
            st_scr[:, sl] = st[n]

    def read_out(kb):
        xb = jnp.concatenate([x_scr[kb * lpb + q] for q in range(lpb)], axis=1).astype(BF16)
        y2 = _dot(xb, wc_ref[kb])
        lsl = slice(kb * LANES, (kb + 1) * LANES)
        y = y2[:, :LANES] - pltpu.roll(y2[:, LANES:], wrows - nseq, 0) + d_ref[:, lsl] * u_scr[:, lsl]
        ub_scr[:, lsl] = _gelu(y).astype(BF16)

    for kb in range(S5_NB + 2):
        if kb < S5_NB:
            project_in(kb)
        if 1 <= kb <= S5_NB:
            scan(kb - 1)
        if kb >= 2:
            read_out(kb - 2)
    hr_ref[...] = st_scr[0:nseq, :]
    hi_ref[...] = st_scr[nseq:2 * nseq, :]

    back = back_ref[...]
    for c in range(D_MODEL // cw):
        csl = slice(c * cw, (c + 1) * cw)
        g_nat = _dot(back, ub_scr[:, csl])
        if chained:
            for s in range(nseq):
                g_ref[s, :, csl] = g_nat[s * seg:(s + 1) * seg].astype(BF16)
        else:
            g_ref[:, csl] = g_nat.astype(BF16)


def _s5_seq(x, nw, wb2, wc2, a_re, a_im, d, h0_re, h0_im, *, nseq, seg, chained):
    wrows = 2 * nseq * seg
    perm, back = _s5_row_perm(nseq, seg)
    const2 = lambda n: (0, 0)
    const3 = lambda n: (0, 0, 0)
    if chained:
        steps = x.shape[0] // seg
        nstate = nseq
        x_spec = pl.BlockSpec((seg, nseq * D_MODEL), lambda n: (n, 0))
        g_spec = pl.BlockSpec((nseq, seg, D_MODEL), lambda n: (0, n, 0))
        g_shape = (nseq, x.shape[0], D_MODEL)
        s_spec = pl.BlockSpec((nseq, S5_STATE), const2)
    else:
        steps = x.shape[0] // (nseq * seg)
        nstate = steps * nseq
        x_spec = pl.BlockSpec((nseq * seg, D_MODEL), lambda n: (n, 0))
        g_spec = x_spec
        g_shape = x.shape
        s_spec = pl.BlockSpec((nseq, S5_STATE), lambda n: (n, 0))
    return pl.pallas_call(
        functools.partial(_s5_seq_kernel, nseq=nseq, seg=seg, chained=chained),
        grid=(steps,),
        in_specs=[
            x_spec,
            pl.BlockSpec((1, D_MODEL), const2),
            pl.BlockSpec((wrows, nseq * seg), const2),
            pl.BlockSpec((nseq * seg, wrows), const2),
            pl.BlockSpec((S5_NB, 2 * LANES, S5_BW), const3),
            pl.BlockSpec((S5_NB, S5_BW, 2 * LANES), const3),
            pl.BlockSpec((1, S5_STATE), const2),
            pl.BlockSpec((1, S5_STATE), const2),
            pl.BlockSpec((1, D_MODEL), const2),
            s_spec,
            s_spec,
        ],
        out_specs=[g_spec, s_spec, s_spec],
        out_shape=[jax.ShapeDtypeStruct(g_shape, BF16),
                   jax.ShapeDtypeStruct((nstate, S5_STATE), F32),
                   jax.ShapeDtypeStruct((nstate, S5_STATE), F32)],
        scratch_shapes=[pltpu.VMEM((2 * nseq, S5_STATE), F32),
                        pltpu.VMEM((wrows, D_MODEL), F32),
                        pltpu.VMEM((wrows, D_MODEL), BF16),
                        pltpu.VMEM((S5_STATE // LANES, wrows, LANES), F32)],
        compiler_params=_params(1),
        name="s5_seq" if chained else "s5_step",
    )(x, nw, perm, back, wb2, wc2, a_re, a_im, d, h0_re, h0_im)


def _s5_row_perm(nseq, seg):
    r = np.arange(2 * nseq * seg)
    src = (r % nseq) * seg + r // (2 * nseq)
    hit = src[:, None] == np.arange(nseq * seg)[None, :]
    real = (r % (2 * nseq)) < nseq
    return jnp.asarray(hit, BF16), jnp.asarray((hit & real[:, None]).T, BF16)


def _glu_kernel(x_ref, g_ref, wv_ref, wg_ref, o_ref):
    gb = g_ref[...]
    o_ref[...] = x_ref[...] + _dot(gb, wv_ref[...]) * _sigmoid(_dot(gb, wg_ref[...]))


def _glu(x, g, w_glu, *, time_major_seqs=0, tm=1024, tn=512):
    nn = D_MODEL // tn
    if time_major_seqs:
        tps = x.shape[0] // tm
        n_tiles = tps * time_major_seqs
        xmap = lambda i, j: (i % tps, (i // tps) * nn + j)
    else:
        n_tiles = x.shape[0] // tm
        xmap = lambda i, j: (i, j)
    return pl.pallas_call(
        _glu_kernel,
        grid=(n_tiles, nn),
        in_specs=[
            pl.BlockSpec((tm, tn), xmap),
            pl.BlockSpec((tm, D_MODEL), lambda i, j: (i, 0)),
            pl.BlockSpec((D_MODEL, tn), lambda i, j: (0, j)),
            pl.BlockSpec((D_MODEL, tn), lambda i, j: (0, nn + j)),
        ],
        out_specs=pl.BlockSpec((tm, tn), xmap),
        out_shape=jax.ShapeDtypeStruct(x.shape, F32),
        compiler_params=_params(2),
        name="glu",
    )(x, g, w_glu, w_glu)


def _rotary_tables(pos):
    half = RET_DK // 2
    inv = ROPE_BASE ** (-jnp.arange(half, dtype=F32) / half)
    ang = pos[:, None] * inv[None, :]
    cos = jnp.cos(ang)
    sin = jnp.sin(ang)
    return jnp.concatenate([cos, cos], axis=-1), jnp.concatenate([-sin, sin], axis=-1)


def _retention_tables(nseq, seg):
    log_g = np.log1p(-np.exp2(-5.0 - np.arange(RET_HEADS)))
    row = np.arange(nseq * seg)
    t = (row % seg).astype(np.float64)
    sid = row // seg
    diff = t[:, None] - t[None, :]
    ok = (sid[:, None] == sid[None, :]) & (diff >= 0)
    dec = np.where(ok, np.exp(log_g[:, None, None] * np.where(ok, diff, 0.0)), 0.0)
    qd = np.broadcast_to(np.exp(log_g[:, None] * (t + 1.0))[..., None], (RET_HEADS, nseq * seg, RET_DK))
    kd = np.broadcast_to(np.exp(log_g[:, None] * (seg - 1.0 - t))[..., None], (RET_HEADS, nseq * seg, RET_DK))
    gains = tuple(float(v) for v in np.exp(log_g * seg))
    return jnp.asarray(dec, F32), jnp.asarray(qd, F32), jnp.asarray(kd, F32), gains


def _block_diag(blocks, rows_per, cols_per):
    tiled = jnp.tile(blocks.reshape(S5_NB, S5_GB * rows_per, cols_per), (1, 1, S5_GB))
    rg = np.arange(S5_GB * rows_per)[:, None] // rows_per
    cg = np.arange(S5_GB * cols_per)[None, :] // cols_per
    return (tiled * jnp.asarray(rg == cg, F32)).astype(BF16)


def _trunk(x3, pos, gdn_s, gdn_cb, ret_s, s5_re, s5_im, ffn_cb, prm, *, chained):
    b, l, _ = x3.shape
    m = b * l
    x = x3.reshape(m, D_MODEL)
    nchunk = l // CHUNK_ROWS if chained else 1

    proj, gate = _in_proj(x, prm['norm_mix_w'][0:1], prm['w_in_main'], prm['w_in_gate'])
    o_a, gdn_new, gcb_new = _gdn(proj, gate, gdn_cb, prm['gdn_conv_w'], prm['alog_row'], prm['dtb_row'],
                                 prm['gdn_norm_w'], None if chained else gdn_s, nbatch=b, nchunk=nchunk)
    if chained:
        cosf, sinf = _rotary_tables(pos)
        dec, qd, kd, g_chunk = _retention_tables(1, CHUNK_ROWS)
    else:
        cosf, sinf = _rotary_tables(jnp.tile(pos, CHUNK_ROWS // l))
        dec, qd, kd, g_chunk = _retention_tables(CHUNK_ROWS // l, l)
    o_b, ret_new = _ret(proj, cosf, sinf, dec, qd, kd, prm['ret_gn_w'], prm['ret_gn_b'],
                        None if chained else ret_s, g_chunk, nbatch=b, nchunk=nchunk)
    x = _out_proj(x, o_a, o_b, prm['w_out_a'], prm['w_out_b'])
    x, fcb0 = _ffn(x, prm['norm_ffn_w'][0:1], prm['w_up'], prm['ffn_conv_w'], prm['ffn_conv_b'],
                   prm['w_down'], ffn_cb, prm['norm_final_w'], layer=0, nseq_total=b, final_norm=False,
                   out_time_major=chained)

    s5_args = (prm['norm_mix_w'][1:2], prm['s5_wb2'], prm['s5_wc2'], prm['s5_a_re'], prm['s5_a_im'],
               prm['s5_d'], s5_re, s5_im)
    if chained:
        g, s5r_new, s5i_new = _s5_seq(x, *s5_args, nseq=b, seg=CHUNK_ROWS, chained=True)
        x = _glu(x, g.reshape(m, D_MODEL), prm['w_glu'], time_major_seqs=b)
    else:
        g, s5r_new, s5i_new = _s5_seq(x, *s5_args, nseq=CHUNK_ROWS // l, seg=l, chained=False)
        x = _glu(x, g, prm['w_glu'])
    y, fcb1 = _ffn(x, prm['norm_ffn_w'][1:2], prm['w_up'], prm['ffn_conv_w'], prm['ffn_conv_b'],
                   prm['w_down'], ffn_cb, prm['norm_final_w'], layer=1, nseq_total=b, final_norm=True,
                   x_time_major=chained)
    return (y.reshape(b, l, D_MODEL), gdn_new[None], gcb_new[None], ret_new[None],
            s5r_new.reshape(1, b, S5_GROUPS, S5_P), s5i_new.reshape(1, b, S5_GROUPS, S5_P),
            jnp.stack([fcb0, fcb1]))


def kernel(x_prompt, x_sample, state_gdn, state_gdn_conv, state_ret, state_s5_re, state_s5_im,
           state_ffn_conv, norm_mix_w, norm_ffn_w, norm_final_w,
           w_in, gdn_conv_w, gdn_a_log, gdn_dt_bias, gdn_norm_w, ret_gn_w, ret_gn_b, w_out,
           s5_lam_re, s5_lam_im, s5_log_dt, s5_b_re, s5_b_im, s5_c_re, s5_c_im, s5_d, w_glu,
           w_up, ffn_conv_w, ffn_conv_b, w_down):
    bp, lp, _ = x_prompt.shape
    bs, ls, _ = x_sample.shape
    past_len = 16384

    w_in_main, w_in_gate = _w_in_prep(w_in)
    pad_row = lambda v: jnp.pad(v.reshape(1, -1), ((0, 0), (0, GATE_W - v.shape[-1])))

    a_re, a_im, bb_re, bb_im = _s5_disc(s5_lam_re[0], s5_lam_im[0], s5_log_dt[0].reshape(S5_GROUPS, 1),
                                        s5_b_re[0].transpose(0, 2, 1), s5_b_im[0].transpose(0, 2, 1))
    s5_wb2 = jnp.concatenate([_block_diag(bb_re, S5_GROUP, S5_P),
                              _block_diag(bb_im, S5_GROUP, S5_P)], axis=1)
    s5_wc2 = jnp.concatenate([_block_diag(s5_c_re[0].transpose(0, 2, 1), S5_P, S5_GROUP),
                              _block_diag(s5_c_im[0].transpose(0, 2, 1), S5_P, S5_GROUP)], axis=-1)

    prm = dict(
        norm_mix_w=norm_mix_w, norm_ffn_w=norm_ffn_w, norm_final_w=norm_final_w.reshape(1, D_MODEL),
        w_in_main=w_in_main, w_in_gate=w_in_gate,
        gdn_conv_w=gdn_conv_w[0], alog_row=pad_row(gdn_a_log[0]), dtb_row=pad_row(gdn_dt_bias[0]),
        gdn_norm_w=gdn_norm_w[0].reshape(1, GDN_DV),
        ret_gn_w=ret_gn_w[0].reshape(1, RET_V_W), ret_gn_b=ret_gn_b[0].reshape(1, RET_V_W),
        w_out_a=w_out[0, :GDN_V_W].astype(BF16), w_out_b=w_out[0, GDN_V_W:].astype(BF16),
        s5_wb2=s5_wb2, s5_wc2=s5_wc2,
        s5_a_re=a_re.reshape(1, S5_STATE), s5_a_im=a_im.reshape(1, S5_STATE), s5_d=s5_d[0].reshape(1, D_MODEL),
        w_glu=w_glu[0].astype(BF16),
        w_up=_w_up_prep(w_up), ffn_conv_w=ffn_conv_w, ffn_conv_b=ffn_conv_b[:, None, :],
        w_down=w_down.astype(BF16),
    )

    z_gcb = jnp.zeros((bp, GDN_CONV - 1, GDN_CONV_W), F32)
    z_s5 = jnp.zeros((bp, S5_STATE), F32)
    z_fcb = jnp.zeros((2, bp, FFN_CONV - 1, 2 * D_FF), F32)
    pos_p = jnp.arange(lp, dtype=F32)
    pos_s = past_len + jnp.arange(ls, dtype=F32)

    y_p, gdn_p, gcb_p, ret_p, s5r_p, s5i_p, fcb_p = _trunk(
        x_prompt, pos_p, None, z_gcb, None, z_s5, z_s5, z_fcb, prm, chained=True)
    y_s, gdn_s, gcb_s, ret_s, s5r_s, s5i_s, fcb_s = _trunk(
        x_sample, pos_s, state_gdn[0], state_gdn_conv[0], state_ret[0],
        state_s5_re[0].reshape(bs, S5_STATE), state_s5_im[0].reshape(bs, S5_STATE),
        state_ffn_conv, prm, chained=False)
    return (y_p, y_s, gdn_p, gdn_s, gcb_p, gcb_s, ret_p, ret_s,
            s5r_p, s5r_s, s5i_p, s5i_s, fcb_p, fcb_s)
```

```python
import functools
import math

import jax
import jax.numpy as jnp
import numpy as np
from jax import lax
from jax.experimental import pallas as pl
from jax.experimental.pallas import tpu as pltpu

F32 = jnp.float32
BF16 = jnp.bfloat16

D_MODEL = 2048
GDN_HEADS = 8
GDN_DK = 128
GDN_DV = 128
GDN_CONV = 4
RET_HEADS = 4
RET_DK = 128
RET_DV = 256
ROPE_BASE = 10000.0
S5_GROUP = 16
S5_GROUPS = 128
S5_P = 64
S5_STATE = S5_GROUPS * S5_P
D_FF = 5632
FFN_CONV = 3
EPS = 1e-6

GDN_QK_W = GDN_HEADS * GDN_DK
GDN_V_W = GDN_HEADS * GDN_DV
GDN_CONV_W = 2 * GDN_QK_W + GDN_V_W
RET_QK_W = RET_HEADS * RET_DK
RET_V_W = RET_HEADS * RET_DV
PROJ_W = GDN_CONV_W + GDN_V_W + 2 * RET_QK_W + 2 * RET_V_W
GATE_W = 128

CHUNK_ROWS = 64
GDN_GROUP = 4
GDN_STEP_CHUNKS = 2
GDN_DIAG = 16
RET_STEP_CHUNKS = 4
LANES = 128
VMEM_LIMIT = 56 * 1024 * 1024

S5_GB = 8
S5_NB = S5_GROUPS // S5_GB
S5_BW = S5_GB * S5_P
GELU_C = math.sqrt(2.0 / math.pi)


def _params(n_grid):
    return pltpu.CompilerParams(dimension_semantics=("arbitrary",) * n_grid,
                                vmem_limit_bytes=VMEM_LIMIT)


def _dot(a, b):
    return jnp.dot(a, b, preferred_element_type=F32)


def _dot_nt(a, b):
    return lax.dot_general(a, b, (((1,), (1,)), ((), ())), preferred_element_type=F32)


def _dot_tn(a, b):
    return lax.dot_general(a, b, (((0,), (0,)), ((), ())), preferred_element_type=F32)


def _sigmoid(x):
    return 0.5 + 0.5 * jnp.tanh(0.5 * x)


def _silu(x):
    h = 0.5 * x
    return h + h * jnp.tanh(h)


def _rms_rows(x, w):
    return x * lax.rsqrt(jnp.mean(x * x, axis=-1, keepdims=True) + EPS) * w


def _shift_rows(x, k, tpos, buf, nseq, seg):
    rows, width = x.shape
    nb = buf.shape[1]
    prev = pltpu.roll(x, k, 0)
    for t in range(k):
        src = buf[:, nb - k + t:nb - k + t + 1, :]
        srcb = jnp.broadcast_to(src, (nseq, seg, width)).reshape(rows, width)
        prev = jnp.where(tpos == t, srcb, prev)
    return prev


def _w_in_prep_kernel(w_ref, main_ref, gate_ref):
    o_gate = GDN_CONV_W + GDN_V_W
    o_rest = o_gate + 2 * GDN_HEADS
    main_ref[:, :o_gate] = w_ref[:, :o_gate].astype(BF16)
    main_ref[:, o_gate:] = w_ref[:, o_rest:].astype(BF16)
    lane = lax.broadcasted_iota(jnp.int32, (1, GATE_W), 1)
    gate_ref[...] = jnp.where(lane < 2 * GDN_HEADS, w_ref[:, o_gate:o_gate + GATE_W], 0.0).astype(BF16)


def _w_in_prep(w, tr=256):
    _, k, n = w.shape
    return pl.pallas_call(
        _w_in_prep_kernel,
        grid=(k // tr,),
        in_specs=[pl.BlockSpec((None, tr, n), lambda i: (0, i, 0))],
        out_specs=[pl.BlockSpec((tr, PROJ_W), lambda i: (i, 0)),
                   pl.BlockSpec((tr, GATE_W), lambda i: (i, 0))],
        out_shape=[jax.ShapeDtypeStruct((k, PROJ_W), BF16), jax.ShapeDtypeStruct((k, GATE_W), BF16)],
        compiler_params=_params(1),
        name="w_in_prep",
    )(w)


def _in_proj_kernel(x_ref, nw_ref, w_ref, wg_ref, o_ref, og_ref, h_scr):
    @pl.when(pl.program_id(1) == 0)
    def _():
        hb = _rms_rows(x_ref[...], nw_ref[...]).astype(BF16)
        h_scr[...] = hb
        og_ref[...] = _dot(hb, wg_ref[...])

    o_ref[...] = _dot(h_scr[...], w_ref[...])


IN_PROJ_ROWS = 2048
IN_PROJ_TILE = 2048 * 512


def _in_proj(x, nw, w_main, w_gate):
    m = x.shape[0]
    tm = min(IN_PROJ_ROWS, m)
    tn = IN_PROJ_TILE // tm
    return pl.pallas_call(
        _in_proj_kernel,
        grid=(m // tm, PROJ_W // tn),
        in_specs=[
            pl.BlockSpec((tm, D_MODEL), lambda i, j: (i, 0), pipeline_mode=pl.Buffered(1)),
            pl.BlockSpec((1, D_MODEL), lambda i, j: (0, 0)),
            pl.BlockSpec((D_MODEL, tn), lambda i, j: (0, j)),
            pl.BlockSpec((D_MODEL, GATE_W), lambda i, j: (0, 0)),
        ],
        out_specs=[
            pl.BlockSpec((tm, tn), lambda i, j: (i, j)),
            pl.BlockSpec((tm, GATE_W), lambda i, j: (i, 0)),
        ],
        out_shape=[jax.ShapeDtypeStruct((m, PROJ_W), F32),
                   jax.ShapeDtypeStruct((m, GATE_W), F32)],
        scratch_shapes=[pltpu.VMEM((tm, D_MODEL), BF16)],
        compiler_params=_params(2),
        name="in_proj",
    )(x, nw, w_main, w_gate)


def _gdn_kernel(*refs, nseq, seg, chained):
    if chained:
        (qkv_ref, z_ref, gate_ref, cbuf_ref, cw_ref, alog_ref, dtb_ref, nw_ref, sum_ref, mask_ref,
         o_ref, s_out_ref, cb_out_ref, carry_scr) = refs
        s_in_ref = s_out_ref
        first = pl.program_id(1) == 0

        @pl.when(first)
        def _():
            s_out_ref[...] = jnp.zeros_like(s_out_ref)
    else:
        (qkv_ref, z_ref, gate_ref, cbuf_ref, cw_ref, alog_ref, dtb_ref, nw_ref, sum_ref, mask_ref, s_in_ref,
         o_ref, s_out_ref, cb_out_ref) = refs

    rows = nseq * seg
    step_rows = qkv_ref.shape[0]
    cps = step_rows // rows
    shift = int(math.log2(seg))
    x = qkv_ref[...]
    if chained:
        buf = jnp.where(first, cbuf_ref[...], carry_scr[5:8, :].reshape(1, GDN_CONV - 1, GDN_CONV_W))
        conv_seg = step_rows
    else:
        buf = cbuf_ref[...]
        conv_seg = seg
    row_id = lax.broadcasted_iota(jnp.int32, (step_rows, 1), 0)
    tpos = row_id & (conv_seg - 1)

    acc = x * cw_ref[GDN_CONV - 1:GDN_CONV, :]
    for k in range(1, GDN_CONV):
        acc = acc + _shift_rows(x, k, tpos, buf, nseq, conv_seg) * cw_ref[GDN_CONV - 1 - k:GDN_CONV - k, :]
    act = _silu(acc)
    cb_out_ref[...] = x.reshape(nseq, conv_seg, GDN_CONV_W)[:, conv_seg - (GDN_CONV - 1):, :]
    if chained:
        carry_scr[...] = x[step_rows - 8:, :]

    gate = gate_ref[...]
    xa = gate + dtb_ref[...]
    softplus = jnp.maximum(xa, 0.0) + jnp.log(1.0 + jnp.exp(-jnp.abs(xa)))
    log_a = -jnp.exp(alog_ref[...]) * softplus
    beta_all = _sigmoid(gate)

    sum_m = sum_ref[...]
    a1 = log_a.astype(BF16)
    r1 = log_a - a1.astype(F32)
    a2 = r1.astype(BF16)
    a3 = (r1 - a2.astype(F32)).astype(BF16)
    if nseq > 1:
        row_masks = [((row_id >> shift) == s).astype(F32) for s in range(nseq)]
    nw = nw_ref[...]
    chunk_sl = [slice(c * rows, (c + 1) * rows) for c in range(cps)]
    g_all = [_dot(sum_m, a1[sl]) + (_dot(sum_m, a2[sl]) + _dot(sum_m, a3[sl])) for sl in chunk_sl]
    g = [t[:rows] for t in g_all]
    g_last = [t[rows:] for t in g_all]
    g2_t = [jnp.concatenate([t, t], axis=0).T for t in g]
    e_g = [jnp.exp(t) for t in g]
    e_rest = [jnp.exp(tl - t) for tl, t in zip(g_last, g)]
    e_last = [jnp.exp(t) for t in g_last]

    incl_f = mask_ref[0]
    strict_f = mask_ref[1]
    lane = lax.broadcasted_iota(jnp.int32, (1, LANES), 1)

    def stack_cols(tile, heads, off=0):
        return jnp.concatenate([tile[:, off + h:off + h + 1] for h in heads], axis=0)

    groups = [list(range(g0, g0 + GDN_GROUP)) for g0 in range(0, GDN_HEADS, GDN_GROUP)]
    units = [(c, heads) for c in range(cps) for heads in groups]
    ks, q_all, k_all, v_all = [], [], [], []
    for c, heads in units:
        a_c = act[chunk_sl[c]]
        qs, kn, vs = [], [], []
        for h in heads:
            lo = h * GDN_DK
            q = a_c[:, lo:lo + GDN_DK]
            k = a_c[:, GDN_QK_W + lo:GDN_QK_W + lo + GDN_DK]
            qs.append(q * lax.rsqrt(jnp.sum(q * q, axis=-1, keepdims=True) + EPS) * (GDN_DK ** -0.5))
            kn.append(k * lax.rsqrt(jnp.sum(k * k, axis=-1, keepdims=True) + EPS))
            vs.append(a_c[:, 2 * GDN_QK_W + lo:2 * GDN_QK_W + lo + GDN_DV])
        ks.append(kn)
        q_all.append(jnp.concatenate(qs, axis=0))
        k_all.append(jnp.concatenate(kn, axis=0))
        v_all.append(jnp.concatenate(vs, axis=0))
    gc = [stack_cols(g[c], heads) for c, heads in units]
    bc = [stack_cols(beta_all[chunk_sl[c]], heads, GDN_HEADS) for c, heads in units]
    egc = [stack_cols(e_g[c], heads) for c, heads in units]
    gr = [jnp.concatenate(
        [jnp.where(lane < rows, g2_t[c][heads[j]:heads[j] + 1, :], g2_t[c][heads[j + 1]:heads[j + 1] + 1, :])
         for j in range(0, GDN_GROUP, 2)], axis=1) for c, heads in units]
    decay = [jnp.exp(jnp.minimum(a - r, 0.0)) * incl_f for a, r in zip(gc, gr)]
    kb = [k.astype(BF16) for k in k_all]
    kk = [_dot_nt(k, k) for k in kb]
    qk = [_dot_nt(q.astype(BF16), k) * d for q, k, d in zip(q_all, kb, decay)]
    lms = [b * m * d * strict_f for b, m, d in zip(bc, kk, decay)]
    inv_e = _unit_lower_inverse_minus_eye(lms, mask_ref, seg)
    rhs = [jnp.concatenate([v * b, k * (b * e)], axis=-1) for v, k, b, e in zip(v_all, k_all, bc, egc)]
    sol = [r + _dot(e.astype(BF16), r.astype(BF16)) for e, r in zip(inv_e, rhs)]
    qd_all = [q * e for q, e in zip(q_all, egc)]

    states = [[s_in_ref[s, h] for s in range(nseq)] for h in range(GDN_HEADS)]
    for c in range(cps):
        unit_ids = [u for u, (uc, _) in enumerate(units) if uc == c]
        v_new, o_all = {}, {}
        for u in unit_ids:
            ws_parts, qs_parts = [], []
            for j, h in enumerate(units[u][1]):
                r0 = j * rows
                lhs = jnp.concatenate([sol[u][r0:r0 + rows, GDN_DV:], qd_all[u][r0:r0 + rows]], axis=0).astype(BF16)
                ws = None
                for s in range(nseq):
                    part = _dot(lhs, states[h][s].astype(BF16))
                    if nseq > 1:
                        part = part * jnp.concatenate([row_masks[s], row_masks[s]], axis=0)
                    ws = part if ws is None else ws + part
                ws_parts.append(ws[:rows])
                qs_parts.append(ws[rows:])
            v_new[u] = sol[u][:, :GDN_DV] - jnp.concatenate(ws_parts, axis=0)
            o_all[u] = jnp.concatenate(qs_parts, axis=0)
        for u in unit_ids:
            o_all[u] = o_all[u] + _dot(qk[u].astype(BF16), v_new[u].astype(BF16))
        for u in unit_ids:
            for j, h in enumerate(units[u][1]):
                r0 = j * rows
                lo = h * GDN_DK
                k_rest = (ks[u][j] * e_rest[c][:, h:h + 1]).astype(BF16)
                vn = v_new[u][r0:r0 + rows]
                for s in range(nseq):
                    vsel = vn * row_masks[s] if nseq > 1 else vn
                    states[h][s] = (states[h][s] * e_last[c][s * seg:s * seg + 1, h:h + 1]
                                    + _dot_tn(k_rest, vsel.astype(BF16)))
                o = o_all[u][r0:r0 + rows]
                o = o * lax.rsqrt(jnp.mean(o * o, axis=-1, keepdims=True) + EPS) * nw
                o = o * _silu(z_ref[chunk_sl[c], lo:lo + GDN_DV])
                o_ref[chunk_sl[c], lo:lo + GDN_DV] = o.astype(BF16)
    for h in range(GDN_HEADS):
        for s in range(nseq):
            s_out_ref[s, h] = states[h][s]


def _gdn_masks(nseq, seg):
    n = GDN_GROUP * nseq * seg
    ri, ci = np.arange(n)[:, None], np.arange(n)[None, :]
    same = (ri // seg) == (ci // seg)
    blk = min(GDN_DIAG, seg)
    masks = [same & (ci <= ri), same & (ci < ri), -1.0 * ((ri // blk) == (ci // blk))]
    size = blk
    while size < seg:
        masks.append(((ri // (2 * size)) == (ci // (2 * size))) & ((ri // size) != (ci // size)))
        size *= 2
    return jnp.asarray(np.stack([np.asarray(m, np.float32) for m in masks]), F32)


def _gdn_sum_matrix(nseq, seg):
    n = nseq * seg
    ri, ci = np.arange(n)[:, None], np.arange(n)[None, :]
    same = (ri // seg) == (ci // seg)
    return jnp.asarray(np.concatenate([same & (ci <= ri), same]), BF16)


def _unit_lower_inverse_minus_eye(lms, mask_ref, seg):
    blk = min(GDN_DIAG, seg)
    neg_diag = mask_ref[2]
    err = [lm * neg_diag for lm in lms]
    m_b = [e.astype(BF16) for e in err]
    for _ in range(int(math.log2(blk)) - 1):
        m_f = [_dot(m, m) for m in m_b]
        m_b = [m.astype(BF16) for m in m_f]
        err = [e + m + _dot(e.astype(BF16), mb) for e, m, mb in zip(err, m_f, m_b)]
    level = 3
    size = blk
    while size < seg:
        c = [lm * mask_ref[level] for lm in lms]
        e_b = [e.astype(BF16) for e in err]
        y = [x + _dot(e, x.astype(BF16)) for x, e in zip(c, e_b)]
        err = [e - (v + _dot(v.astype(BF16), eb)) for e, v, eb in zip(err, y, e_b)]
        level += 1
        size *= 2
    return err


def _gdn(proj, gate, cbuf, conv_w, alog_row, dtb_row, norm_w, state, *, nbatch, nchunk):
    chained = state is None
    rows_total = proj.shape[0]
    if chained:
        nseq, seg = 1, CHUNK_ROWS
        step_rows = GDN_STEP_CHUNKS * CHUNK_ROWS
        nstep = nchunk // GDN_STEP_CHUNKS
        grid = (nbatch, nstep)
        rb = lambda b, n: b * nstep + n
        sb = lambda b, n: b
    else:
        nseq, seg = 8, 8
        step_rows = CHUNK_ROWS
        grid = (rows_total // CHUNK_ROWS,)
        rb = lambda i: i
        sb = lambda i: i
    n_state = nbatch
    sum_m = _gdn_sum_matrix(nseq, seg)
    masks = _gdn_masks(nseq, seg)
    wrap = lambda f, *tail: (lambda *g: (f(*g),) + tail)
    const = lambda *tail: (lambda *g: tail)
    in_specs = [
        pl.BlockSpec((step_rows, GDN_CONV_W), wrap(rb, 0)),
        pl.BlockSpec((step_rows, GDN_V_W), wrap(rb, GDN_CONV_W // GDN_V_W)),
        pl.BlockSpec((step_rows, GATE_W), wrap(rb, 0)),
        pl.BlockSpec((nseq, GDN_CONV - 1, GDN_CONV_W), wrap(sb, 0, 0)),
        pl.BlockSpec((GDN_CONV, GDN_CONV_W), const(0, 0)),
        pl.BlockSpec((1, GATE_W), const(0, 0)),
        pl.BlockSpec((1, GATE_W), const(0, 0)),
        pl.BlockSpec((1, GDN_DV), const(0, 0)),
        pl.BlockSpec(sum_m.shape, const(0, 0)),
        pl.BlockSpec(masks.shape, const(0, 0, 0)),
    ]
    args = [proj, proj, gate, cbuf, conv_w, alog_row, dtb_row, norm_w, sum_m, masks]
    scratch = []
    if chained:
        scratch = [pltpu.VMEM((8, GDN_CONV_W), F32)]
    else:
        in_specs.append(pl.BlockSpec((nseq, GDN_HEADS, GDN_DK, GDN_DV), wrap(sb, 0, 0, 0)))
        args.append(state)
    out_specs = [
        pl.BlockSpec((step_rows, GDN_V_W), wrap(rb, 0)),
        pl.BlockSpec((nseq, GDN_HEADS, GDN_DK, GDN_DV), wrap(sb, 0, 0, 0)),
        pl.BlockSpec((nseq, GDN_CONV - 1, GDN_CONV_W), wrap(sb, 0, 0)),
    ]
    out_shape = [
        jax.ShapeDtypeStruct((rows_total, GDN_V_W), BF16),
        jax.ShapeDtypeStruct((n_state, GDN_HEADS, GDN_DK, GDN_DV), F32),
        jax.ShapeDtypeStruct((n_state, GDN_CONV - 1, GDN_CONV_W), F32),
    ]
    return pl.pallas_call(
        functools.partial(_gdn_kernel, nseq=nseq, seg=seg, chained=chained),
        grid=grid, in_specs=in_specs, out_specs=out_specs, out_shape=out_shape,
        scratch_shapes=scratch, compiler_params=_params(len(grid)),
        name="gdn_chained" if chained else "gdn_step",
    )(*args)


def _ret_kernel(*refs, nseq, seg, chained, g_chunk):
    if chained:
        (q_ref, k_ref, v_ref, gt_ref, cos_ref, sin_ref, dec_ref, qd_ref, kd_ref, gw_ref, gb_ref,
         o_ref, s_out_ref) = refs
        s_in_ref = s_out_ref

        @pl.when(pl.program_id(1) == 0)
        def _():
            s_out_ref[...] = jnp.zeros_like(s_out_ref)
    else:
        (q_ref, k_ref, v_ref, gt_ref, cos_ref, sin_ref, dec_ref, qd_ref, kd_ref, gw_ref, gb_ref,
         s_in_ref, o_ref, s_out_ref) = refs

    rows = nseq * seg
    cps = q_ref.shape[0] // rows
    shift = int(math.log2(seg))
    if nseq > 1:
        row_id = lax.broadcasted_iota(jnp.int32, (rows, 1), 0)
        row_masks = [((row_id >> shift) == s).astype(F32) for s in range(nseq)]

    units = [(c, h) for c in range(cps) for h in range(RET_HEADS)]
    rsl = lambda c: slice(c * rows, (c + 1) * rows)
    qr, kr = [], []
    for c, h in units:
        cosf = cos_ref[rsl(c), :]
        sinf = sin_ref[rsl(c), :]
        q = q_ref[rsl(c), h * RET_DK:(h + 1) * RET_DK]
        k = k_ref[rsl(c), h * RET_DK:(h + 1) * RET_DK]
        qr.append(q * cosf + pltpu.roll(q, RET_DK // 2, 1) * sinf)
        kr.append((k * cosf + pltpu.roll(k, RET_DK // 2, 1) * sinf) * (RET_DK ** -0.5))
    vf = [v_ref[rsl(c), h * RET_DV:(h + 1) * RET_DV] for c, h in units]
    vb = [v.astype(BF16) for v in vf]
    kb = [k.astype(BF16) for k in kr]
    sc = [_dot_nt(q.astype(BF16), k) * dec_ref[h] for q, k, (c, h) in zip(qr, kb, units)]
    o = [_dot(s.astype(BF16), v) for s, v in zip(sc, vb)]
    q_dec = [(q * qd_ref[h]).astype(BF16) for q, (c, h) in zip(qr, units)]
    k_dec = [(k * kd_ref[h]).astype(BF16) for k, (c, h) in zip(kr, units)]
    if nseq > 1:
        for u, (c, h) in enumerate(units):
            for s in range(nseq):
                state = s_in_ref[s, h]
                o[u] = o[u] + _dot(q_dec[u], state.astype(BF16)) * row_masks[s]
                s_out_ref[s, h] = state * g_chunk[h] + _dot_tn(k_dec[u], (vf[u] * row_masks[s]).astype(BF16))
    else:
        kv = [_dot_tn(k, v) for k, v in zip(k_dec, vb)]
        state = [s_in_ref[0, h] for h in range(RET_HEADS)]
        for u, (c, h) in enumerate(units):
            o[u] = o[u] + _dot(q_dec[u], state[h].astype(BF16))
            state[h] = state[h] * g_chunk[h] + kv[u]
        for h in range(RET_HEADS):
            s_out_ref[0, h] = state[h]
    for u, (c, h) in enumerate(units):
        vsl = slice(h * RET_DV, (h + 1) * RET_DV)
        mu = jnp.mean(o[u], axis=-1, keepdims=True)
        oc = o[u] - mu
        var = jnp.mean(oc * oc, axis=-1, keepdims=True)
        on = oc * lax.rsqrt(var + EPS)
        on = (on * gw_ref[:, vsl] + gb_ref[:, vsl]) * _silu(gt_ref[rsl(c), vsl])
        o_ref[rsl(c), vsl] = on.astype(BF16)


def _ret(proj, cosf, sinf, dec, qd, kd, gw, gb, state, g_chunk, *, nbatch, nchunk):
    chained = state is None
    rows_total = proj.shape[0]
    if chained:
        nseq, seg = 1, CHUNK_ROWS
        step_rows = RET_STEP_CHUNKS * CHUNK_ROWS
        nstep = nchunk // RET_STEP_CHUNKS
        grid = (nbatch, nstep)
        rb = lambda b, n: b * nstep + n
        sb = lambda b, n: b
        pb = lambda b, n: n
    else:
        nseq, seg = 8, 8
        step_rows = CHUNK_ROWS
        grid = (rows_total // CHUNK_ROWS,)
        rb = lambda i: i
        sb = lambda i: i
        pb = lambda i: 0
    wrap = lambda f, *tail: (lambda *g: (f(*g),) + tail)
    const = lambda *tail: (lambda *g: tail)
    q_col = (GDN_CONV_W + GDN_V_W) // RET_QK_W
    v_col = (GDN_CONV_W + GDN_V_W + 2 * RET_QK_W) // RET_V_W
    in_specs = [
        pl.BlockSpec((step_rows, RET_QK_W), wrap(rb, q_col)),
        pl.BlockSpec((step_rows, RET_QK_W), wrap(rb, q_col + 1)),
        pl.BlockSpec((step_rows, RET_V_W), wrap(rb, v_col)),
        pl.BlockSpec((step_rows, RET_V_W), wrap(rb, v_col + 1)),
        pl.BlockSpec((step_rows, RET_DK), wrap(pb, 0)),
        pl.BlockSpec((step_rows, RET_DK), wrap(pb, 0)),
        pl.BlockSpec((RET_HEADS, CHUNK_ROWS, CHUNK_ROWS), const(0, 0, 0)),
        pl.BlockSpec((RET_HEADS, CHUNK_ROWS, RET_DK), const(0, 0, 0)),
        pl.BlockSpec((RET_HEADS, CHUNK_ROWS, RET_DK), const(0, 0, 0)),
        pl.BlockSpec((1, RET_V_W), const(0, 0)),
        pl.BlockSpec((1, RET_V_W), const(0, 0)),
    ]
    args = [proj, proj, proj, proj, cosf, sinf, dec, qd, kd, gw, gb]
    if not chained:
        in_specs.append(pl.BlockSpec((nseq, RET_HEADS, RET_DK, RET_DV), wrap(sb, 0, 0, 0)))
        args.append(state)
    out_specs = [
        pl.BlockSpec((step_rows, RET_V_W), wrap(rb, 0)),
        pl.BlockSpec((nseq, RET_HEADS, RET_DK, RET_DV), wrap(sb, 0, 0, 0)),
    ]
    out_shape = [
        jax.ShapeDtypeStruct((rows_total, RET_V_W), BF16),
        jax.ShapeDtypeStruct((nbatch, RET_HEADS, RET_DK, RET_DV), F32),
    ]
    return pl.pallas_call(
        functools.partial(_ret_kernel, nseq=nseq, seg=seg, chained=chained, g_chunk=g_chunk),
        grid=grid, in_specs=in_specs, out_specs=out_specs, out_shape=out_shape,
        compiler_params=_params(len(grid)),
        name="ret_chained" if chained else "ret_step",
    )(*args)


def _out_proj_kernel(x_ref, a_ref, b_ref, wa_ref, wb_ref, o_ref):
    o_ref[...] = x_ref[...] + (_dot(a_ref[...], wa_ref[...]) + _dot(b_ref[...], wb_ref[...]))


def _out_proj(x, o_a, o_b, w_a, w_b, tm=512):
    m = x.shape[0]
    return pl.pallas_call(
        _out_proj_kernel,
        grid=(m // tm,),
        in_specs=[
            pl.BlockSpec((tm, D_MODEL), lambda i: (i, 0)),
            pl.BlockSpec((tm, GDN_V_W), lambda i: (i, 0)),
            pl.BlockSpec((tm, RET_V_W), lambda i: (i, 0)),
            pl.BlockSpec((GDN_V_W, D_MODEL), lambda i: (0, 0)),
            pl.BlockSpec((RET_V_W, D_MODEL), lambda i: (0, 0)),
        ],
        out_specs=pl.BlockSpec((tm, D_MODEL), lambda i: (i, 0)),
        out_shape=jax.ShapeDtypeStruct((m, D_MODEL), F32),
        compiler_params=_params(1),
        name="out_proj",
    )(x, o_a, o_b, w_a, w_b)


FFN_TF = 512


def _w_up_prep_kernel(v_ref, g_ref, o_ref):
    o_ref[:, :FFN_TF] = v_ref[...].astype(BF16)
    o_ref[:, FFN_TF:] = g_ref[...].astype(BF16)


def _w_up_prep(w_up):
    depth = w_up.shape[0]
    nf = D_FF // FFN_TF
    return pl.pallas_call(
        _w_up_prep_kernel,
        grid=(depth, nf),
        in_specs=[pl.BlockSpec((None, D_MODEL, FFN_TF), lambda l, j: (l, 0, j)),
                  pl.BlockSpec((None, D_MODEL, FFN_TF), lambda l, j: (l, 0, nf + j))],
        out_specs=pl.BlockSpec((None, D_MODEL, 2 * FFN_TF), lambda l, j: (l, 0, j)),
        out_shape=jax.ShapeDtypeStruct(w_up.shape, BF16),
        compiler_params=_params(2),
        name="w_up_prep",
    )(w_up, w_up)


def _ffn_kernel(*refs, nseq, seg, tiles_per_seq, final_norm):
    (x_ref, nw_ref, wu_ref, cwv_ref, cwg_ref, bv_ref, bg_ref, wd_ref, bufv_ref, bufg_ref,
     fw_ref, o_ref, nbv_ref, nbg_ref, h_scr) = refs[:15]
    chained = tiles_per_seq > 1
    i = pl.program_id(0)
    j = pl.program_id(1)
    nf = pl.num_programs(1)
    rows = nseq * seg
    taps = FFN_CONV - 1

    @pl.when(j == 0)
    def _():
        xv = x_ref[...]
        h_scr[...] = _rms_rows(xv, nw_ref[...]).astype(BF16)
        o_ref[...] = xv

    up_both = _dot(h_scr[...], wu_ref[...])
    tf = up_both.shape[1] // 2

    def branch_chained(up, cw_ref, b_ref, buf_ref, nb_ref, carry_ref, up_scr):
        up_scr[8:8 + rows, :] = up
        up_scr[8 - taps:8, :] = jnp.where(i % tiles_per_seq == 0, buf_ref[0], carry_ref[j])
        tail = up[rows - taps:, :]
        nb_ref[0] = tail
        carry_ref[j] = tail
        out = up * cw_ref[taps:taps + 1, :] + b_ref[...]
        for k in range(1, FFN_CONV):
            out = out + up_scr[8 - k:8 - k + rows, :] * cw_ref[taps - k:taps - k + 1, :]
        return out

    def branch_short(up, cw_ref, b_ref, buf_ref, nb_ref):
        tpos = lax.broadcasted_iota(jnp.int32, (rows, 1), 0) & (seg - 1)
        buf = buf_ref[...]
        out = up * cw_ref[taps:taps + 1, :] + b_ref[...]
        for k in range(1, FFN_CONV):
            out = out + _shift_rows(up, k, tpos, buf, nseq, seg) * cw_ref[taps - k:taps - k + 1, :]
        nb_ref[...] = up.reshape(nseq, seg, up.shape[1])[:, seg - taps:, :]
        return out

    if chained:
        cv_scr, cg_scr, uv_scr, ug_scr = refs[15:19]
        val = branch_chained(up_both[:, :tf], cwv_ref, bv_ref, bufv_ref, nbv_ref, cv_scr, uv_scr)
        gate = branch_chained(up_both[:, tf:], cwg_ref, bg_ref, bufg_ref, nbg_ref, cg_scr, ug_scr)
    else:
        val = branch_short(up_both[:, :tf], cwv_ref, bv_ref, bufv_ref, nbv_ref)
        gate = branch_short(up_both[:, tf:], cwg_ref, bg_ref, bufg_ref, nbg_ref)
    act = (_silu(gate) * val).astype(BF16)
    o_ref[...] += _dot(act, wd_ref[...].astype(BF16))

    if final_norm:
        @pl.when(j == nf - 1)
        def _():
            o_ref[...] = _rms_rows(o_ref[...], fw_ref[...])


def _ffn(x, nw, w_up, conv_w, conv_b, w_down, buf, final_w, *, layer, nseq_total, final_norm,
         x_time_major=False, out_time_major=False, tm=1024, tf=FFN_TF):
    m = x.shape[0] * x.shape[1] // D_MODEL
    seq_len = m // nseq_total
    if seq_len >= tm:
        nseq, seg, tiles_per_seq = 1, tm, seq_len // tm
    else:
        nseq, seg, tiles_per_seq = tm // seq_len, seq_len, 1
    assert tf == FFN_TF, "the up-projection weight is laid out in FFN_TF chunks"
    nf = D_FF // tf
    sidx = (lambda i, j: i // tiles_per_seq) if tiles_per_seq > 1 else (lambda i, j: i)
    seq_major = lambda i, j: (i, 0)
    time_major = lambda i, j: (i % tiles_per_seq, i // tiles_per_seq)
    once = pl.Buffered(1)
    in_specs = [
        pl.BlockSpec((tm, D_MODEL), time_major if x_time_major else seq_major),
        pl.BlockSpec((1, D_MODEL), lambda i, j: (0, 0)),
        pl.BlockSpec((None, D_MODEL, 2 * tf), lambda i, j: (layer, 0, j)),
        pl.BlockSpec((None, FFN_CONV, tf), lambda i, j: (layer, 0, j)),
        pl.BlockSpec((None, FFN_CONV, tf), lambda i, j: (layer, 0, nf + j)),
        pl.BlockSpec((None, 1, tf), lambda i, j: (layer, 0, j)),
        pl.BlockSpec((None, 1, tf), lambda i, j: (layer, 0, nf + j)),
        pl.BlockSpec((None, tf, D_MODEL), lambda i, j: (layer, j, 0)),
        pl.BlockSpec((None, nseq, FFN_CONV - 1, tf), lambda i, j: (layer, sidx(i, j), 0, j)),
        pl.BlockSpec((None, nseq, FFN_CONV - 1, tf), lambda i, j: (layer, sidx(i, j), 0, nf + j)),
        pl.BlockSpec((1, D_MODEL), lambda i, j: (0, 0)),
    ]
    out_specs = [
        pl.BlockSpec((tm, D_MODEL), time_major if out_time_major else seq_major, pipeline_mode=once),
        pl.BlockSpec((nseq, FFN_CONV - 1, tf), lambda i, j: (i, 0, j)),
        pl.BlockSpec((nseq, FFN_CONV - 1, tf), lambda i, j: (i, 0, j)),
    ]
    out_shape = [
        jax.ShapeDtypeStruct((seq_len, nseq_total * D_MODEL) if out_time_major else (m, D_MODEL), F32),
        jax.ShapeDtypeStruct((m // tm * nseq, FFN_CONV - 1, D_FF), F32),
        jax.ShapeDtypeStruct((m // tm * nseq, FFN_CONV - 1, D_FF), F32),
    ]
    scratch = [pltpu.VMEM((tm, D_MODEL), BF16)]
    if tiles_per_seq > 1:
        scratch += [pltpu.VMEM((nf, FFN_CONV - 1, tf), F32), pltpu.VMEM((nf, FFN_CONV - 1, tf), F32),
                    pltpu.VMEM((8 + tm, tf), F32), pltpu.VMEM((8 + tm, tf), F32)]
    y, nbv, nbg = pl.pallas_call(
        functools.partial(_ffn_kernel, nseq=nseq, seg=seg, tiles_per_seq=tiles_per_seq, final_norm=final_norm),
        grid=(m // tm, nf), in_specs=in_specs, out_specs=out_specs, out_shape=out_shape,
        scratch_shapes=scratch, compiler_params=_params(2),
        name="conv_ffn",
    )(x, nw, w_up, conv_w, conv_w, conv_b, conv_b, w_down, buf, buf, final_w)
    new_buf = jnp.concatenate([nbv, nbg], axis=-1)
    return y, new_buf[tiles_per_seq - 1::tiles_per_seq]


def _s5_disc_kernel(lre_ref, lim_ref, ldt_ref, bre_ref, bim_ref, are_ref, aim_ref, bbre_ref, bbim_ref):
    lam_re = lre_ref[...]
    lam_im = lim_ref[...]
    dt = jnp.exp(ldt_ref[...])
    mag = jnp.exp(lam_re * dt)
    ph = lam_im * dt
    ab_re = mag * jnp.cos(ph)
    ab_im = mag * jnp.sin(ph)
    den = lam_re * lam_re + lam_im * lam_im
    cf_re = ((ab_re - 1.0) * lam_re + ab_im * lam_im) / den
    cf_im = (ab_im * lam_re - (ab_re - 1.0) * lam_im) / den
    are_ref[...] = ab_re
    aim_ref[...] = ab_im
    b_re = bre_ref[...]
    b_im = bim_ref[...]
    bbre_ref[...] = cf_re * b_re - cf_im * b_im
    bbim_ref[...] = cf_re * b_im + cf_im * b_re


def _s5_disc(lam_re, lam_im, log_dt, b_re_t, b_im_t):
    g, c, p = b_re_t.shape
    tile = lambda t: jnp.tile(t, (1, c))
    shp = jax.ShapeDtypeStruct((g, c * p), F32)
    a_re, a_im, bb_re, bb_im = pl.pallas_call(
        _s5_disc_kernel, out_shape=[shp, shp, shp, shp], name="s5_disc",
    )(tile(lam_re), tile(lam_im), jnp.broadcast_to(log_dt, (g, c * p)),
      b_re_t.reshape(g, c * p), b_im_t.reshape(g, c * p))
    return a_re[:, :p], a_im[:, :p], bb_re.reshape(g, c, p), bb_im.reshape(g, c, p)


def _gelu(y):
    return y * (0.5 * (1.0 + jnp.tanh(GELU_C * (y + 0.044715 * (y * y * y)))))


def _s5_seq_kernel(x_ref, nw_ref, perm_ref, back_ref, wb_ref, wc_ref, are_ref, aim_ref, d_ref, h0r_ref, h0i_ref,
                   g_ref, hr_ref, hi_ref, st_scr, u_scr, ub_scr, x_scr, *, nseq, seg, chained):
    wrows = 2 * nseq * seg

    def load_state():
        st_scr[0:nseq, :] = h0r_ref[...]
        st_scr[nseq:2 * nseq, :] = h0i_ref[...]

    nw = nw_ref[...]
    if chained:
        pl.when(pl.program_id(0) == 0)(load_state)
        x = x_ref[...]
        u = jnp.concatenate([_rms_rows(x[:, s * D_MODEL:(s + 1) * D_MODEL], nw) for s in range(nseq)], axis=0)
    else:
        load_state()
        u = _rms_rows(x_ref[...], nw)
    u_hi = u.astype(BF16)
    u_lo = (u - u_hi.astype(F32)).astype(BF16)
    perm = perm_ref[...]
    cw = 512
    for c in range(D_MODEL // cw):
        csl = slice(c * cw, (c + 1) * cw)
        hi = _dot(perm, u_hi[:, csl])
        u_scr[:, csl] = hi + _dot(perm, u_lo[:, csl])
        ub_scr[:, csl] = hi.astype(BF16)

    row = lax.broadcasted_iota(jnp.int32, (wrows, 1), 0)
    is_re = (row & (2 * nseq - 1)) < nseq
    lpb = S5_BW // LANES
    tile = 2 * nseq
    sign = jnp.where(lax.broadcasted_iota(jnp.int32, (tile, 1), 0) < nseq, -1.0, 1.0)

    def project_in(kb):
        blk = ub_scr[:, kb * LANES:(kb + 1) * LANES]
        zero = jnp.zeros_like(blk)
        lhs = jnp.concatenate([jnp.where(is_re, blk, zero), jnp.where(is_re, zero, blk)], axis=1)
        res = _dot(lhs, wb_ref[kb])
        for q in range(lpb):
            x_scr[kb * lpb + q] = res[:, q * LANES:(q + 1) * LANES]

    def scan(kb):
        blocks = range(kb * lpb, (kb + 1) * lpb)
        lane = [slice(lb * LANES, (lb + 1) * LANES) for lb in blocks]
        a1 = [jnp.broadcast_to(are_ref[:, sl], (tile, LANES)) for sl in lane]
        a2 = [jnp.broadcast_to(aim_ref[:, sl], (tile, LANES)) * sign for sl in lane]
        st = [st_scr[:, sl] for sl in lane]
        for t in range(seg):
            for n, lb in enumerate(blocks):
                st[n] = a1[n] * st[n] + a2[n] * pltpu.roll(st[n], nseq, 0) + x_scr[lb, t * tile:(t + 1) * tile, :]
                x_scr[lb, t * tile:(t + 1) * tile, :] = st[n]
        for n, sl in enumerate(lane):
            st_scr[:, sl] = st[n]

    def read_out(kb):
        xb = jnp.concatenate([x_scr[kb * lpb + q] for q in range(lpb)], axis=1).astype(BF16)
        y2 = _dot(xb, wc_ref[kb])
        lsl = slice(kb * LANES, (kb + 1) * LANES)
        y = y2[:, :LANES] - pltpu.roll(y2[:, LANES:], wrows - nseq, 0) + d_ref[:, lsl] * u_scr[:, lsl]
        ub_scr[:, lsl] = _gelu(y).astype(BF16)

    for kb in range(S5_NB + 2):
        if kb < S5_NB:
            project_in(kb)
        if 1 <= kb <= S5_NB:
            scan(kb - 1)
        if kb >= 2:
            read_out(kb - 2)
    hr_ref[...] = st_scr[0:nseq, :]
    hi_ref[...] = st_scr[nseq:2 * nseq, :]

    back = back_ref[...]
    for c in range(D_MODEL // cw):
        csl = slice(c * cw, (c + 1) * cw)
        g_nat = _dot(back, ub_scr[:, csl])
        if chained:
            for s in range(nseq):
                g_ref[s, :, csl] = g_nat[s * seg:(s + 1) * seg].astype(BF16)
        else:
            g_ref[:, csl] = g_nat.astype(BF16)


def _s5_seq(x, nw, wb2, wc2, a_re, a_im, d, h0_re, h0_im, *, nseq, seg, chained):
    wrows = 2 * nseq * seg
    perm, back = _s5_row_perm(nseq, seg)
    const2 = lambda n: (0, 0)
    const3 = lambda n: (0, 0, 0)
    if chained:
        steps = x.shape[0] // seg
        nstate = nseq
        x_spec = pl.BlockSpec((seg, nseq * D_MODEL), lambda n: (n, 0))
        g_spec = pl.BlockSpec((nseq, seg, D_MODEL), lambda n: (0, n, 0))
        g_shape = (nseq, x.shape[0], D_MODEL)
        s_spec = pl.BlockSpec((nseq, S5_STATE), const2)
    else:
        steps = x.shape[0] // (nseq * seg)
        nstate = steps * nseq
        x_spec = pl.BlockSpec((nseq * seg, D_MODEL), lambda n: (n, 0))
        g_spec = x_spec
        g_shape = x.shape
        s_spec = pl.BlockSpec((nseq, S5_STATE), lambda n: (n, 0))
    return pl.pallas_call(
        functools.partial(_s5_seq_kernel, nseq=nseq, seg=seg, chained=chained),
        grid=(steps,),
        in_specs=[
            x_spec,
            pl.BlockSpec((1, D_MODEL), const2),
            pl.BlockSpec((wrows, nseq * seg), const2),
            pl.BlockSpec((nseq * seg, wrows), const2),
            pl.BlockSpec((S5_NB, 2 * LANES, S5_BW), const3),
            pl.BlockSpec((S5_NB, S5_BW, 2 * LANES), const3),
            pl.BlockSpec((1, S5_STATE), const2),
            pl.BlockSpec((1, S5_STATE), const2),
            pl.BlockSpec((1, D_MODEL), const2),
            s_spec,
            s_spec,
        ],
        out_specs=[g_spec, s_spec, s_spec],
        out_shape=[jax.ShapeDtypeStruct(g_shape, BF16),
                   jax.ShapeDtypeStruct((nstate, S5_STATE), F32),
                   jax.ShapeDtypeStruct((nstate, S5_STATE), F32)],
        scratch_shapes=[pltpu.VMEM((2 * nseq, S5_STATE), F32),
                        pltpu.VMEM((wrows, D_MODEL), F32),
                        pltpu.VMEM((wrows, D_MODEL), BF16),
                        pltpu.VMEM((S5_STATE // LANES, wrows, LANES), F32)],
        compiler_params=_params(1),
        name="s5_seq" if chained else "s5_step",
    )(x, nw, perm, back, wb2, wc2, a_re, a_im, d, h0_re, h0_im)


def _s5_row_perm(nseq, seg):
    r = np.arange(2 * nseq * seg)
    src = (r % nseq) * seg + r // (2 * nseq)
    hit = src[:, None] == np.arange(nseq * seg)[None, :]
    real = (r % (2 * nseq)) < nseq
    return jnp.asarray(hit, BF16), jnp.asarray((hit & real[:, None]).T, BF16)


def _glu_kernel(x_ref, g_ref, wv_ref, wg_ref, o_ref):
    gb = g_ref[...]
    val = _dot(gb, wv_ref[...].astype(BF16))
    o_ref[...] = x_ref[...] + val * _sigmoid(_dot(gb, wg_ref[...].astype(BF16)))


def _glu(x, g, w_glu, *, time_major_seqs=0, tm=1024, tn=512):
    nn = D_MODEL // tn
    if time_major_seqs:
        tps = x.shape[0] // tm
        n_tiles = tps * time_major_seqs
        xmap = lambda i, j: (i % tps, (i // tps) * nn + j)
    else:
        n_tiles = x.shape[0] // tm
        xmap = lambda i, j: (i, j)
    return pl.pallas_call(
        _glu_kernel,
        grid=(n_tiles, nn),
        in_specs=[
            pl.BlockSpec((tm, tn), xmap),
            pl.BlockSpec((tm, D_MODEL), lambda i, j: (i, 0)),
            pl.BlockSpec((None, D_MODEL, tn), lambda i, j: (0, 0, j)),
            pl.BlockSpec((None, D_MODEL, tn), lambda i, j: (0, 0, nn + j)),
        ],
        out_specs=pl.BlockSpec((tm, tn), xmap),
        out_shape=jax.ShapeDtypeStruct(x.shape, F32),
        compiler_params=_params(2),
        name="glu",
    )(x, g, w_glu, w_glu)


def _rotary_tables(pos):
    half = RET_DK // 2
    inv = ROPE_BASE ** (-jnp.arange(half, dtype=F32) / half)
    ang = pos[:, None] * inv[None, :]
    cos = jnp.cos(ang)
    sin = jnp.sin(ang)
    return jnp.concatenate([cos, cos], axis=-1), jnp.concatenate([-sin, sin], axis=-1)


def _retention_tables(nseq, seg):
    log_g = np.log1p(-np.exp2(-5.0 - np.arange(RET_HEADS)))
    row = np.arange(nseq * seg)
    t = (row % seg).astype(np.float64)
    sid = row // seg
    diff = t[:, None] - t[None, :]
    ok = (sid[:, None] == sid[None, :]) & (diff >= 0)
    dec = np.where(ok, np.exp(log_g[:, None, None] * np.where(ok, diff, 0.0)), 0.0)
    qd = np.broadcast_to(np.exp(log_g[:, None] * (t + 1.0))[..., None], (RET_HEADS, nseq * seg, RET_DK))
    kd = np.broadcast_to(np.exp(log_g[:, None] * (seg - 1.0 - t))[..., None], (RET_HEADS, nseq * seg, RET_DK))
    gains = tuple(float(v) for v in np.exp(log_g * seg))
    return jnp.asarray(dec, F32), jnp.asarray(qd, F32), jnp.asarray(kd, F32), gains


def _block_diag(blocks, rows_per, cols_per):
    tiled = jnp.tile(blocks.reshape(S5_NB, S5_GB * rows_per, cols_per), (1, 1, S5_GB))
    rg = np.arange(S5_GB * rows_per)[:, None] // rows_per
    cg = np.arange(S5_GB * cols_per)[None, :] // cols_per
    return (tiled * jnp.asarray(rg == cg, F32)).astype(BF16)


def _trunk(x3, pos, gdn_s, gdn_cb, ret_s, s5_re, s5_im, ffn_cb, prm, *, chained):
    b, l, _ = x3.shape
    m = b * l
    x = x3.reshape(m, D_MODEL)
    nchunk = l // CHUNK_ROWS if chained else 1

    proj, gate = _in_proj(x, prm['norm_mix_w'][0:1], prm['w_in_main'], prm['w_in_gate'])
    o_a, gdn_new, gcb_new = _gdn(proj, gate, gdn_cb, prm['gdn_conv_w'], prm['alog_row'], prm['dtb_row'],
                                 prm['gdn_norm_w'], None if chained else gdn_s, nbatch=b, nchunk=nchunk)
    if chained:
        cosf, sinf = _rotary_tables(pos)
        dec, qd, kd, g_chunk = _retention_tables(1, CHUNK_ROWS)
    else:
        cosf, sinf = _rotary_tables(jnp.tile(pos, CHUNK_ROWS // l))
        dec, qd, kd, g_chunk = _retention_tables(CHUNK_ROWS // l, l)
    o_b, ret_new = _ret(proj, cosf, sinf, dec, qd, kd, prm['ret_gn_w'], prm['ret_gn_b'],
                        None if chained else ret_s, g_chunk, nbatch=b, nchunk=nchunk)
    x = _out_proj(x, o_a, o_b, prm['w_out_a'], prm['w_out_b'])
    x, fcb0 = _ffn(x, prm['norm_ffn_w'][0:1], prm['w_up'], prm['ffn_conv_w'], prm['ffn_conv_b'],
                   prm['w_down'], ffn_cb, prm['norm_final_w'], layer=0, nseq_total=b, final_norm=False,
                   out_time_major=chained)

    s5_args = (prm['norm_mix_w'][1:2], prm['s5_wb2'], prm['s5_wc2'], prm['s5_a_re'], prm['s5_a_im'],
               prm['s5_d'], s5_re, s5_im)
    if chained:
        g, s5r_new, s5i_new = _s5_seq(x, *s5_args, nseq=b, seg=CHUNK_ROWS, chained=True)
        x = _glu(x, g.reshape(m, D_MODEL), prm['w_glu'], time_major_seqs=b)
    else:
        g, s5r_new, s5i_new = _s5_seq(x, *s5_args, nseq=CHUNK_ROWS // l, seg=l, chained=False)
        x = _glu(x, g, prm['w_glu'])
    y, fcb1 = _ffn(x, prm['norm_ffn_w'][1:2], prm['w_up'], prm['ffn_conv_w'], prm['ffn_conv_b'],
                   prm['w_down'], ffn_cb, prm['norm_final_w'], layer=1, nseq_total=b, final_norm=True,
                   x_time_major=chained)
    return (y.reshape(b, l, D_MODEL), gdn_new[None], gcb_new[None], ret_new[None],
            s5r_new.reshape(1, b, S5_GROUPS, S5_P), s5i_new.reshape(1, b, S5_GROUPS, S5_P),
            jnp.stack([fcb0, fcb1]))


def kernel(x_prompt, x_sample, state_gdn, state_gdn_conv, state_ret, state_s5_re, state_s5_im,
           state_ffn_conv, norm_mix_w, norm_ffn_w, norm_final_w,
           w_in, gdn_conv_w, gdn_a_log, gdn_dt_bias, gdn_norm_w, ret_gn_w, ret_gn_b, w_out,
           s5_lam_re, s5_lam_im, s5_log_dt, s5_b_re, s5_b_im, s5_c_re, s5_c_im, s5_d, w_glu,
           w_up, ffn_conv_w, ffn_conv_b, w_down):
    bp, lp, _ = x_prompt.shape
    bs, ls, _ = x_sample.shape
    past_len = 16384

    w_in_main, w_in_gate = _w_in_prep(w_in)
    pad_row = lambda v: jnp.pad(v.reshape(1, -1), ((0, 0), (0, GATE_W - v.shape[-1])))

    a_re, a_im, bb_re, bb_im = _s5_disc(s5_lam_re[0], s5_lam_im[0], s5_log_dt[0].reshape(S5_GROUPS, 1),
                                        s5_b_re[0].transpose(0, 2, 1), s5_b_im[0].transpose(0, 2, 1))
    s5_wb2 = jnp.concatenate([_block_diag(bb_re, S5_GROUP, S5_P),
                              _block_diag(bb_im, S5_GROUP, S5_P)], axis=1)
    s5_wc2 = jnp.concatenate([_block_diag(s5_c_re[0].transpose(0, 2, 1), S5_P, S5_GROUP),
                              _block_diag(s5_c_im[0].transpose(0, 2, 1), S5_P, S5_GROUP)], axis=-1)

    prm = dict(
        norm_mix_w=norm_mix_w, norm_ffn_w=norm_ffn_w, norm_final_w=norm_final_w.reshape(1, D_MODEL),
        w_in_main=w_in_main, w_in_gate=w_in_gate,
        gdn_conv_w=gdn_conv_w[0], alog_row=pad_row(gdn_a_log[0]), dtb_row=pad_row(gdn_dt_bias[0]),
        gdn_norm_w=gdn_norm_w[0].reshape(1, GDN_DV),
        ret_gn_w=ret_gn_w[0].reshape(1, RET_V_W), ret_gn_b=ret_gn_b[0].reshape(1, RET_V_W),
        w_out_a=w_out[0, :GDN_V_W].astype(BF16), w_out_b=w_out[0, GDN_V_W:].astype(BF16),
        s5_wb2=s5_wb2, s5_wc2=s5_wc2,
        s5_a_re=a_re.reshape(1, S5_STATE), s5_a_im=a_im.reshape(1, S5_STATE), s5_d=s5_d[0].reshape(1, D_MODEL),
        w_glu=w_glu,
        w_up=_w_up_prep(w_up), ffn_conv_w=ffn_conv_w, ffn_conv_b=ffn_conv_b[:, None, :],
        w_down=w_down,
    )

    z_gcb = jnp.zeros((bp, GDN_CONV - 1, GDN_CONV_W), F32)
    z_s5 = jnp.zeros((bp, S5_STATE), F32)
    z_fcb = jnp.zeros((2, bp, FFN_CONV - 1, 2 * D_FF), F32)
    pos_p = jnp.arange(lp, dtype=F32)
    pos_s = past_len + jnp.arange(ls, dtype=F32)

    y_p, gdn_p, gcb_p, ret_p, s5r_p, s5i_p, fcb_p = _trunk(
        x_prompt, pos_p, None, z_gcb, None, z_s5, z_s5, z_fcb, prm, chained=True)
    y_s, gdn_s, gcb_s, ret_s, s5r_s, s5i_s, fcb_s = _trunk(
        x_sample, pos_s, state_gdn[0], state_gdn_conv[0], state_ret[0],
        state_s5_re[0].reshape(bs, S5_STATE), state_s5_im[0].reshape(bs, S5_STATE),
        state_ffn_conv, prm, chained=False)
    return (y_p, y_s, gdn_p, gdn_s, gcb_p, gcb_s, ret_p, ret_s,
            s5r_p, s5r_s, s5i_p, s5i_s, fcb_p, fcb_s)
```

```python
import functools
import math

import jax
import jax.numpy as jnp
import numpy as np
from jax import lax
from jax.experimental import pallas as pl
from jax.experimental.pallas import tpu as pltpu

F32 = jnp.float32
BF16 = jnp.bfloat16

D_MODEL = 2048
GDN_HEADS = 8
GDN_DK = 128
GDN_DV = 128
GDN_CONV = 4
RET_HEADS = 4
RET_DK = 128
RET_DV = 256
ROPE_BASE = 10000.0
S5_GROUP = 16
S5_GROUPS = 128
S5_P = 64
S5_STATE = S5_GROUPS * S5_P
D_FF = 5632
FFN_CONV = 3
EPS = 1e-6

GDN_QK_W = GDN_HEADS * GDN_DK
GDN_V_W = GDN_HEADS * GDN_DV
GDN_CONV_W = 2 * GDN_QK_W + GDN_V_W
RET_QK_W = RET_HEADS * RET_DK
RET_V_W = RET_HEADS * RET_DV
PROJ_W = GDN_CONV_W + GDN_V_W + 2 * RET_QK_W + 2 * RET_V_W
GATE_W = 128

CHUNK_ROWS = 64
GDN_GROUP = 4
GDN_STEP_CHUNKS = 2
GDN_DIAG = 16
RET_STEP_CHUNKS = 4
LANES = 128
VMEM_LIMIT = 56 * 1024 * 1024

S5_GB = 8
S5_NB = S5_GROUPS // S5_GB
S5_BW = S5_GB * S5_P
GELU_C = math.sqrt(2.0 / math.pi)


def _params(n_grid):
    return pltpu.CompilerParams(dimension_semantics=("arbitrary",) * n_grid,
                                vmem_limit_bytes=VMEM_LIMIT)


def _dot(a, b):
    return jnp.dot(a, b, preferred_element_type=F32)


def _dot_nt(a, b):
    return lax.dot_general(a, b, (((1,), (1,)), ((), ())), preferred_element_type=F32)


def _dot_tn(a, b):
    return lax.dot_general(a, b, (((0,), (0,)), ((), ())), preferred_element_type=F32)


def _sigmoid(x):
    return 0.5 + 0.5 * jnp.tanh(0.5 * x)


def _silu(x):
    h = 0.5 * x
    return h + h * jnp.tanh(h)


def _rms_rows(x, w):
    return x * lax.rsqrt(jnp.mean(x * x, axis=-1, keepdims=True) + EPS) * w


def _shift_rows(x, k, tpos, buf, nseq, seg):
    rows, width = x.shape
    nb = buf.shape[1]
    prev = pltpu.roll(x, k, 0)
    for t in range(k):
        src = buf[:, nb - k + t:nb - k + t + 1, :]
        srcb = jnp.broadcast_to(src, (nseq, seg, width)).reshape(rows, width)
        prev = jnp.where(tpos == t, srcb, prev)
    return prev


def _w_in_layout(w_in):
    w_t = jnp.swapaxes(w_in, 1, 2)[0]
    o_gate = GDN_CONV_W + GDN_V_W
    o_rest = o_gate + 2 * GDN_HEADS
    main = jnp.concatenate([w_t[:o_gate], w_t[o_rest:]], axis=0).astype(BF16)
    gate = jnp.pad(w_t[o_gate:o_rest], ((0, GATE_W - 2 * GDN_HEADS), (0, 0))).astype(BF16)
    return main, gate


def _in_proj_kernel(x_ref, nw_ref, w_ref, wg_ref, o_ref, og_ref, h_scr):
    @pl.when(pl.program_id(1) == 0)
    def _():
        hb = _rms_rows(x_ref[...], nw_ref[...]).astype(BF16)
        h_scr[...] = hb
        og_ref[...] = _dot_nt(hb, wg_ref[...])

    o_ref[...] = _dot_nt(h_scr[...], w_ref[...])


IN_PROJ_ROWS = 2048
IN_PROJ_TILE = 2048 * 512


def _in_proj(x, nw, w_main, w_gate):
    m = x.shape[0]
    tm = min(IN_PROJ_ROWS, m)
    tn = IN_PROJ_TILE // tm
    return pl.pallas_call(
        _in_proj_kernel,
        grid=(m // tm, PROJ_W // tn),
        in_specs=[
            pl.BlockSpec((tm, D_MODEL), lambda i, j: (i, 0), pipeline_mode=pl.Buffered(1)),
            pl.BlockSpec((1, D_MODEL), lambda i, j: (0, 0)),
            pl.BlockSpec((tn, D_MODEL), lambda i, j: (j, 0)),
            pl.BlockSpec((GATE_W, D_MODEL), lambda i, j: (0, 0)),
        ],
        out_specs=[
            pl.BlockSpec((tm, tn), lambda i, j: (i, j)),
            pl.BlockSpec((tm, GATE_W), lambda i, j: (i, 0)),
        ],
        out_shape=[jax.ShapeDtypeStruct((m, PROJ_W), F32),
                   jax.ShapeDtypeStruct((m, GATE_W), F32)],
        scratch_shapes=[pltpu.VMEM((tm, D_MODEL), BF16)],
        compiler_params=_params(2),
        name="in_proj",
    )(x, nw, w_main, w_gate)


def _gdn_kernel(*refs, nseq, seg, chained):
    if chained:
        (qkv_ref, z_ref, gate_ref, cbuf_ref, cw_ref, alog_ref, dtb_ref, nw_ref, sum_ref, mask_ref,
         o_ref, s_out_ref, cb_out_ref, carry_scr) = refs
        s_in_ref = s_out_ref
        first = pl.program_id(1) == 0

        @pl.when(first)
        def _():
            s_out_ref[...] = jnp.zeros_like(s_out_ref)
    else:
        (qkv_ref, z_ref, gate_ref, cbuf_ref, cw_ref, alog_ref, dtb_ref, nw_ref, sum_ref, mask_ref, s_in_ref,
         o_ref, s_out_ref, cb_out_ref) = refs

    rows = nseq * seg
    step_rows = qkv_ref.shape[0]
    cps = step_rows // rows
    shift = int(math.log2(seg))
    x = qkv_ref[...]
    if chained:
        buf = jnp.where(first, cbuf_ref[...], carry_scr[5:8, :].reshape(1, GDN_CONV - 1, GDN_CONV_W))
        conv_seg = step_rows
    else:
        buf = cbuf_ref[...]
        conv_seg = seg
    row_id = lax.broadcasted_iota(jnp.int32, (step_rows, 1), 0)
    tpos = row_id & (conv_seg - 1)

    acc = x * cw_ref[GDN_CONV - 1:GDN_CONV, :]
    for k in range(1, GDN_CONV):
        acc = acc + _shift_rows(x, k, tpos, buf, nseq, conv_seg) * cw_ref[GDN_CONV - 1 - k:GDN_CONV - k, :]
    act = _silu(acc)
    cb_out_ref[...] = x.reshape(nseq, conv_seg, GDN_CONV_W)[:, conv_seg - (GDN_CONV - 1):, :]
    if chained:
        carry_scr[...] = x[step_rows - 8:, :]

    gate = gate_ref[...]
    xa = gate + dtb_ref[...]
    softplus = jnp.maximum(xa, 0.0) + jnp.log(1.0 + jnp.exp(-jnp.abs(xa)))
    log_a = -jnp.exp(alog_ref[...]) * softplus
    beta_all = _sigmoid(gate)

    sum_m = sum_ref[...]
    a1 = log_a.astype(BF16)
    r1 = log_a - a1.astype(F32)
    a2 = r1.astype(BF16)
    a3 = (r1 - a2.astype(F32)).astype(BF16)
    if nseq > 1:
        row_masks = [((row_id >> shift) == s).astype(F32) for s in range(nseq)]
    nw = nw_ref[...]
    chunk_sl = [slice(c * rows, (c + 1) * rows) for c in range(cps)]
    g_all = [_dot(sum_m, a1[sl]) + (_dot(sum_m, a2[sl]) + _dot(sum_m, a3[sl])) for sl in chunk_sl]
    g = [t[:rows] for t in g_all]
    g_last = [t[rows:] for t in g_all]
    g2_t = [jnp.concatenate([t, t], axis=0).T for t in g]
    e_g = [jnp.exp(t) for t in g]
    e_rest = [jnp.exp(tl - t) for tl, t in zip(g_last, g)]
    e_last = [jnp.exp(t) for t in g_last]

    incl_f = mask_ref[0]
    strict_f = mask_ref[1]
    lane = lax.broadcasted_iota(jnp.int32, (1, LANES), 1)

    def stack_cols(tile, heads, off=0):
        return jnp.concatenate([tile[:, off + h:off + h + 1] for h in heads], axis=0)

    groups = [list(range(g0, g0 + GDN_GROUP)) for g0 in range(0, GDN_HEADS, GDN_GROUP)]
    units = [(c, heads) for c in range(cps) for heads in groups]
    ks, q_all, k_all, v_all = [], [], [], []
    for c, heads in units:
        a_c = act[chunk_sl[c]]
        qs, kn, vs = [], [], []
        for h in heads:
            lo = h * GDN_DK
            q = a_c[:, lo:lo + GDN_DK]
            k = a_c[:, GDN_QK_W + lo:GDN_QK_W + lo + GDN_DK]
            qs.append(q * lax.rsqrt(jnp.sum(q * q, axis=-1, keepdims=True) + EPS) * (GDN_DK ** -0.5))
            kn.append(k * lax.rsqrt(jnp.sum(k * k, axis=-1, keepdims=True) + EPS))
            vs.append(a_c[:, 2 * GDN_QK_W + lo:2 * GDN_QK_W + lo + GDN_DV])
        ks.append(kn)
        q_all.append(jnp.concatenate(qs, axis=0))
        k_all.append(jnp.concatenate(kn, axis=0))
        v_all.append(jnp.concatenate(vs, axis=0))
    gc = [stack_cols(g[c], heads) for c, heads in units]
    bc = [stack_cols(beta_all[chunk_sl[c]], heads, GDN_HEADS) for c, heads in units]
    egc = [stack_cols(e_g[c], heads) for c, heads in units]
    gr = [jnp.concatenate(
        [jnp.where(lane < rows, g2_t[c][heads[j]:heads[j] + 1, :], g2_t[c][heads[j + 1]:heads[j + 1] + 1, :])
         for j in range(0, GDN_GROUP, 2)], axis=1) for c, heads in units]
    decay = [jnp.exp(jnp.minimum(a - r, 0.0)) * incl_f for a, r in zip(gc, gr)]
    kb = [k.astype(BF16) for k in k_all]
    kk = [_dot_nt(k, k) for k in kb]
    qk = [_dot_nt(q.astype(BF16), k) * d for q, k, d in zip(q_all, kb, decay)]
    lms = [b * m * d * strict_f for b, m, d in zip(bc, kk, decay)]
    inv_e = _unit_lower_inverse_minus_eye(lms, mask_ref, seg)
    rhs = [jnp.concatenate([v * b, k * (b * e)], axis=-1) for v, k, b, e in zip(v_all, k_all, bc, egc)]
    sol = [r + _dot(e.astype(BF16), r.astype(BF16)) for e, r in zip(inv_e, rhs)]
    qd_all = [q * e for q, e in zip(q_all, egc)]

    states = [[s_in_ref[s, h] for s in range(nseq)] for h in range(GDN_HEADS)]
    for c in range(cps):
        unit_ids = [u for u, (uc, _) in enumerate(units) if uc == c]
        v_new, o_all = {}, {}
        for u in unit_ids:
            ws_parts, qs_parts = [], []
            for j, h in enumerate(units[u][1]):
                r0 = j * rows
                lhs = jnp.concatenate([sol[u][r0:r0 + rows, GDN_DV:], qd_all[u][r0:r0 + rows]], axis=0).astype(BF16)
                ws = None
                for s in range(nseq):
                    part = _dot(lhs, states[h][s].astype(BF16))
                    if nseq > 1:
                        part = part * jnp.concatenate([row_masks[s], row_masks[s]], axis=0)
                    ws = part if ws is None else ws + part
                ws_parts.append(ws[:rows])
                qs_parts.append(ws[rows:])
            v_new[u] = sol[u][:, :GDN_DV] - jnp.concatenate(ws_parts, axis=0)
            o_all[u] = jnp.concatenate(qs_parts, axis=0)
        for u in unit_ids:
            o_all[u] = o_all[u] + _dot(qk[u].astype(BF16), v_new[u].astype(BF16))
        for u in unit_ids:
            for j, h in enumerate(units[u][1]):
                r0 = j * rows
                lo = h * GDN_DK
                k_rest = (ks[u][j] * e_rest[c][:, h:h + 1]).astype(BF16)
                vn = v_new[u][r0:r0 + rows]
                for s in range(nseq):
                    vsel = vn * row_masks[s] if nseq > 1 else vn
                    states[h][s] = (states[h][s] * e_last[c][s * seg:s * seg + 1, h:h + 1]
                                    + _dot_tn(k_rest, vsel.astype(BF16)))
                o = o_all[u][r0:r0 + rows]
                o = o * lax.rsqrt(jnp.mean(o * o, axis=-1, keepdims=True) + EPS) * nw
                o = o * _silu(z_ref[chunk_sl[c], lo:lo + GDN_DV])
                o_ref[chunk_sl[c], lo:lo + GDN_DV] = o.astype(BF16)
    for h in range(GDN_HEADS):
        for s in range(nseq):
            s_out_ref[s, h] = states[h][s]


def _gdn_masks(nseq, seg):
    n = GDN_GROUP * nseq * seg
    ri, ci = np.arange(n)[:, None], np.arange(n)[None, :]
    same = (ri // seg) == (ci // seg)
    blk = min(GDN_DIAG, seg)
    masks = [same & (ci <= ri), same & (ci < ri), -1.0 * ((ri // blk) == (ci // blk))]
    size = blk
    while size < seg:
        masks.append(((ri // (2 * size)) == (ci // (2 * size))) & ((ri // size) != (ci // size)))
        size *= 2
    return jnp.asarray(np.stack([np.asarray(m, np.float32) for m in masks]), F32)


def _gdn_sum_matrix(nseq, seg):
    n = nseq * seg
    ri, ci = np.arange(n)[:, None], np.arange(n)[None, :]
    same = (ri // seg) == (ci // seg)
    return jnp.asarray(np.concatenate([same & (ci <= ri), same]), BF16)


def _unit_lower_inverse_minus_eye(lms, mask_ref, seg):
    blk = min(GDN_DIAG, seg)
    neg_diag = mask_ref[2]
    err = [lm * neg_diag for lm in lms]
    m_b = [e.astype(BF16) for e in err]
    for _ in range(int(math.log2(blk)) - 1):
        m_f = [_dot(m, m) for m in m_b]
        m_b = [m.astype(BF16) for m in m_f]
        err = [e + m + _dot(e.astype(BF16), mb) for e, m, mb in zip(err, m_f, m_b)]
    level = 3
    size = blk
    while size < seg:
        c = [lm * mask_ref[level] for lm in lms]
        e_b = [e.astype(BF16) for e in err]
        y = [x + _dot(e, x.astype(BF16)) for x, e in zip(c, e_b)]
        err = [e - (v + _dot(v.astype(BF16), eb)) for e, v, eb in zip(err, y, e_b)]
        level += 1
        size *= 2
    return err


def _gdn(proj, gate, cbuf, conv_w, alog_row, dtb_row, norm_w, state, *, nbatch, nchunk):
    chained = state is None
    rows_total = proj.shape[0]
    if chained:
        nseq, seg = 1, CHUNK_ROWS
        step_rows = GDN_STEP_CHUNKS * CHUNK_ROWS
        nstep = nchunk // GDN_STEP_CHUNKS
        grid = (nbatch, nstep)
        rb = lambda b, n: b * nstep + n
        sb = lambda b, n: b
    else:
        nseq, seg = 8, 8
        step_rows = CHUNK_ROWS
        grid = (rows_total // CHUNK_ROWS,)
        rb = lambda i: i
        sb = lambda i: i
    n_state = nbatch
    sum_m = _gdn_sum_matrix(nseq, seg)
    masks = _gdn_masks(nseq, seg)
    wrap = lambda f, *tail: (lambda *g: (f(*g),) + tail)
    const = lambda *tail: (lambda *g: tail)
    in_specs = [
        pl.BlockSpec((step_rows, GDN_CONV_W), wrap(rb, 0)),
        pl.BlockSpec((step_rows, GDN_V_W), wrap(rb, GDN_CONV_W // GDN_V_W)),
        pl.BlockSpec((step_rows, GATE_W), wrap(rb, 0)),
        pl.BlockSpec((nseq, GDN_CONV - 1, GDN_CONV_W), wrap(sb, 0, 0)),
        pl.BlockSpec((GDN_CONV, GDN_CONV_W), const(0, 0)),
        pl.BlockSpec((1, GATE_W), const(0, 0)),
        pl.BlockSpec((1, GATE_W), const(0, 0)),
        pl.BlockSpec((1, GDN_DV), const(0, 0)),
        pl.BlockSpec(sum_m.shape, const(0, 0)),
        pl.BlockSpec(masks.shape, const(0, 0, 0)),
    ]
    args = [proj, proj, gate, cbuf, conv_w, alog_row, dtb_row, norm_w, sum_m, masks]
    scratch = []
    if chained:
        scratch = [pltpu.VMEM((8, GDN_CONV_W), F32)]
    else:
        in_specs.append(pl.BlockSpec((nseq, GDN_HEADS, GDN_DK, GDN_DV), wrap(sb, 0, 0, 0)))
        args.append(state)
    out_specs = [
        pl.BlockSpec((step_rows, GDN_V_W), wrap(rb, 0)),
        pl.BlockSpec((nseq, GDN_HEADS, GDN_DK, GDN_DV), wrap(sb, 0, 0, 0)),
        pl.BlockSpec((nseq, GDN_CONV - 1, GDN_CONV_W), wrap(sb, 0, 0)),
    ]
    out_shape = [
        jax.ShapeDtypeStruct((rows_total, GDN_V_W), BF16),
        jax.ShapeDtypeStruct((n_state, GDN_HEADS, GDN_DK, GDN_DV), F32),
        jax.ShapeDtypeStruct((n_state, GDN_CONV - 1, GDN_CONV_W), F32),
    ]
    return pl.pallas_call(
        functools.partial(_gdn_kernel, nseq=nseq, seg=seg, chained=chained),
        grid=grid, in_specs=in_specs, out_specs=out_specs, out_shape=out_shape,
        scratch_shapes=scratch, compiler_params=_params(len(grid)),
        name="gdn_chained" if chained else "gdn_step",
    )(*args)


def _ret_kernel(*refs, nseq, seg, chained, g_chunk):
    if chained:
        (q_ref, k_ref, v_ref, gt_ref, cos_ref, sin_ref, dec_ref, qd_ref, kd_ref, gw_ref, gb_ref,
         o_ref, s_out_ref) = refs
        s_in_ref = s_out_ref

        @pl.when(pl.program_id(1) == 0)
        def _():
            s_out_ref[...] = jnp.zeros_like(s_out_ref)
    else:
        (q_ref, k_ref, v_ref, gt_ref, cos_ref, sin_ref, dec_ref, qd_ref, kd_ref, gw_ref, gb_ref,
         s_in_ref, o_ref, s_out_ref) = refs

    rows = nseq * seg
    cps = q_ref.shape[0] // rows
    shift = int(math.log2(seg))
    if nseq > 1:
        row_id = lax.broadcasted_iota(jnp.int32, (rows, 1), 0)
        row_masks = [((row_id >> shift) == s).astype(F32) for s in range(nseq)]

    units = [(c, h) for c in range(cps) for h in range(RET_HEADS)]
    rsl = lambda c: slice(c * rows, (c + 1) * rows)
    qr, kr = [], []
    for c, h in units:
        cosf = cos_ref[rsl(c), :]
        sinf = sin_ref[rsl(c), :]
        q = q_ref[rsl(c), h * RET_DK:(h + 1) * RET_DK]
        k = k_ref[rsl(c), h * RET_DK:(h + 1) * RET_DK]
        qr.append(q * cosf + pltpu.roll(q, RET_DK // 2, 1) * sinf)
        kr.append((k * cosf + pltpu.roll(k, RET_DK // 2, 1) * sinf) * (RET_DK ** -0.5))
    vf = [v_ref[rsl(c), h * RET_DV:(h + 1) * RET_DV] for c, h in units]
    vb = [v.astype(BF16) for v in vf]
    kb = [k.astype(BF16) for k in kr]
    sc = [_dot_nt(q.astype(BF16), k) * dec_ref[h] for q, k, (c, h) in zip(qr, kb, units)]
    o = [_dot(s.astype(BF16), v) for s, v in zip(sc, vb)]
    q_dec = [(q * qd_ref[h]).astype(BF16) for q, (c, h) in zip(qr, units)]
    k_dec = [(k * kd_ref[h]).astype(BF16) for k, (c, h) in zip(kr, units)]
    if nseq > 1:
        for u, (c, h) in enumerate(units):
            for s in range(nseq):
                state = s_in_ref[s, h]
                o[u] = o[u] + _dot(q_dec[u], state.astype(BF16)) * row_masks[s]
                s_out_ref[s, h] = state * g_chunk[h] + _dot_tn(k_dec[u], (vf[u] * row_masks[s]).astype(BF16))
    else:
        kv = [_dot_tn(k, v) for k, v in zip(k_dec, vb)]
        state = [s_in_ref[0, h] for h in range(RET_HEADS)]
        for u, (c, h) in enumerate(units):
            o[u] = o[u] + _dot(q_dec[u], state[h].astype(BF16))
            state[h] = state[h] * g_chunk[h] + kv[u]
        for h in range(RET_HEADS):
            s_out_ref[0, h] = state[h]
    for u, (c, h) in enumerate(units):
        vsl = slice(h * RET_DV, (h + 1) * RET_DV)
        mu = jnp.mean(o[u], axis=-1, keepdims=True)
        oc = o[u] - mu
        var = jnp.mean(oc * oc, axis=-1, keepdims=True)
        on = oc * lax.rsqrt(var + EPS)
        on = (on * gw_ref[:, vsl] + gb_ref[:, vsl]) * _silu(gt_ref[rsl(c), vsl])
        o_ref[rsl(c), vsl] = on.astype(BF16)


def _ret(proj, cosf, sinf, dec, qd, kd, gw, gb, state, g_chunk, *, nbatch, nchunk):
    chained = state is None
    rows_total = proj.shape[0]
    if chained:
        nseq, seg = 1, CHUNK_ROWS
        step_rows = RET_STEP_CHUNKS * CHUNK_ROWS
        nstep = nchunk // RET_STEP_CHUNKS
        grid = (nbatch, nstep)
        rb = lambda b, n: b * nstep + n
        sb = lambda b, n: b
        pb = lambda b, n: n
    else:
        nseq, seg = 8, 8
        step_rows = CHUNK_ROWS
        grid = (rows_total // CHUNK_ROWS,)
        rb = lambda i: i
        sb = lambda i: i
        pb = lambda i: 0
    wrap = lambda f, *tail: (lambda *g: (f(*g),) + tail)
    const = lambda *tail: (lambda *g: tail)
    q_col = (GDN_CONV_W + GDN_V_W) // RET_QK_W
    v_col = (GDN_CONV_W + GDN_V_W + 2 * RET_QK_W) // RET_V_W
    in_specs = [
        pl.BlockSpec((step_rows, RET_QK_W), wrap(rb, q_col)),
        pl.BlockSpec((step_rows, RET_QK_W), wrap(rb, q_col + 1)),
        pl.BlockSpec((step_rows, RET_V_W), wrap(rb, v_col)),
        pl.BlockSpec((step_rows, RET_V_W), wrap(rb, v_col + 1)),
        pl.BlockSpec((step_rows, RET_DK), wrap(pb, 0)),
        pl.BlockSpec((step_rows, RET_DK), wrap(pb, 0)),
        pl.BlockSpec((RET_HEADS, CHUNK_ROWS, CHUNK_ROWS), const(0, 0, 0)),
        pl.BlockSpec((RET_HEADS, CHUNK_ROWS, RET_DK), const(0, 0, 0)),
        pl.BlockSpec((RET_HEADS, CHUNK_ROWS, RET_DK), const(0, 0, 0)),
        pl.BlockSpec((1, RET_V_W), const(0, 0)),
        pl.BlockSpec((1, RET_V_W), const(0, 0)),
    ]
    args = [proj, proj, proj, proj, cosf, sinf, dec, qd, kd, gw, gb]
    if not chained:
        in_specs.append(pl.BlockSpec((nseq, RET_HEADS, RET_DK, RET_DV), wrap(sb, 0, 0, 0)))
        args.append(state)
    out_specs = [
        pl.BlockSpec((step_rows, RET_V_W), wrap(rb, 0)),
        pl.BlockSpec((nseq, RET_HEADS, RET_DK, RET_DV), wrap(sb, 0, 0, 0)),
    ]
    out_shape = [
        jax.ShapeDtypeStruct((rows_total, RET_V_W), BF16),
        jax.ShapeDtypeStruct((nbatch, RET_HEADS, RET_DK, RET_DV), F32),
    ]
    return pl.pallas_call(
        functools.partial(_ret_kernel, nseq=nseq, seg=seg, chained=chained, g_chunk=g_chunk),
        grid=grid, in_specs=in_specs, out_specs=out_specs, out_shape=out_shape,
        compiler_params=_params(len(grid)),
        name="ret_chained" if chained else "ret_step",
    )(*args)


def _out_proj_kernel(x_ref, a_ref, b_ref, wa_ref, wb_ref, o_ref):
    o_ref[...] = x_ref[...] + (_dot(a_ref[...], wa_ref[...]) + _dot(b_ref[...], wb_ref[...]))


def _out_proj(x, o_a, o_b, w_a, w_b, tm=512):
    m = x.shape[0]
    return pl.pallas_call(
        _out_proj_kernel,
        grid=(m // tm,),
        in_specs=[
            pl.BlockSpec((tm, D_MODEL), lambda i: (i, 0)),
            pl.BlockSpec((tm, GDN_V_W), lambda i: (i, 0)),
            pl.BlockSpec((tm, RET_V_W), lambda i: (i, 0)),
            pl.BlockSpec((GDN_V_W, D_MODEL), lambda i: (0, 0)),
            pl.BlockSpec((RET_V_W, D_MODEL), lambda i: (0, 0)),
        ],
        out_specs=pl.BlockSpec((tm, D_MODEL), lambda i: (i, 0)),
        out_shape=jax.ShapeDtypeStruct((m, D_MODEL), F32),
        compiler_params=_params(1),
        name="out_proj",
    )(x, o_a, o_b, w_a, w_b)


FFN_TF = 512


def _w_up_prep_kernel(v_ref, g_ref, o_ref):
    o_ref[:, :FFN_TF] = v_ref[...].astype(BF16)
    o_ref[:, FFN_TF:] = g_ref[...].astype(BF16)


def _w_up_prep(w_up):
    depth = w_up.shape[0]
    nf = D_FF // FFN_TF
    return pl.pallas_call(
        _w_up_prep_kernel,
        grid=(depth, nf),
        in_specs=[pl.BlockSpec((None, D_MODEL, FFN_TF), lambda l, j: (l, 0, j)),
                  pl.BlockSpec((None, D_MODEL, FFN_TF), lambda l, j: (l, 0, nf + j))],
        out_specs=pl.BlockSpec((None, D_MODEL, 2 * FFN_TF), lambda l, j: (l, 0, j)),
        out_shape=jax.ShapeDtypeStruct(w_up.shape, BF16),
        compiler_params=_params(2),
        name="w_up_prep",
    )(w_up, w_up)


def _ffn_kernel(*refs, nseq, seg, tiles_per_seq, final_norm):
    (x_ref, nw_ref, wu_ref, cwv_ref, cwg_ref, bv_ref, bg_ref, wd_ref, bufv_ref, bufg_ref,
     fw_ref, o_ref, nbv_ref, nbg_ref, h_scr) = refs[:15]
    chained = tiles_per_seq > 1
    i = pl.program_id(0)
    j = pl.program_id(1)
    nf = pl.num_programs(1)
    rows = nseq * seg
    taps = FFN_CONV - 1

    @pl.when(j == 0)
    def _():
        xv = x_ref[...]
        h_scr[...] = _rms_rows(xv, nw_ref[...]).astype(BF16)
        o_ref[...] = xv

    up_both = _dot(h_scr[...], wu_ref[...])
    tf = up_both.shape[1] // 2

    def branch_chained(up, cw_ref, b_ref, buf_ref, nb_ref, carry_ref, up_scr):
        up_scr[8:8 + rows, :] = up
        up_scr[8 - taps:8, :] = jnp.where(i % tiles_per_seq == 0, buf_ref[0], carry_ref[j])
        tail = up[rows - taps:, :]
        nb_ref[0] = tail
        carry_ref[j] = tail
        out = up * cw_ref[taps:taps + 1, :] + b_ref[...]
        for k in range(1, FFN_CONV):
            out = out + up_scr[8 - k:8 - k + rows, :] * cw_ref[taps - k:taps - k + 1, :]
        return out

    def branch_short(up, cw_ref, b_ref, buf_ref, nb_ref):
        tpos = lax.broadcasted_iota(jnp.int32, (rows, 1), 0) & (seg - 1)
        buf = buf_ref[...]
        out = up * cw_ref[taps:taps + 1, :] + b_ref[...]
        for k in range(1, FFN_CONV):
            out = out + _shift_rows(up, k, tpos, buf, nseq, seg) * cw_ref[taps - k:taps - k + 1, :]
        nb_ref[...] = up.reshape(nseq, seg, up.shape[1])[:, seg - taps:, :]
        return out

    if chained:
        cv_scr, cg_scr, uv_scr, ug_scr = refs[15:19]
        val = branch_chained(up_both[:, :tf], cwv_ref, bv_ref, bufv_ref, nbv_ref, cv_scr, uv_scr)
        gate = branch_chained(up_both[:, tf:], cwg_ref, bg_ref, bufg_ref, nbg_ref, cg_scr, ug_scr)
    else:
        val = branch_short(up_both[:, :tf], cwv_ref, bv_ref, bufv_ref, nbv_ref)
        gate = branch_short(up_both[:, tf:], cwg_ref, bg_ref, bufg_ref, nbg_ref)
    act = (_silu(gate) * val).astype(BF16)
    o_ref[...] += _dot(act, wd_ref[...].astype(BF16))

    if final_norm:
        @pl.when(j == nf - 1)
        def _():
            o_ref[...] = _rms_rows(o_ref[...], fw_ref[...])


def _ffn(x, nw, w_up, conv_w, conv_b, w_down, buf, final_w, *, layer, nseq_total, final_norm,
         x_time_major=False, out_time_major=False, tm=1024, tf=FFN_TF):
    m = x.shape[0] * x.shape[1] // D_MODEL
    seq_len = m // nseq_total
    if seq_len >= tm:
        nseq, seg, tiles_per_seq = 1, tm, seq_len // tm
    else:
        nseq, seg, tiles_per_seq = tm // seq_len, seq_len, 1
    assert tf == FFN_TF, "the up-projection weight is laid out in FFN_TF chunks"
    nf = D_FF // tf
    sidx = (lambda i, j: i // tiles_per_seq) if tiles_per_seq > 1 else (lambda i, j: i)
    seq_major = lambda i, j: (i, 0)
    time_major = lambda i, j: (i % tiles_per_seq, i // tiles_per_seq)
    once = pl.Buffered(1)
    in_specs = [
        pl.BlockSpec((tm, D_MODEL), time_major if x_time_major else seq_major),
        pl.BlockSpec((1, D_MODEL), lambda i, j: (0, 0)),
        pl.BlockSpec((None, D_MODEL, 2 * tf), lambda i, j: (layer, 0, j)),
        pl.BlockSpec((None, FFN_CONV, tf), lambda i, j: (layer, 0, j)),
        pl.BlockSpec((None, FFN_CONV, tf), lambda i, j: (layer, 0, nf + j)),
        pl.BlockSpec((None, 1, tf), lambda i, j: (layer, 0, j)),
        pl.BlockSpec((None, 1, tf), lambda i, j: (layer, 0, nf + j)),
        pl.BlockSpec((None, tf, D_MODEL), lambda i, j: (layer, j, 0)),
        pl.BlockSpec((None, nseq, FFN_CONV - 1, tf), lambda i, j: (layer, sidx(i, j), 0, j)),
        pl.BlockSpec((None, nseq, FFN_CONV - 1, tf), lambda i, j: (layer, sidx(i, j), 0, nf + j)),
        pl.BlockSpec((1, D_MODEL), lambda i, j: (0, 0)),
    ]
    out_specs = [
        pl.BlockSpec((tm, D_MODEL), time_major if out_time_major else seq_major, pipeline_mode=once),
        pl.BlockSpec((nseq, FFN_CONV - 1, tf), lambda i, j: (i, 0, j)),
        pl.BlockSpec((nseq, FFN_CONV - 1, tf), lambda i, j: (i, 0, j)),
    ]
    out_shape = [
        jax.ShapeDtypeStruct((seq_len, nseq_total * D_MODEL) if out_time_major else (m, D_MODEL), F32),
        jax.ShapeDtypeStruct((m // tm * nseq, FFN_CONV - 1, D_FF), F32),
        jax.ShapeDtypeStruct((m // tm * nseq, FFN_CONV - 1, D_FF), F32),
    ]
    scratch = [pltpu.VMEM((tm, D_MODEL), BF16)]
    if tiles_per_seq > 1:
        scratch += [pltpu.VMEM((nf, FFN_CONV - 1, tf), F32), pltpu.VMEM((nf, FFN_CONV - 1, tf), F32),
                    pltpu.VMEM((8 + tm, tf), F32), pltpu.VMEM((8 + tm, tf), F32)]
    y, nbv, nbg = pl.pallas_call(
        functools.partial(_ffn_kernel, nseq=nseq, seg=seg, tiles_per_seq=tiles_per_seq, final_norm=final_norm),
        grid=(m // tm, nf), in_specs=in_specs, out_specs=out_specs, out_shape=out_shape,
        scratch_shapes=scratch, compiler_params=_params(2),
        name="conv_ffn",
    )(x, nw, w_up, conv_w, conv_w, conv_b, conv_b, w_down, buf, buf, final_w)
    new_buf = jnp.concatenate([nbv, nbg], axis=-1)
    return y, new_buf[tiles_per_seq - 1::tiles_per_seq]


def _s5_disc_kernel(lre_ref, lim_ref, ldt_ref, bre_ref, bim_ref, are_ref, aim_ref, bbre_ref, bbim_ref):
    lam_re = lre_ref[...]
    lam_im = lim_ref[...]
    dt = jnp.exp(ldt_ref[...])
    mag = jnp.exp(lam_re * dt)
    ph = lam_im * dt
    ab_re = mag * jnp.cos(ph)
    ab_im = mag * jnp.sin(ph)
    den = lam_re * lam_re + lam_im * lam_im
    cf_re = ((ab_re - 1.0) * lam_re + ab_im * lam_im) / den
    cf_im = (ab_im * lam_re - (ab_re - 1.0) * lam_im) / den
    are_ref[...] = ab_re
    aim_ref[...] = ab_im
    b_re = bre_ref[...]
    b_im = bim_ref[...]
    bbre_ref[...] = cf_re * b_re - cf_im * b_im
    bbim_ref[...] = cf_re * b_im + cf_im * b_re


def _s5_disc(lam_re, lam_im, log_dt, b_re_t, b_im_t):
    g, c, p = b_re_t.shape
    tile = lambda t: jnp.tile(t, (1, c))
    shp = jax.ShapeDtypeStruct((g, c * p), F32)
    a_re, a_im, bb_re, bb_im = pl.pallas_call(
        _s5_disc_kernel, out_shape=[shp, shp, shp, shp], name="s5_disc",
    )(tile(lam_re), tile(lam_im), jnp.broadcast_to(log_dt, (g, c * p)),
      b_re_t.reshape(g, c * p), b_im_t.reshape(g, c * p))
    return a_re[:, :p], a_im[:, :p], bb_re.reshape(g, c, p), bb_im.reshape(g, c, p)


def _gelu(y):
    return y * (0.5 * (1.0 + jnp.tanh(GELU_C * (y + 0.044715 * (y * y * y)))))


def _s5_seq_kernel(x_ref, nw_ref, perm_ref, back_ref, wb_ref, wc_ref, are_ref, aim_ref, d_ref, h0r_ref, h0i_ref,
                   g_ref, hr_ref, hi_ref, st_scr, u_scr, ub_scr, x_scr, *, nseq, seg, chained):
    wrows = 2 * nseq * seg

    def load_state():
        st_scr[0:nseq, :] = h0r_ref[...]
        st_scr[nseq:2 * nseq, :] = h0i_ref[...]

    nw = nw_ref[...]
    if chained:
        pl.when(pl.program_id(0) == 0)(load_state)
        x = x_ref[...]
        u = jnp.concatenate([_rms_rows(x[:, s * D_MODEL:(s + 1) * D_MODEL], nw) for s in range(nseq)], axis=0)
    else:
        load_state()
        u = _rms_rows(x_ref[...], nw)
    u_hi = u.astype(BF16)
    u_lo = (u - u_hi.astype(F32)).astype(BF16)
    perm = perm_ref[...]
    cw = 512
    for c in range(D_MODEL // cw):
        csl = slice(c * cw, (c + 1) * cw)
        hi = _dot(perm, u_hi[:, csl])
        u_scr[:, csl] = hi + _dot(perm, u_lo[:, csl])
        ub_scr[:, csl] = hi.astype(BF16)

    row = lax.broadcasted_iota(jnp.int32, (wrows, 1), 0)
    is_re = (row & (2 * nseq - 1)) < nseq
    lpb = S5_BW // LANES
    tile = 2 * nseq
    sign = jnp.where(lax.broadcasted_iota(jnp.int32, (tile, 1), 0) < nseq, -1.0, 1.0)

    def project_in(kb):
        blk = ub_scr[:, kb * LANES:(kb + 1) * LANES]
        zero = jnp.zeros_like(blk)
        lhs = jnp.concatenate([jnp.where(is_re, blk, zero), jnp.where(is_re, zero, blk)], axis=1)
        res = _dot(lhs, wb_ref[kb])
        for q in range(lpb):
            x_scr[kb * lpb + q] = res[:, q * LANES:(q + 1) * LANES]

    def scan(kb):
        blocks = range(kb * lpb, (kb + 1) * lpb)
        lane = [slice(lb * LANES, (lb + 1) * LANES) for lb in blocks]
        a1 = [jnp.broadcast_to(are_ref[:, sl], (tile, LANES)) for sl in lane]
        a2 = [jnp.broadcast_to(aim_ref[:, sl], (tile, LANES)) * sign for sl in lane]
        st = [st_scr[:, sl] for sl in lane]
        for t in range(seg):
            for n, lb in enumerate(blocks):
                st[n] = a1[n] * st[n] + a2[n] * pltpu.roll(st[n], nseq, 0) + x_scr[lb, t * tile:(t + 1) * tile, :]
                x_scr[lb, t * tile:(t + 1) * tile, :] = st[n]
        for n, sl in enumerate(lane):
            st_scr[:, sl] = st[n]

    def read_out(kb):
        xb = jnp.concatenate([x_scr[kb * lpb + q] for q in range(lpb)], axis=1).astype(BF16)
        y2 = _dot(xb, wc_ref[kb])
        lsl = slice(kb * LANES, (kb + 1) * LANES)
        y = y2[:, :LANES] - pltpu.roll(y2[:, LANES:], wrows - nseq, 0) + d_ref[:, lsl] * u_scr[:, lsl]
        ub_scr[:, lsl] = _gelu(y).astype(BF16)

    for kb in range(S5_NB + 2):
        if kb < S5_NB:
            project_in(kb)
        if 1 <= kb <= S5_NB:
            scan(kb - 1)
        if kb >= 2:
            read_out(kb - 2)
    hr_ref[...] = st_scr[0:nseq, :]
    hi_ref[...] = st_scr[nseq:2 * nseq, :]

    back = back_ref[...]
    for c in range(D_MODEL // cw):
        csl = slice(c * cw, (c + 1) * cw)
        g_nat = _dot(back, ub_scr[:, csl])
        if chained:
            for s in range(nseq):
                g_ref[s, :, csl] = g_nat[s * seg:(s + 1) * seg].astype(BF16)
        else:
            g_ref[:, csl] = g_nat.astype(BF16)


def _s5_seq(x, nw, wb2, wc2, a_re, a_im, d, h0_re, h0_im, *, nseq, seg, chained):
    wrows = 2 * nseq * seg
    perm, back = _s5_row_perm(nseq, seg)
    const2 = lambda n: (0, 0)
    const3 = lambda n: (0, 0, 0)
    if chained:
        steps = x.shape[0] // seg
        nstate = nseq
        x_spec = pl.BlockSpec((seg, nseq * D_MODEL), lambda n: (n, 0))
        g_spec = pl.BlockSpec((nseq, seg, D_MODEL), lambda n: (0, n, 0))
        g_shape = (nseq, x.shape[0], D_MODEL)
        s_spec = pl.BlockSpec((nseq, S5_STATE), const2)
    else:
        steps = x.shape[0] // (nseq * seg)
        nstate = steps * nseq
        x_spec = pl.BlockSpec((nseq * seg, D_MODEL), lambda n: (n, 0))
        g_spec = x_spec
        g_shape = x.shape
        s_spec = pl.BlockSpec((nseq, S5_STATE), lambda n: (n, 0))
    return pl.pallas_call(
        functools.partial(_s5_seq_kernel, nseq=nseq, seg=seg, chained=chained),
        grid=(steps,),
        in_specs=[
            x_spec,
            pl.BlockSpec((1, D_MODEL), const2),
            pl.BlockSpec((wrows, nseq * seg), const2),
            pl.BlockSpec((nseq * seg, wrows), const2),
            pl.BlockSpec((S5_NB, 2 * LANES, S5_BW), const3),
            pl.BlockSpec((S5_NB, S5_BW, 2 * LANES), const3),
            pl.BlockSpec((1, S5_STATE), const2),
            pl.BlockSpec((1, S5_STATE), const2),
            pl.BlockSpec((1, D_MODEL), const2),
            s_spec,
            s_spec,
        ],
        out_specs=[g_spec, s_spec, s_spec],
        out_shape=[jax.ShapeDtypeStruct(g_shape, BF16),
                   jax.ShapeDtypeStruct((nstate, S5_STATE), F32),
                   jax.ShapeDtypeStruct((nstate, S5_STATE), F32)],
        scratch_shapes=[pltpu.VMEM((2 * nseq, S5_STATE), F32),
                        pltpu.VMEM((wrows, D_MODEL), F32),
                        pltpu.VMEM((wrows, D_MODEL), BF16),
                        pltpu.VMEM((S5_STATE // LANES, wrows, LANES), F32)],
        compiler_params=_params(1),
        name="s5_seq" if chained else "s5_step",
    )(x, nw, perm, back, wb2, wc2, a_re, a_im, d, h0_re, h0_im)


def _s5_row_perm(nseq, seg):
    r = np.arange(2 * nseq * seg)
    src = (r % nseq) * seg + r // (2 * nseq)
    hit = src[:, None] == np.arange(nseq * seg)[None, :]
    real = (r % (2 * nseq)) < nseq
    return jnp.asarray(hit, BF16), jnp.asarray((hit & real[:, None]).T, BF16)


def _glu_kernel(x_ref, g_ref, wv_ref, wg_ref, o_ref):
    gb = g_ref[...]
    val = _dot(gb, wv_ref[...].astype(BF16))
    o_ref[...] = x_ref[...] + val * _sigmoid(_dot(gb, wg_ref[...].astype(BF16)))


def _glu(x, g, w_glu, *, time_major_seqs=0, tm=1024, tn=512):
    nn = D_MODEL // tn
    if time_major_seqs:
        tps = x.shape[0] // tm
        n_tiles = tps * time_major_seqs
        xmap = lambda i, j: (i % tps, (i // tps) * nn + j)
    else:
        n_tiles = x.shape[0] // tm
        xmap = lambda i, j: (i, j)
    return pl.pallas_call(
        _glu_kernel,
        grid=(n_tiles, nn),
        in_specs=[
            pl.BlockSpec((tm, tn), xmap),
            pl.BlockSpec((tm, D_MODEL), lambda i, j: (i, 0)),
            pl.BlockSpec((None, D_MODEL, tn), lambda i, j: (0, 0, j)),
            pl.BlockSpec((None, D_MODEL, tn), lambda i, j: (0, 0, nn + j)),
        ],
        out_specs=pl.BlockSpec((tm, tn), xmap),
        out_shape=jax.ShapeDtypeStruct(x.shape, F32),
        compiler_params=_params(2),
        name="glu",
    )(x, g, w_glu, w_glu)


def _rotary_tables(pos):
    half = RET_DK // 2
    inv = ROPE_BASE ** (-jnp.arange(half, dtype=F32) / half)
    ang = pos[:, None] * inv[None, :]
    cos = jnp.cos(ang)
    sin = jnp.sin(ang)
    return jnp.concatenate([cos, cos], axis=-1), jnp.concatenate([-sin, sin], axis=-1)


def _retention_tables(nseq, seg):
    log_g = np.log1p(-np.exp2(-5.0 - np.arange(RET_HEADS)))
    row = np.arange(nseq * seg)
    t = (row % seg).astype(np.float64)
    sid = row // seg
    diff = t[:, None] - t[None, :]
    ok = (sid[:, None] == sid[None, :]) & (diff >= 0)
    dec = np.where(ok, np.exp(log_g[:, None, None] * np.where(ok, diff, 0.0)), 0.0)
    qd = np.broadcast_to(np.exp(log_g[:, None] * (t + 1.0))[..., None], (RET_HEADS, nseq * seg, RET_DK))
    kd = np.broadcast_to(np.exp(log_g[:, None] * (seg - 1.0 - t))[..., None], (RET_HEADS, nseq * seg, RET_DK))
    gains = tuple(float(v) for v in np.exp(log_g * seg))
    return jnp.asarray(dec, F32), jnp.asarray(qd, F32), jnp.asarray(kd, F32), gains


def _block_diag(blocks, rows_per, cols_per):
    tiled = jnp.tile(blocks.reshape(S5_NB, S5_GB * rows_per, cols_per), (1, 1, S5_GB))
    rg = np.arange(S5_GB * rows_per)[:, None] // rows_per
    cg = np.arange(S5_GB * cols_per)[None, :] // cols_per
    return (tiled * jnp.asarray(rg == cg, F32)).astype(BF16)


def _trunk(x3, pos, gdn_s, gdn_cb, ret_s, s5_re, s5_im, ffn_cb, prm, *, chained):
    b, l, _ = x3.shape
    m = b * l
    x = x3.reshape(m, D_MODEL)
    nchunk = l // CHUNK_ROWS if chained else 1

    proj, gate = _in_proj(x, prm['norm_mix_w'][0:1], prm['w_in_main'], prm['w_in_gate'])
    o_a, gdn_new, gcb_new = _gdn(proj, gate, gdn_cb, prm['gdn_conv_w'], prm['alog_row'], prm['dtb_row'],
                                 prm['gdn_norm_w'], None if chained else gdn_s, nbatch=b, nchunk=nchunk)
    if chained:
        cosf, sinf = _rotary_tables(pos)
        dec, qd, kd, g_chunk = _retention_tables(1, CHUNK_ROWS)
    else:
        cosf, sinf = _rotary_tables(jnp.tile(pos, CHUNK_ROWS // l))
        dec, qd, kd, g_chunk = _retention_tables(CHUNK_ROWS // l, l)
    o_b, ret_new = _ret(proj, cosf, sinf, dec, qd, kd, prm['ret_gn_w'], prm['ret_gn_b'],
                        None if chained else ret_s, g_chunk, nbatch=b, nchunk=nchunk)
    x = _out_proj(x, o_a, o_b, prm['w_out_a'], prm['w_out_b'])
    x, fcb0 = _ffn(x, prm['norm_ffn_w'][0:1], prm['w_up'], prm['ffn_conv_w'], prm['ffn_conv_b'],
                   prm['w_down'], ffn_cb, prm['norm_final_w'], layer=0, nseq_total=b, final_norm=False,
                   out_time_major=chained)

    s5_args = (prm['norm_mix_w'][1:2], prm['s5_wb2'], prm['s5_wc2'], prm['s5_a_re'], prm['s5_a_im'],
               prm['s5_d'], s5_re, s5_im)
    if chained:
        g, s5r_new, s5i_new = _s5_seq(x, *s5_args, nseq=b, seg=CHUNK_ROWS, chained=True)
        x = _glu(x, g.reshape(m, D_MODEL), prm['w_glu'], time_major_seqs=b)
    else:
        g, s5r_new, s5i_new = _s5_seq(x, *s5_args, nseq=CHUNK_ROWS // l, seg=l, chained=False)
        x = _glu(x, g, prm['w_glu'])
    y, fcb1 = _ffn(x, prm['norm_ffn_w'][1:2], prm['w_up'], prm['ffn_conv_w'], prm['ffn_conv_b'],
                   prm['w_down'], ffn_cb, prm['norm_final_w'], layer=1, nseq_total=b, final_norm=True,
                   x_time_major=chained)
    return (y.reshape(b, l, D_MODEL), gdn_new[None], gcb_new[None], ret_new[None],
            s5r_new.reshape(1, b, S5_GROUPS, S5_P), s5i_new.reshape(1, b, S5_GROUPS, S5_P),
            jnp.stack([fcb0, fcb1]))


def kernel(x_prompt, x_sample, state_gdn, state_gdn_conv, state_ret, state_s5_re, state_s5_im,
           state_ffn_conv, norm_mix_w, norm_ffn_w, norm_final_w,
           w_in, gdn_conv_w, gdn_a_log, gdn_dt_bias, gdn_norm_w, ret_gn_w, ret_gn_b, w_out,
           s5_lam_re, s5_lam_im, s5_log_dt, s5_b_re, s5_b_im, s5_c_re, s5_c_im, s5_d, w_glu,
           w_up, ffn_conv_w, ffn_conv_b, w_down):
    bp, lp, _ = x_prompt.shape
    bs, ls, _ = x_sample.shape
    past_len = 16384

    w_in_main, w_in_gate = _w_in_layout(w_in)
    pad_row = lambda v: jnp.pad(v.reshape(1, -1), ((0, 0), (0, GATE_W - v.shape[-1])))

    a_re, a_im, bb_re, bb_im = _s5_disc(s5_lam_re[0], s5_lam_im[0], s5_log_dt[0].reshape(S5_GROUPS, 1),
                                        s5_b_re[0].transpose(0, 2, 1), s5_b_im[0].transpose(0, 2, 1))
    s5_wb2 = jnp.concatenate([_block_diag(bb_re, S5_GROUP, S5_P),
                              _block_diag(bb_im, S5_GROUP, S5_P)], axis=1)
    s5_wc2 = jnp.concatenate([_block_diag(s5_c_re[0].transpose(0, 2, 1), S5_P, S5_GROUP),
                              _block_diag(s5_c_im[0].transpose(0, 2, 1), S5_P, S5_GROUP)], axis=-1)

    prm = dict(
        norm_mix_w=norm_mix_w, norm_ffn_w=norm_ffn_w, norm_final_w=norm_final_w.reshape(1, D_MODEL),
        w_in_main=w_in_main, w_in_gate=w_in_gate,
        gdn_conv_w=gdn_conv_w[0], alog_row=pad_row(gdn_a_log[0]), dtb_row=pad_row(gdn_dt_bias[0]),
        gdn_norm_w=gdn_norm_w[0].reshape(1, GDN_DV),
        ret_gn_w=ret_gn_w[0].reshape(1, RET_V_W), ret_gn_b=ret_gn_b[0].reshape(1, RET_V_W),
        w_out_a=w_out[0, :GDN_V_W].astype(BF16), w_out_b=w_out[0, GDN_V_W:].astype(BF16),
        s5_wb2=s5_wb2, s5_wc2=s5_wc2,
        s5_a_re=a_re.reshape(1, S5_STATE), s5_a_im=a_im.reshape(1, S5_STATE), s5_d=s5_d[0].reshape(1, D_MODEL),
        w_glu=w_glu,
        w_up=_w_up_prep(w_up), ffn_conv_w=ffn_conv_w, ffn_conv_b=ffn_conv_b[:, None, :],
        w_down=w_down,
    )

    z_gcb = jnp.zeros((bp, GDN_CONV - 1, GDN_CONV_W), F32)
    z_s5 = jnp.zeros((bp, S5_STATE), F32)
    z_fcb = jnp.zeros((2, bp, FFN_CONV - 1, 2 * D_FF), F32)
    pos_p = jnp.arange(lp, dtype=F32)
    pos_s = past_len + jnp.arange(ls, dtype=F32)

    y_p, gdn_p, gcb_p, ret_p, s5r_p, s5i_p, fcb_p = _trunk(
        x_prompt, pos_p, None, z_gcb, None, z_s5, z_s5, z_fcb, prm, chained=True)
    y_s, gdn_s, gcb_s, ret_s, s5r_s, s5i_s, fcb_s = _trunk(
        x_sample, pos_s, state_gdn[0], state_gdn_conv[0], state_ret[0],
        state_s5_re[0].reshape(bs, S5_STATE), state_s5_im[0].reshape(bs, S5_STATE),
        state_ffn_conv, prm, chained=False)
    return (y_p, y_s, gdn_p, gdn_s, gcb_p, gcb_s, ret_p, ret_s,
            s5r_p, s5r_s, s5i_p, s5i_s, fcb_p, fcb_s)
```
